```python
import math
import jax, jax.numpy as jnp
from jax import lax
import numpy as np

D_MODEL = 1024
BATCH = 8
SEQ = 4096
DEPTH = 2

N_BRANCH = 4
BRANCH_W = D_MODEL // 4
POOL_WINDOWS = (2, 4, 8, 16)
POOL_GW = BRANCH_W // len(POOL_WINDOWS)
CONV_W = 3
DA_HEADS = 4
DA_HEAD_DIM = BRANCH_W // (2 * DA_HEADS)
DA_V_DIM = 2 * DA_HEAD_DIM
ROPE_THETA = 10000.0
Q_BLOCK = 128
SG_CHUNK = 128
SG_GROUPS = 4
SG_GW = BRANCH_W // SG_GROUPS
N_GROUPS = 4
EXPERTS_PER_GROUP = 4
N_EXPERTS = N_GROUPS * EXPERTS_PER_GROUP
TOP_K = 2
EXPERT_HIDDEN = 256
IN_COLS = BRANCH_W + 3 * BRANCH_W + 3 * BRANCH_W + 2 * BRANCH_W + N_BRANCH * D_MODEL
LN_EPS = 1e-5
NEG_INF = -1e30
DEEPNORM_ALPHA = (2 * DEPTH) ** 0.25
DEEPNORM_BETA = (8 * DEPTH) ** -0.25

kernel_name = 'hybrid_gated_pool_conv_diffattn_sgmlp_hmoe'


def layer_norm(x, g, b):
    xf = x.astype(jnp.float32)
    mu = jnp.mean(xf, axis=-1, keepdims=True)
    var = jnp.mean(jnp.square(xf - mu), axis=-1, keepdims=True)
    return ((xf - mu) * lax.rsqrt(var + LN_EPS)).astype(x.dtype) * g + b


def rotary(x, cos, sin):
    half = x.shape[-1] // 2
    c = cos[None, :, None, None, :]
    s = sin[None, :, None, None, :]
    xf = x.astype(jnp.float32)
    x1, x2 = xf[..., :half], xf[..., half:]
    return jnp.concatenate([x1 * c - x2 * s, x2 * c + x1 * s], axis=-1).astype(x.dtype)


def causal_multiscale_pool(a, pool_w, pool_scale):
    S = a.shape[1]
    af = a.astype(jnp.float32)
    csum = jnp.pad(jnp.cumsum(af, axis=1), ((0, 0), (1, 0), (0, 0)))
    t = jnp.arange(S)
    pooled = []
    for g, w in enumerate(POOL_WINDOWS):
        cg = csum[..., g * POOL_GW:(g + 1) * POOL_GW]
        lo = jnp.maximum(t + 1 - w, 0)
        win_sum = cg[:, 1:] - jnp.take(cg, lo, axis=1)
        count = jnp.minimum(t + 1, w).astype(jnp.float32)
        pooled.append(win_sum / count[None, :, None])
    pooled = jnp.stack(pooled, axis=2)
    d = (pooled - af.reshape(pooled.shape)).astype(a.dtype)
    y = jnp.einsum('bsgc,gcd->bsgd', d, pool_w)
    return y.reshape(a.shape) * pool_scale


def short_gated_conv(gb, gc, h, conv_w):
    S = h.shape[1]
    z = jnp.pad(gc * h, ((0, 0), (CONV_W - 1, 0), (0, 0)))
    y = z[:, 0:S] * conv_w[0]
    for j in range(1, CONV_W):
        y = y + z[:, j:j + S] * conv_w[j]
    return gb * y


def differential_attention(q, k, v, cos, sin, lam, lam_init, subln_g):
    q = rotary(q, cos, sin)
    k = rotary(k, cos, sin)
    S = q.shape[1]
    scale = DA_HEAD_DIM ** -0.5
    outs = []
    for qb in range(S // Q_BLOCK):
        q0 = qb * Q_BLOCK
        kv = q0 + Q_BLOCK
        s = jnp.einsum('bqhmd,bkhmd->bhmqk', q[:, q0:kv], k[:, :kv]).astype(jnp.float32) * scale
        causal = (q0 + jnp.arange(Q_BLOCK))[:, None] >= jnp.arange(kv)[None, :]
        p = jax.nn.softmax(jnp.where(causal, s, NEG_INF), axis=-1)
        a = p[:, :, 0] - lam * p[:, :, 1]
        outs.append(jnp.einsum('bhqk,bkhe->bqhe', a.astype(v.dtype), v[:, :kv]))
    o = jnp.concatenate(outs, axis=1).astype(jnp.float32)
    o = o * lax.rsqrt(jnp.mean(o * o, axis=-1, keepdims=True) + LN_EPS)
    o = (o * (1.0 - lam_init)).astype(v.dtype) * subln_g
    return o.reshape(o.shape[0], S, DA_HEADS * DA_V_DIM)


def chunked_spatial_gating(u, v, ln_g, ln_b, w_s, b_s):
    B, S, _ = v.shape
    v = layer_norm(v, ln_g, ln_b)
    vc = v.reshape(B, S // SG_CHUNK, SG_CHUNK, SG_GROUPS, SG_GW)
    mask = jnp.tril(jnp.ones((SG_CHUNK, SG_CHUNK), dtype=bool))
    ws = jnp.where(mask[None], w_s, jnp.zeros_like(w_s))
    y = jnp.einsum('gts,bnsgc->bntgc', ws, vc) + b_s.T[None, None, :, :, None]
    return u * y.reshape(B, S, BRANCH_W)


def token_mixer(x, cos, sin, lam_init, w_in, pool_w, pool_scale, conv_w, lam_q1, lam_k1, lam_q2, lam_k2,
                subln_g, sg_ln_g, sg_ln_b, sg_w, sg_b, w_branch, w_o):
    B, S, _ = x.shape
    W = BRANCH_W
    proj = x @ w_in
    p_pool, p_conv, p_attn, p_sg, p_gate = jnp.split(proj, [W, 4 * W, 7 * W, 9 * W], axis=-1)
    y_a = causal_multiscale_pool(p_pool, pool_w, pool_scale)
    gb, gc, h = jnp.split(p_conv, 3, axis=-1)
    y_b = short_gated_conv(gb, gc, h, conv_w)
    q, k, v = jnp.split(p_attn, 3, axis=-1)
    q = q.reshape(B, S, DA_HEADS, 2, DA_HEAD_DIM)
    k = k.reshape(B, S, DA_HEADS, 2, DA_HEAD_DIM)
    v = v.reshape(B, S, DA_HEADS, DA_V_DIM)
    lam = (jnp.exp(jnp.sum(lam_q1.astype(jnp.float32) * lam_k1.astype(jnp.float32)))
           - jnp.exp(jnp.sum(lam_q2.astype(jnp.float32) * lam_k2.astype(jnp.float32))) + lam_init)
    y_c = differential_attention(q, k, v, cos, sin, lam, lam_init, subln_g)
    uv = jax.nn.gelu(p_sg)
    u, vv = jnp.split(uv, 2, axis=-1)
    y_d = chunked_spatial_gating(u, vv, sg_ln_g, sg_ln_b, sg_w, sg_b)
    gates = jax.nn.sigmoid(p_gate.reshape(B, S, N_BRANCH, D_MODEL))
    branches = (y_a, y_b, y_c, y_d)
    merged = gates[:, :, 0] * (branches[0] @ w_branch[0])
    for i in range(1, N_BRANCH):
        merged = merged + gates[:, :, i] * (branches[i] @ w_branch[i])
    return merged @ w_o


def hierarchical_moe(x, w_rg, b_rg, w_re, b_re, w_gate, w_up, w_down):
    gl = (x @ w_rg + b_rg).astype(jnp.float32)
    pg = jax.nn.softmax(gl, axis=-1)
    g_sel = jnp.argmax(gl, axis=-1)
    p_sel = jnp.take_along_axis(pg, g_sel[..., None], axis=-1)
    el = (jnp.einsum('bsd,gde->bsge', x, w_re) + b_re).astype(jnp.float32)
    el_sel = jnp.take_along_axis(el, g_sel[..., None, None], axis=2)[:, :, 0]
    top_v, top_i = lax.top_k(el_sel, TOP_K)
    top_w = jax.nn.softmax(top_v, axis=-1) * p_sel
    expert_id = g_sel[..., None] * EXPERTS_PER_GROUP + top_i
    gate = jnp.sum(jax.nn.one_hot(expert_id, N_EXPERTS, dtype=jnp.float32) * top_w[..., None], axis=-2)
    gate = gate.astype(x.dtype)
    hg = jnp.einsum('bsd,edh->bseh', x, w_gate)
    hu = jnp.einsum('bsd,edh->bseh', x, w_up)
    act = jax.nn.silu(hg) * hu * gate[..., None]
    return jnp.einsum('bseh,ehd->bsd', act, w_down)


def setup_inputs(seed: int = 0) -> dict:
    key = jax.random.key(seed)
    ks = jax.random.split(key, 32)
    f32 = jnp.float32
    L = DEPTH

    def nrm(k, shape, scale):
        return jax.random.normal(k, shape, f32) * scale

    return {
        'x': nrm(ks[0], (BATCH, SEQ, D_MODEL), 1.0),
        'w_in': nrm(ks[1], (L, D_MODEL, IN_COLS), D_MODEL ** -0.5),
        'pool_w': nrm(ks[2], (L, len(POOL_WINDOWS), POOL_GW, POOL_GW), POOL_GW ** -0.5),
        'pool_scale': 1.0 + nrm(ks[3], (L, BRANCH_W), 0.02),
        'conv_w': nrm(ks[4], (L, CONV_W, BRANCH_W), CONV_W ** -0.5),
        'lam_q1': nrm(ks[5], (L, DA_HEAD_DIM), 0.1),
        'lam_k1': nrm(ks[6], (L, DA_HEAD_DIM), 0.1),
        'lam_q2': nrm(ks[7], (L, DA_HEAD_DIM), 0.1),
        'lam_k2': nrm(ks[8], (L, DA_HEAD_DIM), 0.1),
        'subln_g': 1.0 + nrm(ks[9], (L, DA_V_DIM), 0.02),
        'sg_ln_g': 1.0 + nrm(ks[10], (L, BRANCH_W), 0.02),
        'sg_ln_b': nrm(ks[11], (L, BRANCH_W), 0.02),
        'sg_w': nrm(ks[12], (L, SG_GROUPS, SG_CHUNK, SG_CHUNK), SG_CHUNK ** -0.5),
        'sg_b': 1.0 + nrm(ks[13], (L, SG_GROUPS, SG_CHUNK), 0.02),
        'w_branch': nrm(ks[14], (L, N_BRANCH, BRANCH_W, D_MODEL), BRANCH_W ** -0.5),
        'w_o': nrm(ks[15], (L, D_MODEL, D_MODEL), D_MODEL ** -0.5 * DEEPNORM_BETA),
        'ln1_g': 1.0 + nrm(ks[16], (L, D_MODEL), 0.02),
        'ln1_b': nrm(ks[17], (L, D_MODEL), 0.02),
        'w_rg': nrm(ks[18], (L, D_MODEL, N_GROUPS), D_MODEL ** -0.5),
        'b_rg': nrm(ks[19], (L, N_GROUPS), 0.01),
        'w_re': nrm(ks[20], (L, N_GROUPS, D_MODEL, EXPERTS_PER_GROUP), D_MODEL ** -0.5),
        'b_re': nrm(ks[21], (L, N_GROUPS, EXPERTS_PER_GROUP), 0.01),
        'w_gate': nrm(ks[22], (L, N_EXPERTS, D_MODEL, EXPERT_HIDDEN), D_MODEL ** -0.5),
        'w_up': nrm(ks[23], (L, N_EXPERTS, D_MODEL, EXPERT_HIDDEN), D_MODEL ** -0.5),
        'w_down': nrm(ks[24], (L, N_EXPERTS, EXPERT_HIDDEN, D_MODEL), EXPERT_HIDDEN ** -0.5 * DEEPNORM_BETA),
        'ln2_g': 1.0 + nrm(ks[25], (L, D_MODEL), 0.02),
        'ln2_b': nrm(ks[26], (L, D_MODEL), 0.02),
    }


def reference(x, w_in, pool_w, pool_scale, conv_w, lam_q1, lam_k1, lam_q2, lam_k2, subln_g, sg_ln_g, sg_ln_b,
              sg_w, sg_b, w_branch, w_o, ln1_g, ln1_b, w_rg, b_rg, w_re, b_re, w_gate, w_up, w_down, ln2_g, ln2_b):
    S = x.shape[1]
    half = DA_HEAD_DIM // 2
    inv_freq = ROPE_THETA ** (-jnp.arange(half, dtype=jnp.float32) / half)
    ang = jnp.arange(S, dtype=jnp.float32)[:, None] * inv_freq[None, :]
    cos, sin = jnp.cos(ang), jnp.sin(ang)
    for l in range(DEPTH):
        lam_init = 0.8 - 0.6 * math.exp(-0.3 * l)
        mix = token_mixer(x, cos, sin, lam_init, w_in[l], pool_w[l], pool_scale[l], conv_w[l],
                          lam_q1[l], lam_k1[l], lam_q2[l], lam_k2[l], subln_g[l], sg_ln_g[l], sg_ln_b[l],
                          sg_w[l], sg_b[l], w_branch[l], w_o[l])
        x = layer_norm(DEEPNORM_ALPHA * x + mix, ln1_g[l], ln1_b[l])
        ffn = hierarchical_moe(x, w_rg[l], b_rg[l], w_re[l], b_re[l], w_gate[l], w_up[l], w_down[l])
        x = layer_norm(DEEPNORM_ALPHA * x + ffn, ln2_g[l], ln2_b[l])
    return x
```

```python
import functools
import math

import jax
import jax.numpy as jnp
from jax import lax
from jax.experimental import pallas as pl
from jax.experimental.pallas import tpu as pltpu

F32 = jnp.float32
BF16 = jnp.bfloat16

BRANCH_W = 256
POOL_WINDOWS = (2, 4, 8, 16)
POOL_GW = 64
MAX_POOL = 16
CONV_W = 3
DA_HEADS = 4
DA_HEAD_DIM = 32
DA_V_DIM = 64
ROPE_THETA = 10000.0
SG_CHUNK = 128
SG_GROUPS = 4
SG_GW = 64
N_GROUPS = 4
EXPERTS_PER_GROUP = 4
N_EXPERTS = 16
EXPERT_HIDDEN = 256
GROUP_HIDDEN = EXPERTS_PER_GROUP * EXPERT_HIDDEN
N_BRANCH = 4
LN_EPS = 1e-5
NEG_INF = -1e30
ROUTER_LANES = 128
V7X_VMEM_LIMIT = 56 * 1024 * 1024

COL_POOL = 0
COL_CONV = BRANCH_W
COL_ATTN = 4 * BRANCH_W
COL_SG = 7 * BRANCH_W
COL_GATE = 9 * BRANCH_W


def _layer_norm(h, g, b):
    mu = jnp.mean(h, axis=-1, keepdims=True)
    hc = h - mu
    var = jnp.mean(hc * hc, axis=-1, keepdims=True)
    return hc * lax.rsqrt(var + LN_EPS) * g + b


def _gelu_tanh(x):
    c = math.sqrt(2.0 / math.pi)
    return 0.5 * x * (1.0 + jnp.tanh(c * (x + 0.044715 * (x * x * x))))


def _local_mixer_kernel(x_ref, w_ref, cos_ref, sin_ref, poolw_ref, pscale_ref, convw_ref, lng_ref, lnb_ref,
                        sgw_ref, sgb_ref, yloc_ref, q_ref, k_ref, v_ref, pext_ref, zext_ref):
    t = pl.program_id(1)
    ta = x_ref.shape[0]
    w = BRANCH_W
    xb = x_ref[...].astype(BF16)
    lane = lax.broadcasted_iota(jnp.int32, (ta, w), 1)
    row = lax.broadcasted_iota(jnp.int32, (ta, w), 0)

    @pl.when(t == 0)
    def _():
        pext_ref[0:MAX_POOL, :] = jnp.zeros((MAX_POOL, w), F32)
        zext_ref[0:8, :] = jnp.zeros((8, w), F32)

    @pl.when(t > 0)
    def _():
        pext_ref[0:MAX_POOL, :] = pext_ref[ta:ta + MAX_POOL, :]
        zext_ref[0:8, :] = zext_ref[ta:ta + 8, :]

    p = jnp.dot(xb, w_ref[:, COL_POOL:COL_POOL + w], preferred_element_type=F32)
    pext_ref[MAX_POOL:MAX_POOL + ta, :] = p

    def prev(kk):
        return pext_ref[pl.ds(MAX_POOL - kk, ta), :]

    s2 = p + prev(1)
    s4 = s2 + (prev(2) + prev(3))
    s8 = s4 + ((prev(4) + prev(5)) + (prev(6) + prev(7)))
    s16 = s8 + (((prev(8) + prev(9)) + (prev(10) + prev(11))) + ((prev(12) + prev(13)) + (prev(14) + prev(15))))
    grp = lane // POOL_GW
    win_sum = jnp.where(grp == 0, s2, jnp.where(grp == 1, s4, jnp.where(grp == 2, s8, s16)))
    win = jnp.where(grp == 0, 2, jnp.where(grp == 1, 4, jnp.where(grp == 2, 8, 16)))
    count = jnp.minimum(t * ta + row + 1, win).astype(F32)
    d = (win_sum / count - p).astype(BF16)
    y_a = jnp.dot(d, poolw_ref[...], preferred_element_type=F32) * pscale_ref[...]
    yloc_ref[:, 0:w] = y_a.astype(BF16)

    pc = jnp.dot(xb, w_ref[:, COL_CONV:COL_CONV + 3 * w], preferred_element_type=F32)
    gb = pc[:, 0:w]
    z = pc[:, w:2 * w] * pc[:, 2 * w:3 * w]
    zext_ref[8:8 + ta, :] = z
    cw = convw_ref[...]
    y_b = zext_ref[pl.ds(6, ta), :] * cw[0:1, :] + zext_ref[pl.ds(7, ta), :] * cw[1:2, :] + z * cw[2:3, :]
    yloc_ref[:, w:2 * w] = (gb * y_b).astype(BF16)

    pa = jnp.dot(xb, w_ref[:, COL_ATTN:COL_ATTN + 3 * w], preferred_element_type=F32)
    cos = cos_ref[...]
    sin = sin_ref[...]
    first_half = (lane % DA_HEAD_DIM) < (DA_HEAD_DIM // 2)

    def rope(u):
        swapped = jnp.where(first_half, pltpu.roll(u, w - DA_HEAD_DIM // 2, axis=1),
                            pltpu.roll(u, DA_HEAD_DIM // 2, axis=1))
        return u * cos + swapped * sin

    q_ref[...] = (rope(pa[:, 0:w]) * (DA_HEAD_DIM ** -0.5)).astype(BF16)
    k_ref[...] = rope(pa[:, w:2 * w]).astype(BF16)
    v_ref[...] = pa[:, 2 * w:3 * w].astype(BF16)

    uv = _gelu_tanh(jnp.dot(xb, w_ref[:, COL_SG:COL_SG + 2 * w], preferred_element_type=F32))
    u = uv[:, 0:w]
    vn = _layer_norm(uv[:, w:2 * w], lng_ref[...], lnb_ref[...])
    wrow = lax.broadcasted_iota(jnp.int32, (SG_CHUNK, SG_GROUPS * SG_CHUNK), 0)
    wcol = lax.broadcasted_iota(jnp.int32, (SG_CHUNK, SG_GROUPS * SG_CHUNK), 1)
    ws = jnp.where(wrow >= (wcol % SG_CHUNK), sgw_ref[...], jnp.zeros((), BF16))
    cgrp = lax.broadcasted_iota(jnp.int32, (SG_CHUNK, w), 1) // SG_GW
    ys = []
    for c in range(ta // SG_CHUNK):
        vc = vn[c * SG_CHUNK:(c + 1) * SG_CHUNK, :]
        rhs = jnp.concatenate([jnp.where(cgrp == g, vc, 0.0) for g in range(SG_GROUPS)], axis=0).astype(BF16)
        ys.append(jnp.dot(ws, rhs, preferred_element_type=F32) + sgb_ref[...])
    y_d = u * jnp.concatenate(ys, axis=0)
    yloc_ref[:, 2 * w:3 * w] = y_d.astype(BF16)


def _local_mixer(x, w_loc, cos_t, sin_t, pool_bd, pool_scale, conv_w, sg_ln_g, sg_ln_b, sg_wcat, sg_bias, *, ta):
    b, s, d = x.shape
    w = BRANCH_W
    ncol = w_loc.shape[1]
    full = lambda shape: pl.BlockSpec(shape, lambda i, j: (0,) * len(shape))
    seq_blk = lambda width: pl.BlockSpec((None, ta, width), lambda i, j: (i, j, 0))
    return pl.pallas_call(
        _local_mixer_kernel,
        grid=(b, s // ta),
        in_specs=[
            seq_blk(d),
            full((d, ncol)),
            pl.BlockSpec((ta, w), lambda i, j: (j, 0)),
            pl.BlockSpec((ta, w), lambda i, j: (j, 0)),
            full((w, w)), full((1, w)), full((CONV_W, w)), full((1, w)), full((1, w)),
            full((SG_CHUNK, SG_GROUPS * SG_CHUNK)), full((SG_CHUNK, w)),
        ],
        out_specs=[seq_blk(3 * w), seq_blk(w), seq_blk(w), seq_blk(w)],
        out_shape=[
            jax.ShapeDtypeStruct((b, s, 3 * w), BF16),
            jax.ShapeDtypeStruct((b, s, w), BF16),
            jax.ShapeDtypeStruct((b, s, w), BF16),
            jax.ShapeDtypeStruct((b, s, w), BF16),
        ],
        scratch_shapes=[pltpu.VMEM((ta + MAX_POOL, w), F32), pltpu.VMEM((ta + 8, w), F32)],
        compiler_params=pltpu.CompilerParams(dimension_semantics=("parallel", "arbitrary"),
                                             vmem_limit_bytes=V7X_VMEM_LIMIT),
        name="local_mixer",
    )(x, w_loc, cos_t, sin_t, pool_bd, pool_scale, conv_w, sg_ln_g, sg_ln_b, sg_wcat, sg_bias)


def _diff_attn_kernel(lam_ref, q_ref, k_ref, v_ref, g_ref, o_ref, q2_ref, m_ref, acc_ref, *, lam_init):
    qi = pl.program_id(1)
    h = pl.program_id(2)
    tq = q_ref.shape[0]
    tk = tq
    w = BRANCH_W
    lane = lax.broadcasted_iota(jnp.int32, (tq, w), 1)
    klane = lax.broadcasted_iota(jnp.int32, (tk, w), 1)
    ones_lane = (h * DA_V_DIM + DA_V_DIM) % w

    q = q_ref[...]
    sub = lane // DA_HEAD_DIM
    zero = jnp.zeros((), BF16)
    q2_ref[0:tq, :] = jnp.where(sub == 2 * h, q, zero)
    q2_ref[tq:2 * tq, :] = jnp.where(sub == 2 * h + 1, q, zero)
    m_ref[...] = jnp.full(m_ref.shape, NEG_INF, F32)
    acc_ref[...] = jnp.zeros(acc_ref.shape, F32)

    def kv_step(start, masked):
        kt = k_ref[pl.ds(start, tk), :]
        vt = v_ref[pl.ds(start, tk), :]
        s = lax.dot_general(q2_ref[...], kt, (((1,), (1,)), ((), ())), preferred_element_type=F32)
        if masked:
            r = lax.broadcasted_iota(jnp.int32, (2 * tq, tk), 0) % tq
            c = lax.broadcasted_iota(jnp.int32, (2 * tq, tk), 1)
            s = jnp.where(r >= c, s, NEG_INF)
        m_old = m_ref[...]
        m_new = jnp.maximum(m_old, jnp.max(s, axis=-1, keepdims=True))
        e = jnp.exp(s - m_new)
        vh = jnp.where(klane == ones_lane, jnp.ones((), BF16), vt)
        pv = jnp.dot(e.astype(BF16), vh, preferred_element_type=F32)
        acc_ref[...] = acc_ref[...] * jnp.exp(m_old - m_new) + pv
        m_ref[...] = m_new

    def body(i, carry):
        kv_step(pl.multiple_of(i * tk, tk), False)
        return carry

    lax.fori_loop(0, qi, body, 0)
    kv_step(pl.multiple_of(qi * tk, tk), True)

    lam_rows = lam_ref[...]
    lam = (jnp.exp(jnp.sum(lam_rows[0:1, :] * lam_rows[1:2, :], axis=-1, keepdims=True))
           - jnp.exp(jnp.sum(lam_rows[2:3, :] * lam_rows[3:4, :], axis=-1, keepdims=True)) + lam_init)
    o1 = acc_ref[0:tq, :]
    o2 = acc_ref[tq:2 * tq, :]
    is_ones = lane == ones_lane
    l1 = jnp.sum(jnp.where(is_ones, o1, 0.0), axis=-1, keepdims=True)
    l2 = jnp.sum(jnp.where(is_ones, o2, 0.0), axis=-1, keepdims=True)
    in_head = (lane // DA_V_DIM) == h
    a = jnp.where(in_head, o1 / l1 - lam * (o2 / l2), 0.0)
    ms = jnp.sum(a * a, axis=-1, keepdims=True) * (1.0 / DA_V_DIM)
    y = (a * lax.rsqrt(ms + LN_EPS)) * (1.0 - lam_init) * g_ref[...]

    @pl.when(h == 0)
    def _():
        o_ref[...] = y.astype(BF16)

    @pl.when(h > 0)
    def _():
        o_ref[...] = jnp.where(in_head, y, o_ref[...].astype(F32)).astype(BF16)


def _diff_attn(lam_rows, q, k, v, subln_row, *, lam_init, tq):
    b, s, w = q.shape
    return pl.pallas_call(
        functools.partial(_diff_attn_kernel, lam_init=lam_init),
        grid=(b, s // tq, DA_HEADS),
        in_specs=[
            pl.BlockSpec((8, 128), lambda i, j, h: (0, 0)),
            pl.BlockSpec((None, tq, w), lambda i, j, h: (i, j, 0)),
            pl.BlockSpec((None, s, w), lambda i, j, h: (i, 0, 0)),
            pl.BlockSpec((None, s, w), lambda i, j, h: (i, 0, 0)),
            pl.BlockSpec((1, w), lambda i, j, h: (0, 0)),
        ],
        out_specs=pl.BlockSpec((None, tq, w), lambda i, j, h: (i, j, 0)),
        out_shape=jax.ShapeDtypeStruct((b, s, w), BF16),
        scratch_shapes=[pltpu.VMEM((2 * tq, w), BF16), pltpu.VMEM((2 * tq, 1), F32),
                        pltpu.VMEM((2 * tq, w), F32)],
        compiler_params=pltpu.CompilerParams(dimension_semantics=("parallel", "parallel", "arbitrary"),
                                             vmem_limit_bytes=V7X_VMEM_LIMIT),
        name="diff_attn",
    )(lam_rows, q, k, v, subln_row)


def _merge_kernel(x_ref, yloc_ref, yc_ref, wgate_ref, wbr_ref, wo_ref, g_ref, b_ref, o_ref, *, alpha):
    w = BRANCH_W
    d = x_ref.shape[1]
    x = x_ref[...]
    xb = x.astype(BF16)
    branches = (yloc_ref[:, 0:w], yloc_ref[:, w:2 * w], yc_ref[...], yloc_ref[:, 2 * w:3 * w])
    merged = None
    for i in range(N_BRANCH):
        gate = jax.nn.sigmoid(jnp.dot(xb, wgate_ref[:, i * d:(i + 1) * d], preferred_element_type=F32))
        term = gate * jnp.dot(branches[i], wbr_ref[i], preferred_element_type=F32)
        merged = term if merged is None else merged + term
    mix = jnp.dot(merged.astype(BF16), wo_ref[...], preferred_element_type=F32)
    o_ref[...] = _layer_norm(alpha * x + mix, g_ref[...], b_ref[...])


def _merge(x, yloc, yc, w_gate_in, w_branch, w_o, ln_g, ln_b, *, alpha, tc):
    n, d = x.shape
    w = BRANCH_W
    tok = lambda width: pl.BlockSpec((tc, width), lambda i: (i, 0))
    return pl.pallas_call(
        functools.partial(_merge_kernel, alpha=alpha),
        grid=(n // tc,),
        in_specs=[
            tok(d), tok(3 * w), tok(w),
            pl.BlockSpec((d, N_BRANCH * d), lambda i: (0, 0)),
            pl.BlockSpec((N_BRANCH, w, d), lambda i: (0, 0, 0)),
            pl.BlockSpec((d, d), lambda i: (0, 0)),
            pl.BlockSpec((1, d), lambda i: (0, 0)),
            pl.BlockSpec((1, d), lambda i: (0, 0)),
        ],
        out_specs=tok(d),
        out_shape=jax.ShapeDtypeStruct((n, d), F32),
        compiler_params=pltpu.CompilerParams(dimension_semantics=("parallel",),
                                             vmem_limit_bytes=V7X_VMEM_LIMIT),
        name="gated_merge",
    )(x, yloc, yc, w_gate_in, w_branch, w_o, ln_g, ln_b)


def _route(x, wr_hi, wr_lo, bias):
    rows = x.shape[0]
    xh = x.astype(BF16)
    xl = (x - xh.astype(F32)).astype(BF16)
    logits = (jnp.dot(xh, wr_hi, preferred_element_type=F32) + jnp.dot(xl, wr_hi, preferred_element_type=F32)
              + jnp.dot(xh, wr_lo, preferred_element_type=F32)) + bias
    lane = lax.broadcasted_iota(jnp.int32, (rows, ROUTER_LANES), 1)
    gl = jnp.where(lane < N_GROUPS, logits, NEG_INF)
    gmax = jnp.max(gl, axis=-1, keepdims=True)
    g_sel = jnp.min(jnp.where(gl == gmax, lane, ROUTER_LANES), axis=-1, keepdims=True)
    p_sel = 1.0 / jnp.sum(jnp.exp(gl - gmax), axis=-1, keepdims=True)
    lo = N_GROUPS + EXPERTS_PER_GROUP * g_sel
    el = jnp.where((lane >= lo) & (lane < lo + EXPERTS_PER_GROUP), logits, NEG_INF)
    v1 = jnp.max(el, axis=-1, keepdims=True)
    i1 = jnp.min(jnp.where(el == v1, lane, ROUTER_LANES), axis=-1, keepdims=True)
    el2 = jnp.where(lane == i1, NEG_INF, el)
    v2 = jnp.max(el2, axis=-1, keepdims=True)
    i2 = jnp.min(jnp.where(el2 == v2, lane, ROUTER_LANES), axis=-1, keepdims=True)
    e2 = jnp.exp(v2 - v1)
    w1 = p_sel / (1.0 + e2)
    w2 = w1 * e2
    return jnp.where(lane == i1, w1, jnp.where(lane == i2, w2, 0.0))


def _moe_kernel(x_ref, wrh_ref, wrl_ref, rb_ref, wg_ref, wu_ref, wd_ref, g_ref, b_ref, o_ref, gate_ref, *, alpha):
    gi = pl.program_id(1)
    tm = x_ref.shape[0]
    x = x_ref[...]

    @pl.when(gi == 0)
    def _():
        gate_ref[...] = _route(x, wrh_ref[...], wrl_ref[...], rb_ref[...])

    xb = x.astype(BF16)
    hg = jnp.dot(xb, wg_ref[...], preferred_element_type=F32)
    hu = jnp.dot(xb, wu_ref[...], preferred_element_type=F32)
    gates = gate_ref[...]
    lane = lax.broadcasted_iota(jnp.int32, (tm, ROUTER_LANES), 1)
    hcol = lax.broadcasted_iota(jnp.int32, (tm, GROUP_HIDDEN), 1) // EXPERT_HIDDEN
    gate_full = jnp.zeros((tm, GROUP_HIDDEN), F32)
    for e in range(EXPERTS_PER_GROUP):
        ge = jnp.sum(jnp.where(lane == N_GROUPS + EXPERTS_PER_GROUP * gi + e, gates, 0.0), axis=-1, keepdims=True)
        gate_full = jnp.where(hcol == e, ge, gate_full)
    act = (hg * jax.nn.sigmoid(hg)) * hu * gate_full
    y = jnp.dot(act.astype(BF16), wd_ref[...], preferred_element_type=F32)

    @pl.when(gi == 0)
    def _():
        o_ref[...] = y

    @pl.when(gi > 0)
    def _():
        o_ref[...] = o_ref[...] + y

    @pl.when(gi == N_GROUPS - 1)
    def _():
        o_ref[...] = _layer_norm(alpha * x + o_ref[...], g_ref[...], b_ref[...])


def _moe(x, wr_hi, wr_lo, r_bias, wg, wu, wd, ln_g, ln_b, *, alpha, tm):
    n, d = x.shape
    gh = GROUP_HIDDEN
    return pl.pallas_call(
        functools.partial(_moe_kernel, alpha=alpha),
        grid=(n // tm, N_GROUPS),
        in_specs=[
            pl.BlockSpec((tm, d), lambda i, g: (i, 0)),
            pl.BlockSpec((d, ROUTER_LANES), lambda i, g: (0, 0)),
            pl.BlockSpec((d, ROUTER_LANES), lambda i, g: (0, 0)),
            pl.BlockSpec((1, ROUTER_LANES), lambda i, g: (0, 0)),
            pl.BlockSpec((d, gh), lambda i, g: (0, g)),
            pl.BlockSpec((d, gh), lambda i, g: (0, g)),
            pl.BlockSpec((gh, d), lambda i, g: (g, 0)),
            pl.BlockSpec((1, d), lambda i, g: (0, 0)),
            pl.BlockSpec((1, d), lambda i, g: (0, 0)),
        ],
        out_specs=pl.BlockSpec((tm, d), lambda i, g: (i, 0)),
        out_shape=jax.ShapeDtypeStruct((n, d), F32),
        scratch_shapes=[pltpu.VMEM((tm, ROUTER_LANES), F32)],
        compiler_params=pltpu.CompilerParams(dimension_semantics=("parallel", "arbitrary"),
                                             vmem_limit_bytes=V7X_VMEM_LIMIT),
        name="moe",
    )(x, wr_hi, wr_lo, r_bias, wg, wu, wd, ln_g, ln_b)


def _block_diag(blocks):
    n, r, c = blocks.shape
    eye = jnp.eye(n, dtype=blocks.dtype)
    return jnp.einsum("grc,gh->grhc", blocks, eye).reshape(n * r, n * c)


def kernel(x, w_in, pool_w, pool_scale, conv_w, lam_q1, lam_k1, lam_q2, lam_k2, subln_g, sg_ln_g, sg_ln_b, sg_w, sg_b, w_branch, w_o, ln1_g, ln1_b, w_rg, b_rg, w_re, b_re, w_gate, w_up, w_down, ln2_g, ln2_b):
    b, s, d = x.shape
    depth = w_in.shape[0]
    n = b * s
    w = BRANCH_W
    alpha = (2 * depth) ** 0.25
    ta = min(512, s)
    tq = min(512, s)
    tc = min(512, n)
    tm = min(512, n)

    half = DA_HEAD_DIM // 2
    inv_freq = ROPE_THETA ** (-jnp.arange(half, dtype=F32) / half)
    ang = jnp.arange(s, dtype=F32)[:, None] * inv_freq[None, :]
    cos, sin = jnp.cos(ang), jnp.sin(ang)
    reps = w // DA_HEAD_DIM
    cos_t = jnp.tile(jnp.concatenate([cos, cos], axis=-1), (1, reps))
    sin_t = jnp.tile(jnp.concatenate([-sin, sin], axis=-1), (1, reps))

    for l in range(depth):
        lam_init = 0.8 - 0.6 * math.exp(-0.3 * l)
        w_in_b = w_in[l].astype(BF16)
        w_loc = w_in_b[:, :COL_GATE]
        w_gate_in = w_in_b[:, COL_GATE:]
        pool_bd = _block_diag(pool_w[l]).astype(BF16)
        sg_wcat = jnp.transpose(sg_w[l], (1, 0, 2)).reshape(SG_CHUNK, SG_GROUPS * SG_CHUNK).astype(BF16)
        sg_bias = jnp.repeat(sg_b[l].T, SG_GW, axis=1)
        lam_rows = jnp.zeros((8, 128), F32).at[0:4, 0:DA_HEAD_DIM].set(
            jnp.stack([lam_q1[l], lam_k1[l], lam_q2[l], lam_k2[l]]).astype(F32))
        subln_row = jnp.tile(subln_g[l], DA_HEADS)[None, :]

        yloc, q, k, v = _local_mixer(x, w_loc, cos_t, sin_t, pool_bd, pool_scale[l][None, :], conv_w[l],
                                     sg_ln_g[l][None, :], sg_ln_b[l][None, :], sg_wcat, sg_bias, ta=ta)
        yc = _diff_attn(lam_rows, q, k, v, subln_row, lam_init=lam_init, tq=tq)
        x1 = _merge(x.reshape(n, d), yloc.reshape(n, 3 * w), yc.reshape(n, w), w_gate_in,
                    w_branch[l].astype(BF16), w_o[l].astype(BF16), ln1_g[l][None, :], ln1_b[l][None, :],
                    alpha=alpha, tc=tc)

        w_router = jnp.concatenate([w_rg[l], jnp.transpose(w_re[l], (1, 0, 2)).reshape(d, N_EXPERTS)], axis=1)
        w_router = jnp.pad(w_router, ((0, 0), (0, ROUTER_LANES - w_router.shape[1])))
        wr_hi = w_router.astype(BF16)
        wr_lo = (w_router - wr_hi.astype(F32)).astype(BF16)
        r_bias = jnp.pad(jnp.concatenate([b_rg[l], b_re[l].reshape(-1)]), (0, ROUTER_LANES - N_GROUPS - N_EXPERTS))[None, :]
        wg = jnp.transpose(w_gate[l], (1, 0, 2)).reshape(d, N_EXPERTS * EXPERT_HIDDEN).astype(BF16)
        wu = jnp.transpose(w_up[l], (1, 0, 2)).reshape(d, N_EXPERTS * EXPERT_HIDDEN).astype(BF16)
        wd = w_down[l].reshape(N_EXPERTS * EXPERT_HIDDEN, d).astype(BF16)
        x2 = _moe(x1, wr_hi, wr_lo, r_bias, wg, wu, wd, ln2_g[l][None, :], ln2_b[l][None, :], alpha=alpha, tm=tm)
        x = x2.reshape(b, s, d)
    return x
```

```python
import functools
import math

import jax
import jax.numpy as jnp
from jax import lax
from jax.experimental import pallas as pl
from jax.experimental.pallas import tpu as pltpu

F32 = jnp.float32
BF16 = jnp.bfloat16

BRANCH_W = 256
POOL_WINDOWS = (2, 4, 8, 16)
POOL_GW = 64
MAX_POOL = 16
CONV_W = 3
DA_HEADS = 4
DA_HEAD_DIM = 32
DA_V_DIM = 64
ROPE_THETA = 10000.0
SG_CHUNK = 128
SG_GROUPS = 4
SG_GW = 64
N_GROUPS = 4
EXPERTS_PER_GROUP = 4
N_EXPERTS = 16
EXPERT_HIDDEN = 256
GROUP_HIDDEN = EXPERTS_PER_GROUP * EXPERT_HIDDEN
N_BRANCH = 4
LN_EPS = 1e-5
NEG_INF = -1e30
ROUTER_LANES = 128
V7X_VMEM_LIMIT = 56 * 1024 * 1024

COL_POOL = 0
COL_CONV = BRANCH_W
COL_ATTN = 4 * BRANCH_W
COL_SG = 7 * BRANCH_W
COL_GATE = 9 * BRANCH_W


def _layer_norm(h, g, b):
    mu = jnp.mean(h, axis=-1, keepdims=True)
    hc = h - mu
    var = jnp.mean(hc * hc, axis=-1, keepdims=True)
    return hc * lax.rsqrt(var + LN_EPS) * g + b


def _gelu_tanh(x):
    c = math.sqrt(2.0 / math.pi)
    return 0.5 * x * (1.0 + jnp.tanh(c * (x + 0.044715 * (x * x * x))))


def _local_mixer_kernel(x_ref, w_ref, cos_ref, sin_ref, poolw_ref, pscale_ref, convw_ref, lng_ref, lnb_ref,
                        sgw_ref, sgb_ref, yloc_ref, q_ref, k_ref, vt_ref, pext_ref, zext_ref):
    t = pl.program_id(1)
    ta = x_ref.shape[0]
    w = BRANCH_W
    xb = x_ref[...].astype(BF16)
    lane = lax.broadcasted_iota(jnp.int32, (ta, w), 1)
    row = lax.broadcasted_iota(jnp.int32, (ta, w), 0)

    @pl.when(t == 0)
    def _():
        pext_ref[0:MAX_POOL, :] = jnp.zeros((MAX_POOL, w), F32)
        zext_ref[0:8, :] = jnp.zeros((8, w), F32)

    @pl.when(t > 0)
    def _():
        pext_ref[0:MAX_POOL, :] = pext_ref[ta:ta + MAX_POOL, :]
        zext_ref[0:8, :] = zext_ref[ta:ta + 8, :]

    p = jnp.dot(xb, w_ref[:, COL_POOL:COL_POOL + w], preferred_element_type=F32)
    pext_ref[MAX_POOL:MAX_POOL + ta, :] = p

    def prev(kk):
        return pext_ref[pl.ds(MAX_POOL - kk, ta), :]

    s2 = p + prev(1)
    s4 = s2 + (prev(2) + prev(3))
    s8 = s4 + ((prev(4) + prev(5)) + (prev(6) + prev(7)))
    s16 = s8 + (((prev(8) + prev(9)) + (prev(10) + prev(11))) + ((prev(12) + prev(13)) + (prev(14) + prev(15))))
    grp = lane // POOL_GW
    win_sum = jnp.where(grp == 0, s2, jnp.where(grp == 1, s4, jnp.where(grp == 2, s8, s16)))
    win = jnp.where(grp == 0, 2, jnp.where(grp == 1, 4, jnp.where(grp == 2, 8, 16)))
    count = jnp.minimum(t * ta + row + 1, win).astype(F32)
    d = (win_sum / count - p).astype(BF16)
    y_a = jnp.dot(d, poolw_ref[...], preferred_element_type=F32) * pscale_ref[...]
    yloc_ref[:, 0:w] = y_a.astype(BF16)

    pc = jnp.dot(xb, w_ref[:, COL_CONV:COL_CONV + 3 * w], preferred_element_type=F32)
    gb = pc[:, 0:w]
    z = pc[:, w:2 * w] * pc[:, 2 * w:3 * w]
    zext_ref[8:8 + ta, :] = z
    cw = convw_ref[...]
    y_b = zext_ref[pl.ds(6, ta), :] * cw[0:1, :] + zext_ref[pl.ds(7, ta), :] * cw[1:2, :] + z * cw[2:3, :]
    yloc_ref[:, w:2 * w] = (gb * y_b).astype(BF16)

    pa = jnp.dot(xb, w_ref[:, COL_ATTN:COL_ATTN + 3 * w], preferred_element_type=F32)
    cos = cos_ref[...]
    sin = sin_ref[...]
    first_half = (lane % DA_HEAD_DIM) < (DA_HEAD_DIM // 2)

    def rope(u):
        swapped = jnp.where(first_half, pltpu.roll(u, w - DA_HEAD_DIM // 2, axis=1),
                            pltpu.roll(u, DA_HEAD_DIM // 2, axis=1))
        return u * cos + swapped * sin

    q_ref[...] = (rope(pa[:, 0:w]) * (DA_HEAD_DIM ** -0.5 * math.log2(math.e))).astype(BF16)
    k_ref[...] = rope(pa[:, w:2 * w]).astype(BF16)
    vt_ref[...] = pa[:, 2 * w:3 * w].T.astype(BF16)

    uv = _gelu_tanh(jnp.dot(xb, w_ref[:, COL_SG:COL_SG + 2 * w], preferred_element_type=F32))
    u = uv[:, 0:w]
    vn = _layer_norm(uv[:, w:2 * w], lng_ref[...], lnb_ref[...])
    wrow = lax.broadcasted_iota(jnp.int32, (SG_CHUNK, SG_GROUPS * SG_CHUNK), 0)
    wcol = lax.broadcasted_iota(jnp.int32, (SG_CHUNK, SG_GROUPS * SG_CHUNK), 1)
    ws = jnp.where(wrow >= (wcol % SG_CHUNK), sgw_ref[...], jnp.zeros((), BF16))
    cgrp = lax.broadcasted_iota(jnp.int32, (SG_CHUNK, w), 1) // SG_GW
    ys = []
    for c in range(ta // SG_CHUNK):
        vc = vn[c * SG_CHUNK:(c + 1) * SG_CHUNK, :]
        rhs = jnp.concatenate([jnp.where(cgrp == g, vc, 0.0) for g in range(SG_GROUPS)], axis=0).astype(BF16)
        ys.append(jnp.dot(ws, rhs, preferred_element_type=F32) + sgb_ref[...])
    y_d = u * jnp.concatenate(ys, axis=0)
    yloc_ref[:, 2 * w:3 * w] = y_d.astype(BF16)


def _local_mixer(x, w_loc, cos_t, sin_t, pool_bd, pool_scale, conv_w, sg_ln_g, sg_ln_b, sg_wcat, sg_bias, *, ta):
    b, s, d = x.shape
    w = BRANCH_W
    ncol = w_loc.shape[1]
    full = lambda shape: pl.BlockSpec(shape, lambda i, j: (0,) * len(shape))
    seq_blk = lambda width: pl.BlockSpec((None, ta, width), lambda i, j: (i, j, 0))
    return pl.pallas_call(
        _local_mixer_kernel,
        grid=(b, s // ta),
        in_specs=[
            seq_blk(d),
            full((d, ncol)),
            pl.BlockSpec((ta, w), lambda i, j: (j, 0)),
            pl.BlockSpec((ta, w), lambda i, j: (j, 0)),
            full((w, w)), full((1, w)), full((CONV_W, w)), full((1, w)), full((1, w)),
            full((SG_CHUNK, SG_GROUPS * SG_CHUNK)), full((SG_CHUNK, w)),
        ],
        out_specs=[seq_blk(3 * w), seq_blk(w), seq_blk(w), pl.BlockSpec((None, w, ta), lambda i, j: (i, 0, j))],
        out_shape=[
            jax.ShapeDtypeStruct((b, s, 3 * w), BF16),
            jax.ShapeDtypeStruct((b, s, w), BF16),
            jax.ShapeDtypeStruct((b, s, w), BF16),
            jax.ShapeDtypeStruct((b, w, s), BF16),
        ],
        scratch_shapes=[pltpu.VMEM((ta + MAX_POOL, w), F32), pltpu.VMEM((ta + 8, w), F32)],
        compiler_params=pltpu.CompilerParams(dimension_semantics=("parallel", "arbitrary"),
                                             vmem_limit_bytes=V7X_VMEM_LIMIT),
        name="local_mixer",
    )(x, w_loc, cos_t, sin_t, pool_bd, pool_scale, conv_w, sg_ln_g, sg_ln_b, sg_wcat, sg_bias)


ATTN_ONES_ROWS = 16
ATTN_LANE_CHUNK = 256


def _diff_attn_kernel(lam_ref, q_ref, k_ref, vt_ref, g_ref, o_ref, q2_ref, s0_ref, s1_ref, m_ref, acc_ref, *,
                      lam_init):
    qi = pl.program_id(1)
    tq = q_ref.shape[0]
    tk = tq
    w = BRANCH_W
    n_kv = qi + 1
    total = DA_HEADS * n_kv
    nt_dims = (((1,), (1,)), ((), ()))

    q = q_ref[...]
    sub = lax.broadcasted_iota(jnp.int32, (tq, w), 1) // DA_HEAD_DIM
    zero = jnp.zeros((), BF16)
    for h in range(DA_HEADS):
        q2_ref[h, 0:tq, :] = jnp.where(sub == 2 * h, q, zero)
        q2_ref[h, tq:2 * tq, :] = jnp.where(sub == 2 * h + 1, q, zero)

    lam_rows = lam_ref[...]
    lam = (jnp.exp(jnp.sum(lam_rows[0:1, :] * lam_rows[1:2, :], axis=-1, keepdims=True))
           - jnp.exp(jnp.sum(lam_rows[2:3, :] * lam_rows[3:4, :], axis=-1, keepdims=True)) + lam_init)

    def scores(step, dst_ref):
        step = jnp.minimum(step, total - 1)
        h = step // n_kv
        kv = step - h * n_kv
        kt = k_ref[pl.ds(pl.multiple_of(kv * tk, tk), tk), :]
        dst_ref[...] = lax.dot_general(kt, q2_ref[h], nt_dims, preferred_element_type=F32)

    def column_max(sc):
        parts = [sc[r * 8:(r + 1) * 8, :] for r in range(sc.shape[0] // 8)]
        while len(parts) > 1:
            parts = [jnp.maximum(parts[2 * r], parts[2 * r + 1]) for r in range(len(parts) // 2)]
        return jnp.max(parts[0], axis=0, keepdims=True)

    def step(t, cur_ref, nxt_ref):
        h = t // n_kv
        kv = t - h * n_kv

        @pl.when(kv == 0)
        def _():
            m_ref[...] = jnp.full(m_ref.shape, NEG_INF, F32)
            acc_ref[...] = jnp.zeros(acc_ref.shape, F32)

        @pl.when(kv == qi)
        def _():
            key = lax.broadcasted_iota(jnp.int32, (tk, 2 * tq), 0)
            qry = lax.broadcasted_iota(jnp.int32, (tk, 2 * tq), 1) % tq
            cur_ref[...] = jnp.where(key <= qry, cur_ref[...], NEG_INF)

        scores(t + 1, nxt_ref)
        vth = vt_ref[pl.ds(pl.multiple_of(h * DA_V_DIM, DA_V_DIM), DA_V_DIM), pl.ds(pl.multiple_of(kv * tk, tk), tk)]
        lhs = jnp.concatenate([vth, jnp.ones((ATTN_ONES_ROWS, tk), BF16)], axis=0)
        for c in range(2 * tq // ATTN_LANE_CHUNK):
            cols = slice(c * ATTN_LANE_CHUNK, (c + 1) * ATTN_LANE_CHUNK)
            sc = cur_ref[:, cols]
            m_old = m_ref[:, cols]
            m_new = jnp.maximum(m_old, column_max(sc))
            e = jnp.exp2(sc - m_new).astype(BF16)
            pv = jnp.dot(lhs, e, preferred_element_type=F32)
            acc_ref[:, cols] = acc_ref[:, cols] * jnp.exp2(m_old - m_new) + pv
            m_ref[:, cols] = m_new

        @pl.when(kv == qi)
        def _():
            o1 = acc_ref[0:DA_V_DIM, 0:tq]
            o2 = acc_ref[0:DA_V_DIM, tq:2 * tq]
            r1 = 1.0 / acc_ref[DA_V_DIM:DA_V_DIM + 1, 0:tq]
            r2 = 1.0 / acc_ref[DA_V_DIM:DA_V_DIM + 1, tq:2 * tq]
            a = o1 * r1 - lam * (o2 * r2)
            ms = jnp.mean(a * a, axis=0, keepdims=True)
            y = (a * lax.rsqrt(ms + LN_EPS)) * (1.0 - lam_init) * g_ref[...]
            o_ref[pl.ds(pl.multiple_of(h * DA_V_DIM, DA_V_DIM), DA_V_DIM), :] = y.astype(BF16)

    scores(0, s0_ref)

    def body(i, carry):
        step(2 * i, s0_ref, s1_ref)
        step(2 * i + 1, s1_ref, s0_ref)
        return carry

    lax.fori_loop(0, total // 2, body, 0)


def _diff_attn(lam_rows, q, k, vt, subln_cols, *, lam_init, tq):
    b, s, w = q.shape
    return pl.pallas_call(
        functools.partial(_diff_attn_kernel, lam_init=lam_init),
        grid=(b, s // tq),
        in_specs=[
            pl.BlockSpec((8, 128), lambda i, j: (0, 0)),
            pl.BlockSpec((None, tq, w), lambda i, j: (i, j, 0)),
            pl.BlockSpec((None, s, w), lambda i, j: (i, 0, 0)),
            pl.BlockSpec((None, w, s), lambda i, j: (i, 0, 0)),
            pl.BlockSpec((DA_V_DIM, tq), lambda i, j: (0, 0)),
        ],
        out_specs=pl.BlockSpec((None, w, tq), lambda i, j: (i, 0, j)),
        out_shape=jax.ShapeDtypeStruct((b, w, s), BF16),
        scratch_shapes=[pltpu.VMEM((DA_HEADS, 2 * tq, w), BF16), pltpu.VMEM((tq, 2 * tq), F32), pltpu.VMEM((tq, 2 * tq), F32),
                        pltpu.VMEM((1, 2 * tq), F32), pltpu.VMEM((DA_V_DIM + ATTN_ONES_ROWS, 2 * tq), F32)],
        compiler_params=pltpu.CompilerParams(dimension_semantics=("parallel", "parallel"),
                                             vmem_limit_bytes=V7X_VMEM_LIMIT),
        name="diff_attn",
    )(lam_rows, q, k, vt, subln_cols)


def _merge_kernel(x_ref, yloc_ref, yct_ref, wgate_ref, wbr_ref, wo_ref, g_ref, b_ref, o_ref, *, alpha):
    w = BRANCH_W
    d = x_ref.shape[1]
    x = x_ref[...]
    xb = x.astype(BF16)
    y_c = yct_ref[...].astype(F32).T.astype(BF16)
    branches = (yloc_ref[:, 0:w], yloc_ref[:, w:2 * w], y_c, yloc_ref[:, 2 * w:3 * w])
    merged = None
    for i in range(N_BRANCH):
        gate = jax.nn.sigmoid(jnp.dot(xb, wgate_ref[:, i * d:(i + 1) * d], preferred_element_type=F32))
        term = gate * jnp.dot(branches[i], wbr_ref[i], preferred_element_type=F32)
        merged = term if merged is None else merged + term
    mix = jnp.dot(merged.astype(BF16), wo_ref[...], preferred_element_type=F32)
    o_ref[...] = _layer_norm(alpha * x + mix, g_ref[...], b_ref[...])


def _merge(x, yloc, yct, w_gate_in, w_branch, w_o, ln_g, ln_b, *, alpha, tc):
    n, d = x.shape
    w = BRANCH_W
    s = yct.shape[2]
    tiles_per_row = s // tc
    tok = lambda width: pl.BlockSpec((tc, width), lambda i: (i, 0))
    return pl.pallas_call(
        functools.partial(_merge_kernel, alpha=alpha),
        grid=(n // tc,),
        in_specs=[
            tok(d), tok(3 * w),
            pl.BlockSpec((None, w, tc), lambda i: (i // tiles_per_row, 0, i % tiles_per_row)),
            pl.BlockSpec((d, N_BRANCH * d), lambda i: (0, 0)),
            pl.BlockSpec((N_BRANCH, w, d), lambda i: (0, 0, 0)),
            pl.BlockSpec((d, d), lambda i: (0, 0)),
            pl.BlockSpec((1, d), lambda i: (0, 0)),
            pl.BlockSpec((1, d), lambda i: (0, 0)),
        ],
        out_specs=tok(d),
        out_shape=jax.ShapeDtypeStruct((n, d), F32),
        compiler_params=pltpu.CompilerParams(dimension_semantics=("parallel",),
                                             vmem_limit_bytes=V7X_VMEM_LIMIT),
        name="gated_merge",
    )(x, yloc, yct, w_gate_in, w_branch, w_o, ln_g, ln_b)


def _route(x, wr_hi, wr_lo, bias):
    rows = x.shape[0]
    xh = x.astype(BF16)
    xl = (x - xh.astype(F32)).astype(BF16)
    logits = (jnp.dot(xh, wr_hi, preferred_element_type=F32) + jnp.dot(xl, wr_hi, preferred_element_type=F32)
              + jnp.dot(xh, wr_lo, preferred_element_type=F32)) + bias
    lane = lax.broadcasted_iota(jnp.int32, (rows, ROUTER_LANES), 1)
    gl = jnp.where(lane < N_GROUPS, logits, NEG_INF)
    gmax = jnp.max(gl, axis=-1, keepdims=True)
    g_sel = jnp.min(jnp.where(gl == gmax, lane, ROUTER_LANES), axis=-1, keepdims=True)
    p_sel = 1.0 / jnp.sum(jnp.exp(gl - gmax), axis=-1, keepdims=True)
    lo = N_GROUPS + EXPERTS_PER_GROUP * g_sel
    el = jnp.where((lane >= lo) & (lane < lo + EXPERTS_PER_GROUP), logits, NEG_INF)
    v1 = jnp.max(el, axis=-1, keepdims=True)
    i1 = jnp.min(jnp.where(el == v1, lane, ROUTER_LANES), axis=-1, keepdims=True)
    el2 = jnp.where(lane == i1, NEG_INF, el)
    v2 = jnp.max(el2, axis=-1, keepdims=True)
    i2 = jnp.min(jnp.where(el2 == v2, lane, ROUTER_LANES), axis=-1, keepdims=True)
    e2 = jnp.exp(v2 - v1)
    w1 = p_sel / (1.0 + e2)
    w2 = w1 * e2
    return jnp.where(lane == i1, w1, jnp.where(lane == i2, w2, 0.0))


def _moe_kernel(x_ref, wrh_ref, wrl_ref, rb_ref, wg_ref, wu_ref, wd_ref, g_ref, b_ref, o_ref, gate_ref, *, alpha):
    gi = pl.program_id(1)
    tm = x_ref.shape[0]
    x = x_ref[...]

    @pl.when(gi == 0)
    def _():
        gate_ref[...] = _route(x, wrh_ref[...], wrl_ref[...], rb_ref[...])

    xb = x.astype(BF16)
    hg = jnp.dot(xb, wg_ref[...], preferred_element_type=F32)
    hu = jnp.dot(xb, wu_ref[...], preferred_element_type=F32)
    gates = gate_ref[...]
    lane = lax.broadcasted_iota(jnp.int32, (tm, ROUTER_LANES), 1)
    hcol = lax.broadcasted_iota(jnp.int32, (tm, GROUP_HIDDEN), 1) // EXPERT_HIDDEN
    gate_full = jnp.zeros((tm, GROUP_HIDDEN), F32)
    for e in range(EXPERTS_PER_GROUP):
        ge = jnp.sum(jnp.where(lane == N_GROUPS + EXPERTS_PER_GROUP * gi + e, gates, 0.0), axis=-1, keepdims=True)
        gate_full = jnp.where(hcol == e, ge, gate_full)
    act = (hg * jax.nn.sigmoid(hg)) * hu * gate_full
    y = jnp.dot(act.astype(BF16), wd_ref[...], preferred_element_type=F32)

    @pl.when(gi == 0)
    def _():
        o_ref[...] = y

    @pl.when(gi > 0)
    def _():
        o_ref[...] = o_ref[...] + y

    @pl.when(gi == N_GROUPS - 1)
    def _():
        o_ref[...] = _layer_norm(alpha * x + o_ref[...], g_ref[...], b_ref[...])


def _moe(x, wr_hi, wr_lo, r_bias, wg, wu, wd, ln_g, ln_b, *, alpha, tm):
    n, d = x.shape
    gh = GROUP_HIDDEN
    return pl.pallas_call(
        functools.partial(_moe_kernel, alpha=alpha),
        grid=(n // tm, N_GROUPS),
        in_specs=[
            pl.BlockSpec((tm, d), lambda i, g: (i, 0)),
            pl.BlockSpec((d, ROUTER_LANES), lambda i, g: (0, 0)),
            pl.BlockSpec((d, ROUTER_LANES), lambda i, g: (0, 0)),
            pl.BlockSpec((1, ROUTER_LANES), lambda i, g: (0, 0)),
            pl.BlockSpec((d, gh), lambda i, g: (0, g)),
            pl.BlockSpec((d, gh), lambda i, g: (0, g)),
            pl.BlockSpec((gh, d), lambda i, g: (g, 0)),
            pl.BlockSpec((1, d), lambda i, g: (0, 0)),
            pl.BlockSpec((1, d), lambda i, g: (0, 0)),
        ],
        out_specs=pl.BlockSpec((tm, d), lambda i, g: (i, 0)),
        out_shape=jax.ShapeDtypeStruct((n, d), F32),
        scratch_shapes=[pltpu.VMEM((tm, ROUTER_LANES), F32)],
        compiler_params=pltpu.CompilerParams(dimension_semantics=("parallel", "arbitrary"),
                                             vmem_limit_bytes=V7X_VMEM_LIMIT),
        name="moe",
    )(x, wr_hi, wr_lo, r_bias, wg, wu, wd, ln_g, ln_b)


def _block_diag(blocks):
    n, r, c = blocks.shape
    eye = jnp.eye(n, dtype=blocks.dtype)
    return jnp.einsum("grc,gh->grhc", blocks, eye).reshape(n * r, n * c)


def kernel(x, w_in, pool_w, pool_scale, conv_w, lam_q1, lam_k1, lam_q2, lam_k2, subln_g, sg_ln_g, sg_ln_b, sg_w, sg_b, w_branch, w_o, ln1_g, ln1_b, w_rg, b_rg, w_re, b_re, w_gate, w_up, w_down, ln2_g, ln2_b):
    b, s, d = x.shape
    depth = w_in.shape[0]
    n = b * s
    w = BRANCH_W
    alpha = (2 * depth) ** 0.25
    ta = min(512, s)
    tq = min(512, s)
    tc = min(512, s)
    tm = min(512, n)

    half = DA_HEAD_DIM // 2
    inv_freq = ROPE_THETA ** (-jnp.arange(half, dtype=F32) / half)
    ang = jnp.arange(s, dtype=F32)[:, None] * inv_freq[None, :]
    cos, sin = jnp.cos(ang), jnp.sin(ang)
    reps = w // DA_HEAD_DIM
    cos_t = jnp.tile(jnp.concatenate([cos, cos], axis=-1), (1, reps))
    sin_t = jnp.tile(jnp.concatenate([-sin, sin], axis=-1), (1, reps))

    for l in range(depth):
        lam_init = 0.8 - 0.6 * math.exp(-0.3 * l)
        w_in_b = w_in[l].astype(BF16)
        w_loc = w_in_b[:, :COL_GATE]
        w_gate_in = w_in_b[:, COL_GATE:]
        pool_bd = _block_diag(pool_w[l]).astype(BF16)
        sg_wcat = jnp.transpose(sg_w[l], (1, 0, 2)).reshape(SG_CHUNK, SG_GROUPS * SG_CHUNK).astype(BF16)
        sg_bias = jnp.repeat(sg_b[l].T, SG_GW, axis=1)
        lam_rows = jnp.zeros((8, 128), F32).at[0:4, 0:DA_HEAD_DIM].set(
            jnp.stack([lam_q1[l], lam_k1[l], lam_q2[l], lam_k2[l]]).astype(F32))
        subln_cols = jnp.broadcast_to(subln_g[l][:, None], (DA_V_DIM, tq))

        yloc, q, k, vt = _local_mixer(x, w_loc, cos_t, sin_t, pool_bd, pool_scale[l][None, :], conv_w[l],
                                      sg_ln_g[l][None, :], sg_ln_b[l][None, :], sg_wcat, sg_bias, ta=ta)
        yct = _diff_attn(lam_rows, q, k, vt, subln_cols, lam_init=lam_init, tq=tq)
        x1 = _merge(x.reshape(n, d), yloc.reshape(n, 3 * w), yct, w_gate_in,
                    w_branch[l].astype(BF16), w_o[l].astype(BF16), ln1_g[l][None, :], ln1_b[l][None, :],
                    alpha=alpha, tc=tc)

        w_router = jnp.concatenate([w_rg[l], jnp.transpose(w_re[l], (1, 0, 2)).reshape(d, N_EXPERTS)], axis=1)
        w_router = jnp.pad(w_router, ((0, 0), (0, ROUTER_LANES - w_router.shape[1])))
        wr_hi = w_router.astype(BF16)
        wr_lo = (w_router - wr_hi.astype(F32)).astype(BF16)
        r_bias = jnp.pad(jnp.concatenate([b_rg[l], b_re[l].reshape(-1)]), (0, ROUTER_LANES - N_GROUPS - N_EXPERTS))[None, :]
        wg = jnp.transpose(w_gate[l], (1, 0, 2)).reshape(d, N_EXPERTS * EXPERT_HIDDEN).astype(BF16)
        wu = jnp.transpose(w_up[l], (1, 0, 2)).reshape(d, N_EXPERTS * EXPERT_HIDDEN).astype(BF16)
        wd = w_down[l].reshape(N_EXPERTS * EXPERT_HIDDEN, d).astype(BF16)
        x2 = _moe(x1, wr_hi, wr_lo, r_bias, wg, wu, wd, ln2_g[l][None, :], ln2_b[l][None, :], alpha=alpha, tm=tm)
        x = x2.reshape(b, s, d)
    return x
```

```python
import functools
import math

import jax
import jax.numpy as jnp
from jax import lax
from jax.experimental import pallas as pl
from jax.experimental.pallas import tpu as pltpu

F32 = jnp.float32
BF16 = jnp.bfloat16

BRANCH_W = 256
POOL_WINDOWS = (2, 4, 8, 16)
POOL_GW = 64
MAX_POOL = 16
CONV_W = 3
DA_HEADS = 4
DA_HEAD_DIM = 32
DA_V_DIM = 64
ROPE_THETA = 10000.0
SG_CHUNK = 128
SG_GROUPS = 4
SG_GW = 64
N_GROUPS = 4
EXPERTS_PER_GROUP = 4
N_EXPERTS = 16
EXPERT_HIDDEN = 256
GROUP_HIDDEN = EXPERTS_PER_GROUP * EXPERT_HIDDEN
N_BRANCH = 4
LN_EPS = 1e-5
NEG_INF = -1e30
ROUTER_LANES = 128
V7X_VMEM_LIMIT = 56 * 1024 * 1024

COL_POOL = 0
COL_CONV = BRANCH_W
COL_ATTN = 4 * BRANCH_W
COL_SG = 7 * BRANCH_W
COL_GATE = 9 * BRANCH_W


def _layer_norm(h, g, b):
    mu = jnp.mean(h, axis=-1, keepdims=True)
    hc = h - mu
    var = jnp.mean(hc * hc, axis=-1, keepdims=True)
    return hc * lax.rsqrt(var + LN_EPS) * g + b


def _gelu_tanh(x):
    c = math.sqrt(2.0 / math.pi)
    return 0.5 * x * (1.0 + jnp.tanh(c * (x + 0.044715 * (x * x * x))))


def _local_mixer_kernel(x_ref, w_ref, cos_ref, sin_ref, poolw_ref, pscale_ref, convw_ref, lng_ref, lnb_ref,
                        sgw_ref, sgb_ref, yloc_ref, q_ref, k_ref, vt_ref, pext_ref, zext_ref):
    t = pl.program_id(1)
    ta = x_ref.shape[0]
    w = BRANCH_W
    xb = x_ref[...].astype(BF16)
    lane = lax.broadcasted_iota(jnp.int32, (ta, w), 1)
    row = lax.broadcasted_iota(jnp.int32, (ta, w), 0)

    @pl.when(t == 0)
    def _():
        pext_ref[0:MAX_POOL, :] = jnp.zeros((MAX_POOL, w), F32)
        zext_ref[0:8, :] = jnp.zeros((8, w), F32)

    @pl.when(t > 0)
    def _():
        pext_ref[0:MAX_POOL, :] = pext_ref[ta:ta + MAX_POOL, :]
        zext_ref[0:8, :] = zext_ref[ta:ta + 8, :]

    p = jnp.dot(xb, w_ref[:, COL_POOL:COL_POOL + w], preferred_element_type=F32)
    pext_ref[MAX_POOL:MAX_POOL + ta, :] = p

    def prev(kk):
        return pext_ref[pl.ds(MAX_POOL - kk, ta), :]

    s2 = p + prev(1)
    s4 = s2 + (prev(2) + prev(3))
    s8 = s4 + ((prev(4) + prev(5)) + (prev(6) + prev(7)))
    s16 = s8 + (((prev(8) + prev(9)) + (prev(10) + prev(11))) + ((prev(12) + prev(13)) + (prev(14) + prev(15))))
    grp = lane // POOL_GW
    win_sum = jnp.where(grp == 0, s2, jnp.where(grp == 1, s4, jnp.where(grp == 2, s8, s16)))
    win = jnp.where(grp == 0, 2, jnp.where(grp == 1, 4, jnp.where(grp == 2, 8, 16)))
    count = jnp.minimum(t * ta + row + 1, win).astype(F32)
    d = (win_sum / count - p).astype(BF16)
    y_a = jnp.dot(d, poolw_ref[...], preferred_element_type=F32) * pscale_ref[...]
    yloc_ref[:, 0:w] = y_a.astype(BF16)

    pc = jnp.dot(xb, w_ref[:, COL_CONV:COL_CONV + 3 * w], preferred_element_type=F32)
    gb = pc[:, 0:w]
    z = pc[:, w:2 * w] * pc[:, 2 * w:3 * w]
    zext_ref[8:8 + ta, :] = z
    cw = convw_ref[...]
    y_b = zext_ref[pl.ds(6, ta), :] * cw[0:1, :] + zext_ref[pl.ds(7, ta), :] * cw[1:2, :] + z * cw[2:3, :]
    yloc_ref[:, w:2 * w] = (gb * y_b).astype(BF16)

    pa = jnp.dot(xb, w_ref[:, COL_ATTN:COL_ATTN + 3 * w], preferred_element_type=F32)
    cos = cos_ref[...]
    sin = sin_ref[...]
    first_half = (lane % DA_HEAD_DIM) < (DA_HEAD_DIM // 2)

    def rope(u):
        swapped = jnp.where(first_half, pltpu.roll(u, w - DA_HEAD_DIM // 2, axis=1),
                            pltpu.roll(u, DA_HEAD_DIM // 2, axis=1))
        return u * cos + swapped * sin

    q_ref[...] = (rope(pa[:, 0:w]) * (DA_HEAD_DIM ** -0.5 * math.log2(math.e))).astype(BF16)
    k_ref[...] = rope(pa[:, w:2 * w]).astype(BF16)
    vt_ref[...] = pa[:, 2 * w:3 * w].T.astype(BF16)

    uv = _gelu_tanh(jnp.dot(xb, w_ref[:, COL_SG:COL_SG + 2 * w], preferred_element_type=F32))
    u = uv[:, 0:w]
    vn = _layer_norm(uv[:, w:2 * w], lng_ref[...], lnb_ref[...])
    wrow = lax.broadcasted_iota(jnp.int32, (SG_CHUNK, SG_GROUPS * SG_CHUNK), 0)
    wcol = lax.broadcasted_iota(jnp.int32, (SG_CHUNK, SG_GROUPS * SG_CHUNK), 1)
    ws = jnp.where(wrow >= (wcol % SG_CHUNK), sgw_ref[...], jnp.zeros((), BF16))
    cgrp = lax.broadcasted_iota(jnp.int32, (SG_CHUNK, w), 1) // SG_GW
    ys = []
    for c in range(ta // SG_CHUNK):
        vc = vn[c * SG_CHUNK:(c + 1) * SG_CHUNK, :]
        rhs = jnp.concatenate([jnp.where(cgrp == g, vc, 0.0) for g in range(SG_GROUPS)], axis=0).astype(BF16)
        ys.append(jnp.dot(ws, rhs, preferred_element_type=F32) + sgb_ref[...])
    y_d = u * jnp.concatenate(ys, axis=0)
    yloc_ref[:, 2 * w:3 * w] = y_d.astype(BF16)


def _local_mixer(x, w_loc, cos_t, sin_t, pool_bd, pool_scale, conv_w, sg_ln_g, sg_ln_b, sg_wcat, sg_bias, *, ta):
    b, s, d = x.shape
    w = BRANCH_W
    ncol = w_loc.shape[1]
    full = lambda shape: pl.BlockSpec(shape, lambda i, j: (0,) * len(shape))
    seq_blk = lambda width: pl.BlockSpec((None, ta, width), lambda i, j: (i, j, 0))
    return pl.pallas_call(
        _local_mixer_kernel,
        grid=(b, s // ta),
        in_specs=[
            seq_blk(d),
            full((d, ncol)),
            pl.BlockSpec((ta, w), lambda i, j: (j, 0)),
            pl.BlockSpec((ta, w), lambda i, j: (j, 0)),
            full((w, w)), full((1, w)), full((CONV_W, w)), full((1, w)), full((1, w)),
            full((SG_CHUNK, SG_GROUPS * SG_CHUNK)), full((SG_CHUNK, w)),
        ],
        out_specs=[seq_blk(3 * w), seq_blk(w), seq_blk(w), pl.BlockSpec((None, w, ta), lambda i, j: (i, 0, j))],
        out_shape=[
            jax.ShapeDtypeStruct((b, s, 3 * w), BF16),
            jax.ShapeDtypeStruct((b, s, w), BF16),
            jax.ShapeDtypeStruct((b, s, w), BF16),
            jax.ShapeDtypeStruct((b, w, s), BF16),
        ],
        scratch_shapes=[pltpu.VMEM((ta + MAX_POOL, w), F32), pltpu.VMEM((ta + 8, w), F32)],
        compiler_params=pltpu.CompilerParams(dimension_semantics=("parallel", "arbitrary"),
                                             vmem_limit_bytes=V7X_VMEM_LIMIT),
        name="local_mixer",
    )(x, w_loc, cos_t, sin_t, pool_bd, pool_scale, conv_w, sg_ln_g, sg_ln_b, sg_wcat, sg_bias)


ATTN_ONES_ROWS = 16
ATTN_LANE_CHUNK = 256


def _diff_attn_kernel(lam_ref, q_ref, k_ref, vt_ref, g_ref, o_ref, q2_ref, s0_ref, s1_ref, m_ref, acc_ref, *,
                      lam_init):
    qi = pl.program_id(1)
    tq = q_ref.shape[0]
    tk = tq
    w = BRANCH_W
    n_kv = qi + 1
    total = DA_HEADS * n_kv
    nt_dims = (((1,), (1,)), ((), ()))

    q = q_ref[...]
    sub = lax.broadcasted_iota(jnp.int32, (tq, w), 1) // DA_HEAD_DIM
    zero = jnp.zeros((), BF16)
    for h in range(DA_HEADS):
        q2_ref[h, 0:tq, :] = jnp.where(sub == 2 * h, q, zero)
        q2_ref[h, tq:2 * tq, :] = jnp.where(sub == 2 * h + 1, q, zero)

    lam_rows = lam_ref[...]
    lam = (jnp.exp(jnp.sum(lam_rows[0:1, :] * lam_rows[1:2, :], axis=-1, keepdims=True))
           - jnp.exp(jnp.sum(lam_rows[2:3, :] * lam_rows[3:4, :], axis=-1, keepdims=True)) + lam_init)

    def scores(step, dst_ref):
        step = jnp.minimum(step, total - 1)
        h = step // n_kv
        kv = step - h * n_kv
        kt = k_ref[pl.ds(pl.multiple_of(kv * tk, tk), tk), :]
        dst_ref[...] = lax.dot_general(kt, q2_ref[h], nt_dims, preferred_element_type=F32)

    def column_max(sc):
        parts = [sc[r * 8:(r + 1) * 8, :] for r in range(sc.shape[0] // 8)]
        while len(parts) > 1:
            parts = [jnp.maximum(parts[2 * r], parts[2 * r + 1]) for r in range(len(parts) // 2)]
        return jnp.max(parts[0], axis=0, keepdims=True)

    def step(t, cur_ref, nxt_ref):
        h = t // n_kv
        kv = t - h * n_kv

        @pl.when(kv == 0)
        def _():
            m_ref[...] = jnp.full(m_ref.shape, NEG_INF, F32)
            acc_ref[...] = jnp.zeros(acc_ref.shape, F32)

        @pl.when(kv == qi)
        def _():
            key = lax.broadcasted_iota(jnp.int32, (tk, 2 * tq), 0)
            qry = lax.broadcasted_iota(jnp.int32, (tk, 2 * tq), 1) % tq
            cur_ref[...] = jnp.where(key <= qry, cur_ref[...], NEG_INF)

        scores(t + 1, nxt_ref)
        vth = vt_ref[pl.ds(pl.multiple_of(h * DA_V_DIM, DA_V_DIM), DA_V_DIM), pl.ds(pl.multiple_of(kv * tk, tk), tk)]
        lhs = jnp.concatenate([vth, jnp.ones((ATTN_ONES_ROWS, tk), BF16)], axis=0)
        for c in range(2 * tq // ATTN_LANE_CHUNK):
            cols = slice(c * ATTN_LANE_CHUNK, (c + 1) * ATTN_LANE_CHUNK)
            sc = cur_ref[:, cols]
            m_old = m_ref[:, cols]
            m_new = jnp.maximum(m_old, column_max(sc))
            e = jnp.exp2(sc - m_new).astype(BF16)
            pv = jnp.dot(lhs, e, preferred_element_type=F32)
            acc_ref[:, cols] = acc_ref[:, cols] * jnp.exp2(m_old - m_new) + pv
            m_ref[:, cols] = m_new

        @pl.when(kv == qi)
        def _():
            o1 = acc_ref[0:DA_V_DIM, 0:tq]
            o2 = acc_ref[0:DA_V_DIM, tq:2 * tq]
            r1 = 1.0 / acc_ref[DA_V_DIM:DA_V_DIM + 1, 0:tq]
            r2 = 1.0 / acc_ref[DA_V_DIM:DA_V_DIM + 1, tq:2 * tq]
            a = o1 * r1 - lam * (o2 * r2)
            ms = jnp.mean(a * a, axis=0, keepdims=True)
            y = (a * lax.rsqrt(ms + LN_EPS)) * (1.0 - lam_init) * g_ref[...]
            o_ref[pl.ds(pl.multiple_of(h * DA_V_DIM, DA_V_DIM), DA_V_DIM), :] = y.astype(BF16)

    scores(0, s0_ref)

    def body(i, carry):
        step(2 * i, s0_ref, s1_ref)
        step(2 * i + 1, s1_ref, s0_ref)
        return carry

    lax.fori_loop(0, total // 2, body, 0)


def _diff_attn(lam_rows, q, k, vt, subln_cols, *, lam_init, tq):
    b, s, w = q.shape
    return pl.pallas_call(
        functools.partial(_diff_attn_kernel, lam_init=lam_init),
        grid=(b, s // tq),
        in_specs=[
            pl.BlockSpec((8, 128), lambda i, j: (0, 0)),
            pl.BlockSpec((None, tq, w), lambda i, j: (i, j, 0)),
            pl.BlockSpec((None, s, w), lambda i, j: (i, 0, 0)),
            pl.BlockSpec((None, w, s), lambda i, j: (i, 0, 0)),
            pl.BlockSpec((DA_V_DIM, tq), lambda i, j: (0, 0)),
        ],
        out_specs=pl.BlockSpec((None, w, tq), lambda i, j: (i, 0, j)),
        out_shape=jax.ShapeDtypeStruct((b, w, s), BF16),
        scratch_shapes=[pltpu.VMEM((DA_HEADS, 2 * tq, w), BF16), pltpu.VMEM((tq, 2 * tq), F32), pltpu.VMEM((tq, 2 * tq), F32),
                        pltpu.VMEM((1, 2 * tq), F32), pltpu.VMEM((DA_V_DIM + ATTN_ONES_ROWS, 2 * tq), F32)],
        compiler_params=pltpu.CompilerParams(dimension_semantics=("parallel", "parallel"),
                                             vmem_limit_bytes=V7X_VMEM_LIMIT),
        name="diff_attn",
    )(lam_rows, q, k, vt, subln_cols)


def _route(x, wr_hi, wr_lo, bias):
    rows = x.shape[0]
    xh = x.astype(BF16)
    xl = (x - xh.astype(F32)).astype(BF16)
    logits = (jnp.dot(xh, wr_hi, preferred_element_type=F32) + jnp.dot(xl, wr_hi, preferred_element_type=F32)
              + jnp.dot(xh, wr_lo, preferred_element_type=F32)) + bias
    lane = lax.broadcasted_iota(jnp.int32, (rows, ROUTER_LANES), 1)
    gl = jnp.where(lane < N_GROUPS, logits, NEG_INF)
    gmax = jnp.max(gl, axis=-1, keepdims=True)
    g_sel = jnp.min(jnp.where(gl == gmax, lane, ROUTER_LANES), axis=-1, keepdims=True)
    p_sel = 1.0 / jnp.sum(jnp.exp(gl - gmax), axis=-1, keepdims=True)
    lo = N_GROUPS + EXPERTS_PER_GROUP * g_sel
    el = jnp.where((lane >= lo) & (lane < lo + EXPERTS_PER_GROUP), logits, NEG_INF)
    v1 = jnp.max(el, axis=-1, keepdims=True)
    i1 = jnp.min(jnp.where(el == v1, lane, ROUTER_LANES), axis=-1, keepdims=True)
    el2 = jnp.where(lane == i1, NEG_INF, el)
    v2 = jnp.max(el2, axis=-1, keepdims=True)
    i2 = jnp.min(jnp.where(el2 == v2, lane, ROUTER_LANES), axis=-1, keepdims=True)
    e2 = jnp.exp(v2 - v1)
    w1 = p_sel / (1.0 + e2)
    w2 = w1 * e2
    gates = jnp.where(lane == i1, w1, jnp.where(lane == i2, w2, 0.0))
    compact = jnp.zeros_like(gates)
    for grp in range(N_GROUPS):
        shifted = pltpu.roll(gates, ROUTER_LANES - (N_GROUPS + EXPERTS_PER_GROUP * grp), axis=1)
        compact = jnp.where(g_sel == grp, shifted, compact)
    return jnp.where(lane < EXPERTS_PER_GROUP, compact, 0.0), g_sel


def _merge_kernel(x_ref, yloc_ref, yct_ref, wgate_ref, wbr_ref, wo_ref, g_ref, b_ref, wrh_ref, wrl_ref, rb_ref,
                  o_ref, xa_ref, gid_ref, *, alpha):
    w = BRANCH_W
    tc, d = x_ref.shape
    x = x_ref[...]
    xb = x.astype(BF16)
    y_c = yct_ref[...].astype(F32).T.astype(BF16)
    branches = (yloc_ref[:, 0:w], yloc_ref[:, w:2 * w], y_c, yloc_ref[:, 2 * w:3 * w])
    merged = None
    for i in range(N_BRANCH):
        gate = jax.nn.sigmoid(jnp.dot(xb, wgate_ref[:, i * d:(i + 1) * d], preferred_element_type=F32))
        term = gate * jnp.dot(branches[i], wbr_ref[i], preferred_element_type=F32)
        merged = term if merged is None else merged + term
    mix = jnp.dot(merged.astype(BF16), wo_ref[...], preferred_element_type=F32)
    x1 = _layer_norm(alpha * x + mix, g_ref[...], b_ref[...])
    o_ref[...] = x1

    gates, g_sel = _route(x1, wrh_ref[...], wrl_ref[...], rb_ref[...])
    g_hi = gates.astype(BF16).astype(F32)
    aux = g_hi + pltpu.roll(gates - g_hi, EXPERTS_PER_GROUP, axis=1)
    xa_ref[:, 0:d] = x1.astype(BF16)
    xa_ref[:, d:d + AUX_LANES] = aux.astype(BF16)
    gb = jnp.broadcast_to(g_sel, (tc, ROUTER_LANES))
    on_diag = (lax.broadcasted_iota(jnp.int32, (ROUTER_LANES, ROUTER_LANES), 0)
               == lax.broadcasted_iota(jnp.int32, (ROUTER_LANES, ROUTER_LANES), 1))
    for blk in range(tc // ROUTER_LANES):
        sub = gb[blk * ROUTER_LANES:(blk + 1) * ROUTER_LANES, :]
        gid_ref[:, blk * ROUTER_LANES:(blk + 1) * ROUTER_LANES] = jnp.sum(jnp.where(on_diag, sub, 0), axis=0,
                                                                          keepdims=True)


def _merge(x, yloc, yct, w_gate_in, w_branch, w_o, ln_g, ln_b, wr_hi, wr_lo, r_bias, *, alpha, tc):
    n, d = x.shape
    w = BRANCH_W
    s = yct.shape[2]
    tiles_per_row = s // tc
    tok = lambda width: pl.BlockSpec((tc, width), lambda i: (i, 0))
    const = lambda shape: pl.BlockSpec(shape, lambda i: (0,) * len(shape))
    return pl.pallas_call(
        functools.partial(_merge_kernel, alpha=alpha),
        grid=(n // tc,),
        in_specs=[
            tok(d), tok(3 * w),
            pl.BlockSpec((None, w, tc), lambda i: (i // tiles_per_row, 0, i % tiles_per_row)),
            const((d, N_BRANCH * d)), const((N_BRANCH, w, d)), const((d, d)), const((1, d)), const((1, d)),
            const((d, ROUTER_LANES)), const((d, ROUTER_LANES)), const((1, ROUTER_LANES)),
        ],
        out_specs=[tok(d), tok(d + AUX_LANES), pl.BlockSpec((None, 1, tc), lambda i: (i, 0, 0))],
        out_shape=[jax.ShapeDtypeStruct((n, d), F32), jax.ShapeDtypeStruct((n, d + AUX_LANES), BF16),
                   jax.ShapeDtypeStruct((n // tc, 1, tc), jnp.int32)],
        compiler_params=pltpu.CompilerParams(dimension_semantics=("parallel",),
                                             vmem_limit_bytes=V7X_VMEM_LIMIT),
        name="gated_merge",
    )(x, yloc, yct, w_gate_in, w_branch, w_o, ln_g, ln_b, wr_hi, wr_lo, r_bias)


DISPATCH_TILE = 256
DISPATCH_CHUNK = 16
DISPATCH_ROWS = 384
AUX_LANES = 128
MOE_TILE = 512
TAB_DEST, TAB_LOFF, TAB_NCH = 0, 1, 2


def _dispatch_tables(gid, n, tm):
    t, ch = DISPATCH_TILE, DISPATCH_CHUNK
    nt = n // t
    g = gid.reshape(nt, t)
    counts = jnp.sum((g[:, :, None] == jnp.arange(N_GROUPS, dtype=jnp.int32)).astype(jnp.int32), axis=1)
    nch = (counts + ch - 1) // ch
    rows = nch * ch
    loff = jnp.cumsum(rows, axis=1) - rows
    coff = jnp.cumsum(rows, axis=0) - rows
    gsize = jnp.sum(rows, axis=0)
    gpad = (gsize + tm - 1) // tm * tm
    gstart = jnp.cumsum(gpad) - gpad
    dest = gstart[None, :] + coff
    tab = jnp.concatenate([dest.reshape(-1), loff.reshape(-1), nch.reshape(-1)]).astype(jnp.int32)
    cap = -(-(n + nt * N_GROUPS * ch + N_GROUPS * tm) // tm) * tm
    tstart = jnp.arange(cap // tm, dtype=jnp.int32) * tm
    tgroup = jnp.minimum(jnp.sum((tstart[:, None] >= (gstart + gpad)[None, :]).astype(jnp.int32), axis=1),
                         N_GROUPS - 1)
    tvalid = (tstart < (gstart + gsize)[tgroup]).astype(jnp.int32)
    return tab, tgroup, tvalid, cap


def _tab(tab_ref, section, tile, grp):
    n_entries = tab_ref.shape[0] // 3
    return tab_ref[section * n_entries + tile * N_GROUPS + grp]


def _n_chunks(tab_ref, tile):
    total = _tab(tab_ref, TAB_NCH, tile, 0)
    for c in range(1, N_GROUPS):
        total = total + _tab(tab_ref, TAB_NCH, tile, c)
    return total


def _permutation(tab_ref, tile, g_row):
    t = g_row.shape[1]
    src = lax.broadcasted_iota(jnp.int32, (t, t), 0)
    dst = lax.broadcasted_iota(jnp.int32, (t, t), 1)
    before = jnp.where(src < dst, 1.0, 0.0).astype(BF16)
    grp = lax.broadcasted_iota(jnp.int32, (8, t), 0)
    onehot = jnp.where(grp == g_row, 1.0, 0.0)
    earlier = jnp.dot(onehot.astype(BF16), before, preferred_element_type=F32)
    rank = jnp.sum(onehot * earlier, axis=0, keepdims=True).astype(jnp.int32)
    base = jnp.zeros((1, t), jnp.int32)
    for c in range(N_GROUPS):
        base = jnp.where(g_row == c, _tab(tab_ref, TAB_LOFF, tile, c), base)
    rows = lax.broadcasted_iota(jnp.int32, (DISPATCH_ROWS, t), 0)
    return jnp.where(rows == base + rank, 1.0, 0.0)


def _dispatch_kernel(tab_ref, gid_ref, xa_ref, xs_init_ref, xs_ref, buf_ref, sem_ref):
    del xs_init_ref
    i = pl.program_id(0)
    nt = pl.num_programs(0)
    slot = lax.rem(i, 2)
    ch = DISPATCH_CHUNK

    def chunk_copy(slot_, src_row, dst_row):
        return pltpu.make_async_copy(buf_ref.at[slot_, pl.ds(src_row, ch), :], xs_ref.at[pl.ds(dst_row, ch), :],
                                     sem_ref.at[slot_])

    def wait_tile(tile, slot_):
        def one(k, carry):
            chunk_copy(slot_, 0, 0).wait()
            return carry
        lax.fori_loop(0, _n_chunks(tab_ref, tile), one, 0)

    @pl.when(i >= 2)
    def _():
        wait_tile(i - 2, slot)

    perm = _permutation(tab_ref, i, gid_ref[...]).astype(BF16)
    buf_ref[slot] = jnp.dot(perm, xa_ref[...], preferred_element_type=F32).astype(BF16)
    for c in range(N_GROUPS):
        src0 = _tab(tab_ref, TAB_LOFF, i, c)
        dst0 = _tab(tab_ref, TAB_DEST, i, c)

        def issue(k, carry, src0=src0, dst0=dst0):
            chunk_copy(slot, pl.multiple_of(src0 + k * ch, ch), pl.multiple_of(dst0 + k * ch, ch)).start()
            return carry
        lax.fori_loop(0, _tab(tab_ref, TAB_NCH, i, c), issue, 0)

    @pl.when(i == nt - 1)
    def _():
        wait_tile(i, slot)

        @pl.when(i >= 1)
        def _():
            wait_tile(i - 1, 1 - slot)


def _dispatch(tab, gid, xa, cap):
    n, width = xa.shape
    t = DISPATCH_TILE
    grid_spec = pltpu.PrefetchScalarGridSpec(
        num_scalar_prefetch=1,
        grid=(n // t,),
        in_specs=[
            pl.BlockSpec((None, 1, t), lambda i, tab_ref: (i, 0, 0)),
            pl.BlockSpec((t, width), lambda i, tab_ref: (i, 0)),
            pl.BlockSpec(memory_space=pl.ANY),
        ],
        out_specs=pl.BlockSpec(memory_space=pl.ANY),
        scratch_shapes=[pltpu.VMEM((2, DISPATCH_ROWS, width), BF16), pltpu.SemaphoreType.DMA((2,))],
    )
    return pl.pallas_call(
        _dispatch_kernel,
        grid_spec=grid_spec,
        out_shape=jax.ShapeDtypeStruct((cap, width), BF16),
        input_output_aliases={3: 0},
        compiler_params=pltpu.CompilerParams(dimension_semantics=("arbitrary",),
                                             vmem_limit_bytes=V7X_VMEM_LIMIT),
        name="moe_dispatch",
    )(tab, gid.reshape(n // t, 1, t), xa, jnp.zeros((cap, width), BF16))


def _expert_kernel(tg_ref, tv_ref, xs_ref, wg_ref, wu_ref, wd_ref, ys_ref):
    del tg_ref
    tm = xs_ref.shape[0]
    d = wg_ref.shape[0]

    @pl.when(tv_ref[pl.program_id(0)] == 0)
    def _():
        ys_ref[...] = jnp.zeros(ys_ref.shape, BF16)

    @pl.when(tv_ref[pl.program_id(0)] > 0)
    def _():
        xt = xs_ref[:, 0:d]
        aux = xs_ref[:, d:d + AUX_LANES].astype(F32)
        hg = jnp.dot(xt, wg_ref[...], preferred_element_type=F32)
        hu = jnp.dot(xt, wu_ref[...], preferred_element_type=F32)
        lane = lax.broadcasted_iota(jnp.int32, (tm, AUX_LANES), 1)
        hcol = lax.broadcasted_iota(jnp.int32, (tm, GROUP_HIDDEN), 1) // EXPERT_HIDDEN
        gate_full = jnp.zeros((tm, GROUP_HIDDEN), F32)
        for e in range(EXPERTS_PER_GROUP):
            two_terms = (lane == e) | (lane == e + EXPERTS_PER_GROUP)
            ge = jnp.sum(jnp.where(two_terms, aux, 0.0), axis=-1, keepdims=True)
            gate_full = jnp.where(hcol == e, ge, gate_full)
        act = (hg * jax.nn.sigmoid(hg)) * hu * gate_full
        ys_ref[...] = jnp.dot(act.astype(BF16), wd_ref[...], preferred_element_type=F32).astype(BF16)


def _experts(tgroup, tvalid, xs, wg, wu, wd, *, tm):
    cap, width = xs.shape
    d = wg.shape[1]
    gh = GROUP_HIDDEN
    grid_spec = pltpu.PrefetchScalarGridSpec(
        num_scalar_prefetch=2,
        grid=(cap // tm,),
        in_specs=[
            pl.BlockSpec((tm, width), lambda j, tg, tv: (j, 0)),
            pl.BlockSpec((None, d, gh), lambda j, tg, tv: (tg[j], 0, 0)),
            pl.BlockSpec((None, d, gh), lambda j, tg, tv: (tg[j], 0, 0)),
            pl.BlockSpec((None, gh, d), lambda j, tg, tv: (tg[j], 0, 0)),
        ],
        out_specs=pl.BlockSpec((tm, d), lambda j, tg, tv: (j, 0)),
    )
    return pl.pallas_call(
        _expert_kernel,
        grid_spec=grid_spec,
        out_shape=jax.ShapeDtypeStruct((cap, d), BF16),
        compiler_params=pltpu.CompilerParams(dimension_semantics=("arbitrary",),
                                             vmem_limit_bytes=V7X_VMEM_LIMIT),
        name="moe_experts",
    )(tgroup, tvalid, xs, wg, wu, wd)


def _combine_kernel(tab_ref, gid_ref, x_ref, ys_ref, g_ref, b_ref, o_ref, buf_ref, sem_ref, *, alpha):
    i = pl.program_id(0)
    nt = pl.num_programs(0)
    slot = lax.rem(i, 2)
    ch = DISPATCH_CHUNK

    def chunk_copy(slot_, src_row, dst_row):
        return pltpu.make_async_copy(ys_ref.at[pl.ds(src_row, ch), :], buf_ref.at[slot_, pl.ds(dst_row, ch), :],
                                     sem_ref.at[slot_])

    def fetch(tile, slot_):
        for c in range(N_GROUPS):
            src0 = _tab(tab_ref, TAB_DEST, tile, c)
            dst0 = _tab(tab_ref, TAB_LOFF, tile, c)

            def issue(k, carry, src0=src0, dst0=dst0):
                chunk_copy(slot_, pl.multiple_of(src0 + k * ch, ch), pl.multiple_of(dst0 + k * ch, ch)).start()
                return carry
            lax.fori_loop(0, _tab(tab_ref, TAB_NCH, tile, c), issue, 0)

    @pl.when(i == 0)
    def _():
        buf_ref[...] = jnp.zeros(buf_ref.shape, BF16)
        fetch(0, 0)

    @pl.when(i + 1 < nt)
    def _():
        fetch(i + 1, 1 - slot)

    def one(k, carry):
        chunk_copy(slot, 0, 0).wait()
        return carry
    lax.fori_loop(0, _n_chunks(tab_ref, i), one, 0)

    perm_t = _permutation(tab_ref, i, gid_ref[...]).T.astype(BF16)
    y = jnp.dot(perm_t, buf_ref[slot], preferred_element_type=F32)
    o_ref[...] = _layer_norm(alpha * x_ref[...] + y, g_ref[...], b_ref[...])


def _combine(tab, gid, x1, ys, ln_g, ln_b, *, alpha):
    n, d = x1.shape
    t = DISPATCH_TILE
    grid_spec = pltpu.PrefetchScalarGridSpec(
        num_scalar_prefetch=1,
        grid=(n // t,),
        in_specs=[
            pl.BlockSpec((None, 1, t), lambda i, tab_ref: (i, 0, 0)),
            pl.BlockSpec((t, d), lambda i, tab_ref: (i, 0)),
            pl.BlockSpec(memory_space=pl.ANY),
            pl.BlockSpec((1, d), lambda i, tab_ref: (0, 0)),
            pl.BlockSpec((1, d), lambda i, tab_ref: (0, 0)),
        ],
        out_specs=pl.BlockSpec((t, d), lambda i, tab_ref: (i, 0)),
        scratch_shapes=[pltpu.VMEM((2, DISPATCH_ROWS, d), BF16), pltpu.SemaphoreType.DMA((2,))],
    )
    return pl.pallas_call(
        functools.partial(_combine_kernel, alpha=alpha),
        grid_spec=grid_spec,
        out_shape=jax.ShapeDtypeStruct((n, d), F32),
        compiler_params=pltpu.CompilerParams(dimension_semantics=("arbitrary",),
                                             vmem_limit_bytes=V7X_VMEM_LIMIT),
        name="moe_combine",
    )(tab, gid.reshape(n // t, 1, t), x1, ys, ln_g, ln_b)


def _block_diag(blocks):
    n, r, c = blocks.shape
    eye = jnp.eye(n, dtype=blocks.dtype)
    return jnp.einsum("grc,gh->grhc", blocks, eye).reshape(n * r, n * c)


def kernel(x, w_in, pool_w, pool_scale, conv_w, lam_q1, lam_k1, lam_q2, lam_k2, subln_g, sg_ln_g, sg_ln_b, sg_w, sg_b, w_branch, w_o, ln1_g, ln1_b, w_rg, b_rg, w_re, b_re, w_gate, w_up, w_down, ln2_g, ln2_b):
    b, s, d = x.shape
    depth = w_in.shape[0]
    n = b * s
    w = BRANCH_W
    alpha = (2 * depth) ** 0.25
    ta = min(512, s)
    tq = min(512, s)
    tc = min(512, s)
    tm = MOE_TILE

    half = DA_HEAD_DIM // 2
    inv_freq = ROPE_THETA ** (-jnp.arange(half, dtype=F32) / half)
    ang = jnp.arange(s, dtype=F32)[:, None] * inv_freq[None, :]
    cos, sin = jnp.cos(ang), jnp.sin(ang)
    reps = w // DA_HEAD_DIM
    cos_t = jnp.tile(jnp.concatenate([cos, cos], axis=-1), (1, reps))
    sin_t = jnp.tile(jnp.concatenate([-sin, sin], axis=-1), (1, reps))

    for l in range(depth):
        lam_init = 0.8 - 0.6 * math.exp(-0.3 * l)
        w_in_b = w_in[l].astype(BF16)
        w_loc = w_in_b[:, :COL_GATE]
        w_gate_in = w_in_b[:, COL_GATE:]
        pool_bd = _block_diag(pool_w[l]).astype(BF16)
        sg_wcat = jnp.transpose(sg_w[l], (1, 0, 2)).reshape(SG_CHUNK, SG_GROUPS * SG_CHUNK).astype(BF16)
        sg_bias = jnp.repeat(sg_b[l].T, SG_GW, axis=1)
        lam_rows = jnp.zeros((8, 128), F32).at[0:4, 0:DA_HEAD_DIM].set(
            jnp.stack([lam_q1[l], lam_k1[l], lam_q2[l], lam_k2[l]]).astype(F32))
        subln_cols = jnp.broadcast_to(subln_g[l][:, None], (DA_V_DIM, tq))
        w_router = jnp.concatenate([w_rg[l], jnp.transpose(w_re[l], (1, 0, 2)).reshape(d, N_EXPERTS)], axis=1)
        w_router = jnp.pad(w_router, ((0, 0), (0, ROUTER_LANES - w_router.shape[1])))
        wr_hi = w_router.astype(BF16)
        wr_lo = (w_router - wr_hi.astype(F32)).astype(BF16)
        r_bias = jnp.pad(jnp.concatenate([b_rg[l], b_re[l].reshape(-1)]), (0, ROUTER_LANES - N_GROUPS - N_EXPERTS))[None, :]
        wg = jnp.transpose(w_gate[l].reshape(N_GROUPS, EXPERTS_PER_GROUP, d, EXPERT_HIDDEN), (0, 2, 1, 3))
        wu = jnp.transpose(w_up[l].reshape(N_GROUPS, EXPERTS_PER_GROUP, d, EXPERT_HIDDEN), (0, 2, 1, 3))
        wg = wg.reshape(N_GROUPS, d, GROUP_HIDDEN).astype(BF16)
        wu = wu.reshape(N_GROUPS, d, GROUP_HIDDEN).astype(BF16)
        wd = w_down[l].reshape(N_GROUPS, GROUP_HIDDEN, d).astype(BF16)

        yloc, q, k, vt = _local_mixer(x, w_loc, cos_t, sin_t, pool_bd, pool_scale[l][None, :], conv_w[l],
                                      sg_ln_g[l][None, :], sg_ln_b[l][None, :], sg_wcat, sg_bias, ta=ta)
        yct = _diff_attn(lam_rows, q, k, vt, subln_cols, lam_init=lam_init, tq=tq)
        x1, xa, gid = _merge(x.reshape(n, d), yloc.reshape(n, 3 * w), yct, w_gate_in,
                             w_branch[l].astype(BF16), w_o[l].astype(BF16), ln1_g[l][None, :], ln1_b[l][None, :],
                             wr_hi, wr_lo, r_bias, alpha=alpha, tc=tc)
        tab, tgroup, tvalid, cap = _dispatch_tables(gid, n, tm)
        xs = _dispatch(tab, gid, xa, cap)
        ys = _experts(tgroup, tvalid, xs, wg, wu, wd, tm=tm)
        x2 = _combine(tab, gid, x1, ys, ln2_g[l][None, :], ln2_b[l][None, :], alpha=alpha)
        x = x2.reshape(b, s, d)
    return x
```

```python
import functools
import math

import jax
import jax.numpy as jnp
from jax import lax
from jax.experimental import pallas as pl
from jax.experimental.pallas import tpu as pltpu

F32 = jnp.float32
BF16 = jnp.bfloat16

BRANCH_W = 256
POOL_WINDOWS = (2, 4, 8, 16)
POOL_GW = 64
MAX_POOL = 16
CONV_W = 3
DA_HEADS = 4
DA_HEAD_DIM = 32
DA_V_DIM = 64
ROPE_THETA = 10000.0
SG_CHUNK = 128
SG_GROUPS = 4
SG_GW = 64
N_GROUPS = 4
EXPERTS_PER_GROUP = 4
N_EXPERTS = 16
EXPERT_HIDDEN = 256
GROUP_HIDDEN = EXPERTS_PER_GROUP * EXPERT_HIDDEN
N_BRANCH = 4
LN_EPS = 1e-5
NEG_INF = -1e30
ROUTER_LANES = 128
V7X_VMEM_LIMIT = 56 * 1024 * 1024

COL_POOL = 0
COL_CONV = BRANCH_W
COL_ATTN = 4 * BRANCH_W
COL_SG = 7 * BRANCH_W
COL_GATE = 9 * BRANCH_W


def _layer_norm(h, g, b):
    mu = jnp.mean(h, axis=-1, keepdims=True)
    hc = h - mu
    var = jnp.mean(hc * hc, axis=-1, keepdims=True)
    return hc * lax.rsqrt(var + LN_EPS) * g + b


def _gelu_tanh(x):
    c = math.sqrt(2.0 / math.pi)
    return 0.5 * x * (1.0 + jnp.tanh(c * (x + 0.044715 * (x * x * x))))


def _local_mixer_kernel(x_ref, w_ref, cos_ref, sin_ref, poolw_ref, pscale_ref, convw_ref, lng_ref, lnb_ref,
                        sgw_ref, sgb_ref, yloc_ref, q_ref, k_ref, vt_ref, pext_ref, zext_ref):
    t = pl.program_id(1)
    ta = x_ref.shape[0]
    w = BRANCH_W
    xb = x_ref[...].astype(BF16)
    lane = lax.broadcasted_iota(jnp.int32, (ta, w), 1)
    row = lax.broadcasted_iota(jnp.int32, (ta, w), 0)

    @pl.when(t == 0)
    def _():
        pext_ref[0:MAX_POOL, :] = jnp.zeros((MAX_POOL, w), F32)
        zext_ref[0:8, :] = jnp.zeros((8, w), F32)

    @pl.when(t > 0)
    def _():
        pext_ref[0:MAX_POOL, :] = pext_ref[ta:ta + MAX_POOL, :]
        zext_ref[0:8, :] = zext_ref[ta:ta + 8, :]

    p = jnp.dot(xb, w_ref[:, COL_POOL:COL_POOL + w], preferred_element_type=F32)
    pext_ref[MAX_POOL:MAX_POOL + ta, :] = p

    def prev(kk):
        return pext_ref[pl.ds(MAX_POOL - kk, ta), :]

    s2 = p + prev(1)
    s4 = s2 + (prev(2) + prev(3))
    s8 = s4 + ((prev(4) + prev(5)) + (prev(6) + prev(7)))
    s16 = s8 + (((prev(8) + prev(9)) + (prev(10) + prev(11))) + ((prev(12) + prev(13)) + (prev(14) + prev(15))))
    grp = lane // POOL_GW
    win_sum = jnp.where(grp == 0, s2, jnp.where(grp == 1, s4, jnp.where(grp == 2, s8, s16)))
    win = jnp.where(grp == 0, 2, jnp.where(grp == 1, 4, jnp.where(grp == 2, 8, 16)))
    count = jnp.minimum(t * ta + row + 1, win).astype(F32)
    d = (win_sum / count - p).astype(BF16)
    y_a = jnp.dot(d, poolw_ref[...], preferred_element_type=F32) * pscale_ref[...]
    yloc_ref[:, 0:w] = y_a.astype(BF16)

    pc = jnp.dot(xb, w_ref[:, COL_CONV:COL_CONV + 3 * w], preferred_element_type=F32)
    gb = pc[:, 0:w]
    z = pc[:, w:2 * w] * pc[:, 2 * w:3 * w]
    zext_ref[8:8 + ta, :] = z
    cw = convw_ref[...]
    y_b = zext_ref[pl.ds(6, ta), :] * cw[0:1, :] + zext_ref[pl.ds(7, ta), :] * cw[1:2, :] + z * cw[2:3, :]
    yloc_ref[:, w:2 * w] = (gb * y_b).astype(BF16)

    pa = jnp.dot(xb, w_ref[:, COL_ATTN:COL_ATTN + 3 * w], preferred_element_type=F32)
    cos = cos_ref[...]
    sin = sin_ref[...]
    first_half = (lane % DA_HEAD_DIM) < (DA_HEAD_DIM // 2)

    def rope(u):
        swapped = jnp.where(first_half, pltpu.roll(u, w - DA_HEAD_DIM // 2, axis=1),
                            pltpu.roll(u, DA_HEAD_DIM // 2, axis=1))
        return u * cos + swapped * sin

    q_ref[...] = (rope(pa[:, 0:w]) * (DA_HEAD_DIM ** -0.5 * math.log2(math.e))).astype(BF16)
    k_ref[...] = rope(pa[:, w:2 * w]).astype(BF16)
    vt_ref[...] = pa[:, 2 * w:3 * w].T.astype(BF16)

    uv = _gelu_tanh(jnp.dot(xb, w_ref[:, COL_SG:COL_SG + 2 * w], preferred_element_type=F32))
    u = uv[:, 0:w]
    vn = _layer_norm(uv[:, w:2 * w], lng_ref[...], lnb_ref[...])
    wrow = lax.broadcasted_iota(jnp.int32, (SG_CHUNK, SG_GROUPS * SG_CHUNK), 0)
    wcol = lax.broadcasted_iota(jnp.int32, (SG_CHUNK, SG_GROUPS * SG_CHUNK), 1)
    ws = jnp.where(wrow >= (wcol % SG_CHUNK), sgw_ref[...], jnp.zeros((), BF16))
    cgrp = lax.broadcasted_iota(jnp.int32, (SG_CHUNK, w), 1) // SG_GW
    ys = []
    for c in range(ta // SG_CHUNK):
        vc = vn[c * SG_CHUNK:(c + 1) * SG_CHUNK, :]
        rhs = jnp.concatenate([jnp.where(cgrp == g, vc, 0.0) for g in range(SG_GROUPS)], axis=0).astype(BF16)
        ys.append(jnp.dot(ws, rhs, preferred_element_type=F32) + sgb_ref[...])
    y_d = u * jnp.concatenate(ys, axis=0)
    yloc_ref[:, 2 * w:3 * w] = y_d.astype(BF16)


def _local_mixer(x, w_loc, cos_t, sin_t, pool_bd, pool_scale, conv_w, sg_ln_g, sg_ln_b, sg_wcat, sg_bias, *, ta):
    b, s, d = x.shape
    w = BRANCH_W
    ncol = w_loc.shape[1]
    full = lambda shape: pl.BlockSpec(shape, lambda i, j: (0,) * len(shape))
    seq_blk = lambda width: pl.BlockSpec((None, ta, width), lambda i, j: (i, j, 0))
    return pl.pallas_call(
        _local_mixer_kernel,
        grid=(b, s // ta),
        in_specs=[
            seq_blk(d),
            full((d, ncol)),
            pl.BlockSpec((ta, w), lambda i, j: (j, 0)),
            pl.BlockSpec((ta, w), lambda i, j: (j, 0)),
            full((w, w)), full((1, w)), full((CONV_W, w)), full((1, w)), full((1, w)),
            full((SG_CHUNK, SG_GROUPS * SG_CHUNK)), full((SG_CHUNK, w)),
        ],
        out_specs=[seq_blk(3 * w), seq_blk(w), seq_blk(w), pl.BlockSpec((None, w, ta), lambda i, j: (i, 0, j))],
        out_shape=[
            jax.ShapeDtypeStruct((b, s, 3 * w), BF16),
            jax.ShapeDtypeStruct((b, s, w), BF16),
            jax.ShapeDtypeStruct((b, s, w), BF16),
            jax.ShapeDtypeStruct((b, w, s), BF16),
        ],
        scratch_shapes=[pltpu.VMEM((ta + MAX_POOL, w), F32), pltpu.VMEM((ta + 8, w), F32)],
        compiler_params=pltpu.CompilerParams(dimension_semantics=("parallel", "arbitrary"),
                                             vmem_limit_bytes=V7X_VMEM_LIMIT),
        name="local_mixer",
    )(x, w_loc, cos_t, sin_t, pool_bd, pool_scale, conv_w, sg_ln_g, sg_ln_b, sg_wcat, sg_bias)


ATTN_ONES_ROWS = 16
ATTN_LANE_CHUNK = 256


def _diff_attn_kernel(lam_ref, q_ref, k_ref, vt_ref, g_ref, o_ref, q2_ref, s0_ref, s1_ref, cm0_ref, cm1_ref,
                      m_ref, acc_ref, *, lam_init):
    qi = pl.program_id(1)
    tq = q_ref.shape[0]
    tk = tq
    w = BRANCH_W
    nt_dims = (((1,), (1,)), ((), ()))

    q = q_ref[...]
    sub = lax.broadcasted_iota(jnp.int32, (tq, w), 1) // DA_HEAD_DIM
    zero = jnp.zeros((), BF16)
    for h in range(DA_HEADS):
        q2_ref[h, 0:tq, :] = jnp.where(sub == 2 * h, q, zero)
        q2_ref[h, tq:2 * tq, :] = jnp.where(sub == 2 * h + 1, q, zero)

    lam_rows = lam_ref[...]
    lam = (jnp.exp(jnp.sum(lam_rows[0:1, :] * lam_rows[1:2, :], axis=-1, keepdims=True))
           - jnp.exp(jnp.sum(lam_rows[2:3, :] * lam_rows[3:4, :], axis=-1, keepdims=True)) + lam_init)

    def column_max(sc):
        parts = [sc[r * 8:(r + 1) * 8, :] for r in range(sc.shape[0] // 8)]
        while len(parts) > 1:
            parts = [jnp.maximum(parts[2 * r], parts[2 * r + 1]) for r in range(len(parts) // 2)]
        return jnp.max(parts[0], axis=0, keepdims=True)

    bufs = ((s0_ref, cm0_ref), (s1_ref, cm1_ref))

    def scores(h, kv, dst):
        dst_ref, dst_max_ref = dst
        kt = k_ref[pl.ds(pl.multiple_of(kv * tk, tk), tk), :]
        for c in range(2 * tq // ATTN_LANE_CHUNK):
            cols = slice(c * ATTN_LANE_CHUNK, (c + 1) * ATTN_LANE_CHUNK)
            sc = lax.dot_general(kt, q2_ref[h, cols, :], nt_dims, preferred_element_type=F32)
            dst_ref[:, cols] = sc
            dst_max_ref[:, cols] = column_max(sc)

    def mask_diagonal(cur):
        cur_ref, cur_max_ref = cur
        key = lax.broadcasted_iota(jnp.int32, (tk, ATTN_LANE_CHUNK), 0)
        lane = lax.broadcasted_iota(jnp.int32, (tk, ATTN_LANE_CHUNK), 1)
        for c in range(2 * tq // ATTN_LANE_CHUNK):
            cols = slice(c * ATTN_LANE_CHUNK, (c + 1) * ATTN_LANE_CHUNK)
            qry = lane + (c * ATTN_LANE_CHUNK) % tq
            sc = jnp.where(key <= qry, cur_ref[:, cols], NEG_INF)
            cur_ref[:, cols] = sc
            cur_max_ref[:, cols] = column_max(sc)

    def softmax_pv(h, kv, cur):
        cur_ref, cur_max_ref = cur
        vth = vt_ref[h * DA_V_DIM:(h + 1) * DA_V_DIM, pl.ds(pl.multiple_of(kv * tk, tk), tk)]
        lhs = jnp.concatenate([vth, jnp.ones((ATTN_ONES_ROWS, tk), BF16)], axis=0)
        for c in range(2 * tq // ATTN_LANE_CHUNK):
            cols = slice(c * ATTN_LANE_CHUNK, (c + 1) * ATTN_LANE_CHUNK)
            sc = cur_ref[:, cols]
            m_old = m_ref[h, :, cols]
            m_new = jnp.maximum(m_old, cur_max_ref[:, cols])
            e = jnp.exp2(sc - m_new).astype(BF16)
            pv = jnp.dot(lhs, e, preferred_element_type=F32)
            acc_ref[h, :, cols] = acc_ref[h, :, cols] * jnp.exp2(m_old - m_new) + pv
            m_ref[h, :, cols] = m_new

    def finalize(h):
        o1 = acc_ref[h, 0:DA_V_DIM, 0:tq]
        o2 = acc_ref[h, 0:DA_V_DIM, tq:2 * tq]
        r1 = 1.0 / acc_ref[h, DA_V_DIM:DA_V_DIM + 1, 0:tq]
        r2 = 1.0 / acc_ref[h, DA_V_DIM:DA_V_DIM + 1, tq:2 * tq]
        a = o1 * r1 - lam * (o2 * r2)
        ms = jnp.mean(a * a, axis=0, keepdims=True)
        y = (a * lax.rsqrt(ms + LN_EPS)) * (1.0 - lam_init) * g_ref[...]
        o_ref[h * DA_V_DIM:(h + 1) * DA_V_DIM, :] = y.astype(BF16)

    m_ref[...] = jnp.full(m_ref.shape, NEG_INF, F32)
    acc_ref[...] = jnp.zeros(acc_ref.shape, F32)
    scores(0, 0, bufs[0])

    def full_tile(kv, carry):
        for h in range(DA_HEADS):
            nxt_h, nxt_kv = (h + 1, kv) if h + 1 < DA_HEADS else (0, kv + 1)
            scores(nxt_h, nxt_kv, bufs[(h + 1) % 2])
            softmax_pv(h, kv, bufs[h % 2])
        return carry

    lax.fori_loop(0, qi, full_tile, 0)

    for h in range(DA_HEADS):
        mask_diagonal(bufs[h % 2])
        if h + 1 < DA_HEADS:
            scores(h + 1, qi, bufs[(h + 1) % 2])
        softmax_pv(h, qi, bufs[h % 2])
        finalize(h)


def _diff_attn(lam_rows, q, k, vt, subln_cols, *, lam_init, tq):
    b, s, w = q.shape
    return pl.pallas_call(
        functools.partial(_diff_attn_kernel, lam_init=lam_init),
        grid=(b, s // tq),
        in_specs=[
            pl.BlockSpec((8, 128), lambda i, j: (0, 0)),
            pl.BlockSpec((None, tq, w), lambda i, j: (i, j, 0)),
            pl.BlockSpec((None, s, w), lambda i, j: (i, 0, 0)),
            pl.BlockSpec((None, w, s), lambda i, j: (i, 0, 0)),
            pl.BlockSpec((DA_V_DIM, tq), lambda i, j: (0, 0)),
        ],
        out_specs=pl.BlockSpec((None, w, tq), lambda i, j: (i, 0, j)),
        out_shape=jax.ShapeDtypeStruct((b, w, s), BF16),
        scratch_shapes=[pltpu.VMEM((DA_HEADS, 2 * tq, w), BF16),
                        pltpu.VMEM((tq, 2 * tq), F32), pltpu.VMEM((tq, 2 * tq), F32),
                        pltpu.VMEM((1, 2 * tq), F32), pltpu.VMEM((1, 2 * tq), F32),
                        pltpu.VMEM((DA_HEADS, 1, 2 * tq), F32),
                        pltpu.VMEM((DA_HEADS, DA_V_DIM + ATTN_ONES_ROWS, 2 * tq), F32)],
        compiler_params=pltpu.CompilerParams(dimension_semantics=("parallel", "parallel"),
                                             vmem_limit_bytes=V7X_VMEM_LIMIT),
        name="diff_attn",
    )(lam_rows, q, k, vt, subln_cols)


def _route(x, wr2, bias):
    rows = x.shape[0]
    xh = x.astype(BF16)
    xl = (x - xh.astype(F32)).astype(BF16)
    prod = jnp.dot(jnp.concatenate([xh, xl], axis=1), wr2, preferred_element_type=F32)
    logits = prod[:, 0:ROUTER_LANES] + prod[:, ROUTER_LANES:2 * ROUTER_LANES] + bias
    lt = logits.T

    def first_argmax(vals):
        best = vals[0]
        for v in vals[1:]:
            best = jnp.maximum(best, v)
        idx = jnp.full(best.shape, len(vals) - 1, jnp.int32)
        for i in range(len(vals) - 2, -1, -1):
            idx = jnp.where(vals[i] == best, i, idx)
        return best, idx

    gl = [lt[c:c + 1, :] for c in range(N_GROUPS)]
    gmax, g_sel = first_argmax(gl)
    denom = jnp.exp(gl[0] - gmax)
    for c in range(1, N_GROUPS):
        denom = denom + jnp.exp(gl[c] - gmax)
    p_sel = 1.0 / denom
    el = []
    for e in range(EXPERTS_PER_GROUP):
        row = N_GROUPS + EXPERTS_PER_GROUP * (N_GROUPS - 1) + e
        v = lt[row:row + 1, :]
        for c in range(N_GROUPS - 2, -1, -1):
            row = N_GROUPS + EXPERTS_PER_GROUP * c + e
            v = jnp.where(g_sel == c, lt[row:row + 1, :], v)
        el.append(v)
    v1, i1 = first_argmax(el)
    v2, i2 = first_argmax([jnp.where(i1 == e, NEG_INF, el[e]) for e in range(EXPERTS_PER_GROUP)])
    e2 = jnp.exp(v2 - v1)
    w1 = p_sel / (1.0 + e2)
    w2 = w1 * e2
    gates = [jnp.where(i1 == e, w1, jnp.where(i2 == e, w2, 0.0)) for e in range(EXPERTS_PER_GROUP)]
    hi = [g.astype(BF16).astype(F32) for g in gates]
    lo = [g - h for g, h in zip(gates, hi)]
    pad = jnp.zeros((ROUTER_LANES - 2 * EXPERTS_PER_GROUP, rows), F32)
    aux = jnp.concatenate(hi + lo + [pad], axis=0).T
    return aux, g_sel


def _merge_kernel(x_ref, yloc_ref, yct_ref, wgate_ref, wbr_ref, wo_ref, g_ref, b_ref, wr2_ref, rb_ref,
                  o_ref, xa_ref, gid_ref, *, alpha):
    w = BRANCH_W
    tc, d = x_ref.shape
    x = x_ref[...]
    xb = x.astype(BF16)
    y_c = yct_ref[...].astype(F32).T.astype(BF16)
    branches = (yloc_ref[:, 0:w], yloc_ref[:, w:2 * w], y_c, yloc_ref[:, 2 * w:3 * w])
    merged = None
    for i in range(N_BRANCH):
        gate = jax.nn.sigmoid(jnp.dot(xb, wgate_ref[:, i * d:(i + 1) * d], preferred_element_type=F32))
        term = gate * jnp.dot(branches[i], wbr_ref[i], preferred_element_type=F32)
        merged = term if merged is None else merged + term
    mix = jnp.dot(merged.astype(BF16), wo_ref[...], preferred_element_type=F32)
    x1 = _layer_norm(alpha * x + mix, g_ref[...], b_ref[...])
    o_ref[...] = x1

    aux, g_row = _route(x1, wr2_ref[...], rb_ref[...])
    xa_ref[:, 0:d] = x1.astype(BF16)
    xa_ref[:, d:d + AUX_LANES] = aux.astype(BF16)
    gid_ref[...] = g_row


def _merge(x, yloc, yct, w_gate_in, w_branch, w_o, ln_g, ln_b, wr2, r_bias, *, alpha, tc):
    n, d = x.shape
    w = BRANCH_W
    s = yct.shape[2]
    tiles_per_row = s // tc
    tok = lambda width: pl.BlockSpec((tc, width), lambda i: (i, 0))
    const = lambda shape: pl.BlockSpec(shape, lambda i: (0,) * len(shape))
    return pl.pallas_call(
        functools.partial(_merge_kernel, alpha=alpha),
        grid=(n // tc,),
        in_specs=[
            tok(d), tok(3 * w),
            pl.BlockSpec((None, w, tc), lambda i: (i // tiles_per_row, 0, i % tiles_per_row)),
            const((d, N_BRANCH * d)), const((N_BRANCH, w, d)), const((d, d)), const((1, d)), const((1, d)),
            const((2 * d, 2 * ROUTER_LANES)), const((1, ROUTER_LANES)),
        ],
        out_specs=[tok(d), tok(d + AUX_LANES), pl.BlockSpec((None, 1, tc), lambda i: (i, 0, 0))],
        out_shape=[jax.ShapeDtypeStruct((n, d), F32), jax.ShapeDtypeStruct((n, d + AUX_LANES), BF16),
                   jax.ShapeDtypeStruct((n // tc, 1, tc), jnp.int32)],
        compiler_params=pltpu.CompilerParams(dimension_semantics=("parallel",),
                                             vmem_limit_bytes=V7X_VMEM_LIMIT),
        name="gated_merge",
    )(x, yloc, yct, w_gate_in, w_branch, w_o, ln_g, ln_b, wr2, r_bias)


DISPATCH_TILE = 256
DISPATCH_CHUNK = 16
DISPATCH_ROWS = 384
AUX_LANES = 128
MOE_TILE = 512
TAB_DEST, TAB_LOFF, TAB_NCH = 0, 1, 2


def _dispatch_tables(gid, n, tm):
    t, ch = DISPATCH_TILE, DISPATCH_CHUNK
    nt = n // t
    g = gid.reshape(nt, t)
    counts = jnp.sum((g[:, :, None] == jnp.arange(N_GROUPS, dtype=jnp.int32)).astype(jnp.int32), axis=1)
    nch = (counts + ch - 1) // ch
    rows = nch * ch
    loff = jnp.cumsum(rows, axis=1) - rows
    coff = jnp.cumsum(rows, axis=0) - rows
    gsize = jnp.sum(rows, axis=0)
    gpad = (gsize + tm - 1) // tm * tm
    gstart = jnp.cumsum(gpad) - gpad
    dest = gstart[None, :] + coff
    tab = jnp.concatenate([dest.reshape(-1), loff.reshape(-1), nch.reshape(-1)]).astype(jnp.int32)
    cap = -(-(n + nt * N_GROUPS * ch + N_GROUPS * tm) // tm) * tm
    used = jnp.sum(gpad)
    gtab = jnp.concatenate([gstart + gsize, (gpad - gsize) // ch, used[None], (cap - used)[None] // tm])
    gtab = gtab.astype(jnp.int32)
    tile = jnp.arange(cap // tm, dtype=jnp.int32)
    tgroup = jnp.minimum(jnp.sum((tile[:, None] * tm >= (gstart + gpad)[None, :]).astype(jnp.int32), axis=1),
                         N_GROUPS - 1)
    tsrc = jnp.minimum(tile, jnp.sum(gpad) // tm - 1)
    return tab, gtab, tgroup, tsrc, cap


def _tab(tab_ref, section, tile, grp):
    n_entries = tab_ref.shape[0] // 3
    return tab_ref[section * n_entries + tile * N_GROUPS + grp]


def _n_chunks(tab_ref, tile):
    total = _tab(tab_ref, TAB_NCH, tile, 0)
    for c in range(1, N_GROUPS):
        total = total + _tab(tab_ref, TAB_NCH, tile, c)
    return total


def _permutation(tab_ref, tile, g_row):
    t = g_row.shape[1]
    src = lax.broadcasted_iota(jnp.int32, (t, t), 0)
    dst = lax.broadcasted_iota(jnp.int32, (t, t), 1)
    before = jnp.where(src < dst, 1.0, 0.0).astype(BF16)
    grp = lax.broadcasted_iota(jnp.int32, (8, t), 0)
    onehot = jnp.where(grp == g_row, 1.0, 0.0)
    earlier = jnp.dot(onehot.astype(BF16), before, preferred_element_type=F32)
    rank = jnp.sum(onehot * earlier, axis=0, keepdims=True).astype(jnp.int32)
    base = jnp.zeros((1, t), jnp.int32)
    for c in range(N_GROUPS):
        base = jnp.where(g_row == c, _tab(tab_ref, TAB_LOFF, tile, c), base)
    rows = lax.broadcasted_iota(jnp.int32, (DISPATCH_ROWS, t), 0)
    return jnp.where(rows == base + rank, 1.0, 0.0)


def _dispatch_kernel(tab_ref, gtab_ref, gid_ref, xa_ref, xs_ref, buf_ref, zero_ref, sem_ref):
    i = pl.program_id(0)
    nt = pl.num_programs(0)
    slot = lax.rem(i, 2)
    ch = DISPATCH_CHUNK

    def chunk_copy(slot_, src_row, dst_row):
        return pltpu.make_async_copy(buf_ref.at[slot_, pl.ds(src_row, ch), :], xs_ref.at[pl.ds(dst_row, ch), :],
                                     sem_ref.at[slot_])

    def wait_tile(tile, slot_):
        def one(k, carry):
            chunk_copy(slot_, 0, 0).wait()
            return carry
        lax.fori_loop(0, _n_chunks(tab_ref, tile), one, 0)

    @pl.when(i >= 2)
    def _():
        wait_tile(i - 2, slot)

    perm = _permutation(tab_ref, i, gid_ref[...]).astype(BF16)
    buf_ref[slot] = jnp.dot(perm, xa_ref[...], preferred_element_type=F32).astype(BF16)
    for c in range(N_GROUPS):
        src0 = _tab(tab_ref, TAB_LOFF, i, c)
        dst0 = _tab(tab_ref, TAB_DEST, i, c)

        def issue(k, carry, src0=src0, dst0=dst0):
            chunk_copy(slot, pl.multiple_of(src0 + k * ch, ch), pl.multiple_of(dst0 + k * ch, ch)).start()
            return carry
        lax.fori_loop(0, _tab(tab_ref, TAB_NCH, i, c), issue, 0)

    @pl.when(i == nt - 1)
    def _():
        zero_ref[...] = jnp.zeros(zero_ref.shape, BF16)

        def tail_copy(dst_row):
            return pltpu.make_async_copy(zero_ref.at[pl.ds(0, ch), :], xs_ref.at[pl.ds(dst_row, ch), :],
                                         sem_ref.at[2])

        tile_rows = zero_ref.shape[0]

        def spare_copy(dst_row):
            return pltpu.make_async_copy(zero_ref, xs_ref.at[pl.ds(dst_row, tile_rows), :], sem_ref.at[3])

        def fill_spare(k, carry):
            spare_copy(pl.multiple_of(gtab_ref[2 * N_GROUPS] + k * tile_rows, tile_rows)).start()
            return carry
        lax.fori_loop(0, gtab_ref[2 * N_GROUPS + 1], fill_spare, 0)

        def drain_spare(k, carry):
            spare_copy(0).wait()
            return carry
        lax.fori_loop(0, gtab_ref[2 * N_GROUPS + 1], drain_spare, 0)

        n_tail = 0
        for c in range(N_GROUPS):
            dst0 = gtab_ref[c]

            def fill(k, carry, dst0=dst0):
                tail_copy(pl.multiple_of(dst0 + k * ch, ch)).start()
                return carry
            lax.fori_loop(0, gtab_ref[N_GROUPS + c], fill, 0)
            n_tail = n_tail + gtab_ref[N_GROUPS + c]

        def drain(k, carry):
            tail_copy(0).wait()
            return carry
        lax.fori_loop(0, n_tail, drain, 0)

        wait_tile(i, slot)

        @pl.when(i >= 1)
        def _():
            wait_tile(i - 1, 1 - slot)


def _dispatch(tab, gtab, gid, xa, cap):
    n, width = xa.shape
    t = DISPATCH_TILE
    grid_spec = pltpu.PrefetchScalarGridSpec(
        num_scalar_prefetch=2,
        grid=(n // t,),
        in_specs=[
            pl.BlockSpec((None, 1, t), lambda i, tab_ref, gtab_ref: (i, 0, 0)),
            pl.BlockSpec((t, width), lambda i, tab_ref, gtab_ref: (i, 0)),
        ],
        out_specs=pl.BlockSpec(memory_space=pl.ANY),
        scratch_shapes=[pltpu.VMEM((2, DISPATCH_ROWS, width), BF16), pltpu.VMEM((MOE_TILE, width), BF16),
                        pltpu.SemaphoreType.DMA((4,))],
    )
    return pl.pallas_call(
        _dispatch_kernel,
        grid_spec=grid_spec,
        out_shape=jax.ShapeDtypeStruct((cap, width), BF16),
        compiler_params=pltpu.CompilerParams(dimension_semantics=("arbitrary",),
                                             vmem_limit_bytes=V7X_VMEM_LIMIT),
        name="moe_dispatch",
    )(tab, gtab, gid.reshape(n // t, 1, t), xa)


def _expert_kernel(tg_ref, ts_ref, xs_ref, wg_ref, wu_ref, wd_ref, ys_ref):
    del tg_ref
    j = pl.program_id(0)
    tm = xs_ref.shape[0]
    d = wg_ref.shape[1]

    @pl.when(ts_ref[j] != j)
    def _():
        ys_ref[...] = jnp.zeros(ys_ref.shape, BF16)

    @pl.when(ts_ref[j] == j)
    def _():
        xt = xs_ref[:, 0:d]
        aux = xs_ref[:, d:d + AUX_LANES].astype(F32)
        lane = lax.broadcasted_iota(jnp.int32, (tm, AUX_LANES), 1)
        acts = []
        for e in range(EXPERTS_PER_GROUP):
            hg = jnp.dot(xt, wg_ref[e], preferred_element_type=F32)
            hu = jnp.dot(xt, wu_ref[e], preferred_element_type=F32)
            two_terms = (lane == e) | (lane == e + EXPERTS_PER_GROUP)
            ge = jnp.sum(jnp.where(two_terms, aux, 0.0), axis=-1, keepdims=True)
            acts.append(((hg * jax.nn.sigmoid(hg)) * hu * ge).astype(BF16))
        act = jnp.concatenate(acts, axis=1)
        ys_ref[...] = jnp.dot(act, wd_ref[...], preferred_element_type=F32).astype(BF16)


def _experts(tgroup, tsrc, xs, wg, wu, wd, *, tm):
    cap, width = xs.shape
    d = wg.shape[1]
    gh = GROUP_HIDDEN
    epg = EXPERTS_PER_GROUP
    grid_spec = pltpu.PrefetchScalarGridSpec(
        num_scalar_prefetch=2,
        grid=(cap // tm,),
        in_specs=[
            pl.BlockSpec((tm, width), lambda j, tg, ts: (ts[j], 0)),
            pl.BlockSpec((epg, d, EXPERT_HIDDEN), lambda j, tg, ts: (tg[j], 0, 0)),
            pl.BlockSpec((epg, d, EXPERT_HIDDEN), lambda j, tg, ts: (tg[j], 0, 0)),
            pl.BlockSpec((None, gh, d), lambda j, tg, ts: (tg[j], 0, 0)),
        ],
        out_specs=pl.BlockSpec((tm, d), lambda j, tg, ts: (j, 0)),
    )
    return pl.pallas_call(
        _expert_kernel,
        grid_spec=grid_spec,
        out_shape=jax.ShapeDtypeStruct((cap, d), BF16),
        compiler_params=pltpu.CompilerParams(dimension_semantics=("arbitrary",),
                                             vmem_limit_bytes=V7X_VMEM_LIMIT),
        name="moe_experts",
    )(tgroup, tsrc, xs, wg, wu, wd)


def _combine_kernel(tab_ref, gid_ref, x_ref, ys_ref, g_ref, b_ref, o_ref, buf_ref, sem_ref, *, alpha):
    i = pl.program_id(0)
    nt = pl.num_programs(0)
    slot = lax.rem(i, 2)
    ch = DISPATCH_CHUNK

    def chunk_copy(slot_, src_row, dst_row):
        return pltpu.make_async_copy(ys_ref.at[pl.ds(src_row, ch), :], buf_ref.at[slot_, pl.ds(dst_row, ch), :],
                                     sem_ref.at[slot_])

    def fetch(tile, slot_):
        for c in range(N_GROUPS):
            src0 = _tab(tab_ref, TAB_DEST, tile, c)
            dst0 = _tab(tab_ref, TAB_LOFF, tile, c)

            def issue(k, carry, src0=src0, dst0=dst0):
                chunk_copy(slot_, pl.multiple_of(src0 + k * ch, ch), pl.multiple_of(dst0 + k * ch, ch)).start()
                return carry
            lax.fori_loop(0, _tab(tab_ref, TAB_NCH, tile, c), issue, 0)

    @pl.when(i == 0)
    def _():
        buf_ref[...] = jnp.zeros(buf_ref.shape, BF16)
        fetch(0, 0)

    @pl.when(i + 1 < nt)
    def _():
        fetch(i + 1, 1 - slot)

    def one(k, carry):
        chunk_copy(slot, 0, 0).wait()
        return carry
    lax.fori_loop(0, _n_chunks(tab_ref, i), one, 0)

    perm_t = _permutation(tab_ref, i, gid_ref[...]).T.astype(BF16)
    y = jnp.dot(perm_t, buf_ref[slot], preferred_element_type=F32)
    o_ref[...] = _layer_norm(alpha * x_ref[...] + y, g_ref[...], b_ref[...])


def _combine(tab, gid, x1, ys, ln_g, ln_b, *, alpha):
    n, d = x1.shape
    t = DISPATCH_TILE
    grid_spec = pltpu.PrefetchScalarGridSpec(
        num_scalar_prefetch=1,
        grid=(n // t,),
        in_specs=[
            pl.BlockSpec((None, 1, t), lambda i, tab_ref: (i, 0, 0)),
            pl.BlockSpec((t, d), lambda i, tab_ref: (i, 0)),
            pl.BlockSpec(memory_space=pl.ANY),
            pl.BlockSpec((1, d), lambda i, tab_ref: (0, 0)),
            pl.BlockSpec((1, d), lambda i, tab_ref: (0, 0)),
        ],
        out_specs=pl.BlockSpec((t, d), lambda i, tab_ref: (i, 0)),
        scratch_shapes=[pltpu.VMEM((2, DISPATCH_ROWS, d), BF16), pltpu.SemaphoreType.DMA((2,))],
    )
    return pl.pallas_call(
        functools.partial(_combine_kernel, alpha=alpha),
        grid_spec=grid_spec,
        out_shape=jax.ShapeDtypeStruct((n, d), F32),
        compiler_params=pltpu.CompilerParams(dimension_semantics=("arbitrary",),
                                             vmem_limit_bytes=V7X_VMEM_LIMIT),
        name="moe_combine",
    )(tab, gid.reshape(n // t, 1, t), x1, ys, ln_g, ln_b)


def _block_diag(blocks):
    n, r, c = blocks.shape
    eye = jnp.eye(n, dtype=blocks.dtype)
    return jnp.einsum("grc,gh->grhc", blocks, eye).reshape(n * r, n * c)


def kernel(x, w_in, pool_w, pool_scale, conv_w, lam_q1, lam_k1, lam_q2, lam_k2, subln_g, sg_ln_g, sg_ln_b, sg_w, sg_b, w_branch, w_o, ln1_g, ln1_b, w_rg, b_rg, w_re, b_re, w_gate, w_up, w_down, ln2_g, ln2_b):
    b, s, d = x.shape
    depth = w_in.shape[0]
    n = b * s
    w = BRANCH_W
    alpha = (2 * depth) ** 0.25
    ta = min(512, s)
    tq = min(512, s)
    tc = min(512, s)
    tm = MOE_TILE

    half = DA_HEAD_DIM // 2
    inv_freq = ROPE_THETA ** (-jnp.arange(half, dtype=F32) / half)
    ang = jnp.arange(s, dtype=F32)[:, None] * inv_freq[None, :]
    cos, sin = jnp.cos(ang), jnp.sin(ang)
    reps = w // DA_HEAD_DIM
    cos_t = jnp.tile(jnp.concatenate([cos, cos], axis=-1), (1, reps))
    sin_t = jnp.tile(jnp.concatenate([-sin, sin], axis=-1), (1, reps))

    for l in range(depth):
        lam_init = 0.8 - 0.6 * math.exp(-0.3 * l)
        w_in_b = w_in[l].astype(BF16)
        w_loc = w_in_b[:, :COL_GATE]
        w_gate_in = w_in_b[:, COL_GATE:]
        pool_bd = _block_diag(pool_w[l]).astype(BF16)
        sg_wcat = jnp.transpose(sg_w[l], (1, 0, 2)).reshape(SG_CHUNK, SG_GROUPS * SG_CHUNK).astype(BF16)
        sg_bias = jnp.repeat(sg_b[l].T, SG_GW, axis=1)
        lam_rows = jnp.zeros((8, 128), F32).at[0:4, 0:DA_HEAD_DIM].set(
            jnp.stack([lam_q1[l], lam_k1[l], lam_q2[l], lam_k2[l]]).astype(F32))
        subln_cols = jnp.broadcast_to(subln_g[l][:, None], (DA_V_DIM, tq))
        w_router = jnp.concatenate([w_rg[l], jnp.transpose(w_re[l], (1, 0, 2)).reshape(d, N_EXPERTS)], axis=1)
        w_router = jnp.pad(w_router, ((0, 0), (0, ROUTER_LANES - w_router.shape[1])))
        wr_hi = w_router.astype(BF16)
        wr_lo = (w_router - wr_hi.astype(F32)).astype(BF16)
        wr2 = jnp.concatenate([jnp.concatenate([wr_hi, wr_lo], axis=1),
                               jnp.concatenate([wr_hi, jnp.zeros_like(wr_lo)], axis=1)], axis=0)
        r_bias = jnp.pad(jnp.concatenate([b_rg[l], b_re[l].reshape(-1)]), (0, ROUTER_LANES - N_GROUPS - N_EXPERTS))[None, :]
        wg = w_gate[l].astype(BF16)
        wu = w_up[l].astype(BF16)
        wd = w_down[l].reshape(N_GROUPS, GROUP_HIDDEN, d).astype(BF16)

        yloc, q, k, vt = _local_mixer(x, w_loc, cos_t, sin_t, pool_bd, pool_scale[l][None, :], conv_w[l],
                                      sg_ln_g[l][None, :], sg_ln_b[l][None, :], sg_wcat, sg_bias, ta=ta)
        yct = _diff_attn(lam_rows, q, k, vt, subln_cols, lam_init=lam_init, tq=tq)
        x1, xa, gid = _merge(x.reshape(n, d), yloc.reshape(n, 3 * w), yct, w_gate_in,
                             w_branch[l].astype(BF16), w_o[l].astype(BF16), ln1_g[l][None, :], ln1_b[l][None, :],
                             wr2, r_bias, alpha=alpha, tc=tc)
        tab, gtab, tgroup, tsrc, cap = _dispatch_tables(gid, n, tm)
        xs = _dispatch(tab, gtab, gid, xa, cap)
        ys = _experts(tgroup, tsrc, xs, wg, wu, wd, tm=tm)
        x2 = _combine(tab, gid, x1, ys, ln2_g[l][None, :], ln2_b[l][None, :], alpha=alpha)
        x = x2.reshape(b, s, d)
    return x
```

```python
import functools
import math

import jax
import jax.numpy as jnp
from jax import lax
from jax.experimental import pallas as pl
from jax.experimental.pallas import tpu as pltpu

F32 = jnp.float32
BF16 = jnp.bfloat16

BRANCH_W = 256
POOL_WINDOWS = (2, 4, 8, 16)
POOL_GW = 64
MAX_POOL = 16
CONV_W = 3
DA_HEADS = 4
DA_HEAD_DIM = 32
DA_V_DIM = 64
ROPE_THETA = 10000.0
SG_CHUNK = 128
SG_GROUPS = 4
SG_GW = 64
N_GROUPS = 4
EXPERTS_PER_GROUP = 4
N_EXPERTS = 16
EXPERT_HIDDEN = 256
GROUP_HIDDEN = EXPERTS_PER_GROUP * EXPERT_HIDDEN
N_BRANCH = 4
LN_EPS = 1e-5
NEG_INF = -1e30
ROUTER_LANES = 128
V7X_VMEM_LIMIT = 56 * 1024 * 1024

COL_POOL = 0
COL_CONV = BRANCH_W
COL_ATTN = 4 * BRANCH_W
COL_SG = 7 * BRANCH_W
COL_GATE = 9 * BRANCH_W


def _layer_norm(h, g, b):
    mu = jnp.mean(h, axis=-1, keepdims=True)
    hc = h - mu
    var = jnp.mean(hc * hc, axis=-1, keepdims=True)
    return hc * lax.rsqrt(var + LN_EPS) * g + b


def _gelu_tanh(x):
    c = math.sqrt(2.0 / math.pi)
    return 0.5 * x * (1.0 + jnp.tanh(c * (x + 0.044715 * (x * x * x))))


def _local_mixer_kernel(x_ref, w_ref, cos_ref, sin_ref, poolw_ref, pscale_ref, convw_ref, lng_ref, lnb_ref,
                        sgw_ref, sgb_ref, yloc_ref, q_ref, k_ref, vt_ref, pext_ref, zext_ref):
    t = pl.program_id(1)
    ta = x_ref.shape[0]
    w = BRANCH_W
    xb = x_ref[...].astype(BF16)
    lane = lax.broadcasted_iota(jnp.int32, (ta, w), 1)
    row = lax.broadcasted_iota(jnp.int32, (ta, w), 0)

    @pl.when(t == 0)
    def _():
        pext_ref[0:MAX_POOL, :] = jnp.zeros((MAX_POOL, w), F32)
        zext_ref[0:8, :] = jnp.zeros((8, w), F32)

    @pl.when(t > 0)
    def _():
        pext_ref[0:MAX_POOL, :] = pext_ref[ta:ta + MAX_POOL, :]
        zext_ref[0:8, :] = zext_ref[ta:ta + 8, :]

    p = jnp.dot(xb, w_ref[:, COL_POOL:COL_POOL + w], preferred_element_type=F32)
    pext_ref[MAX_POOL:MAX_POOL + ta, :] = p

    def prev(kk):
        return pext_ref[pl.ds(MAX_POOL - kk, ta), :]

    s2 = p + prev(1)
    s4 = s2 + (prev(2) + prev(3))
    s8 = s4 + ((prev(4) + prev(5)) + (prev(6) + prev(7)))
    s16 = s8 + (((prev(8) + prev(9)) + (prev(10) + prev(11))) + ((prev(12) + prev(13)) + (prev(14) + prev(15))))
    grp = lane // POOL_GW
    win_sum = jnp.where(grp == 0, s2, jnp.where(grp == 1, s4, jnp.where(grp == 2, s8, s16)))
    win = jnp.where(grp == 0, 2, jnp.where(grp == 1, 4, jnp.where(grp == 2, 8, 16)))
    count = jnp.minimum(t * ta + row + 1, win).astype(F32)
    d = (win_sum / count - p).astype(BF16)
    y_a = jnp.dot(d, poolw_ref[...], preferred_element_type=F32) * pscale_ref[...]
    yloc_ref[:, 0:w] = y_a.astype(BF16)

    pc = jnp.dot(xb, w_ref[:, COL_CONV:COL_CONV + 3 * w], preferred_element_type=F32)
    gb = pc[:, 0:w]
    z = pc[:, w:2 * w] * pc[:, 2 * w:3 * w]
    zext_ref[8:8 + ta, :] = z
    cw = convw_ref[...]
    y_b = zext_ref[pl.ds(6, ta), :] * cw[0:1, :] + zext_ref[pl.ds(7, ta), :] * cw[1:2, :] + z * cw[2:3, :]
    yloc_ref[:, w:2 * w] = (gb * y_b).astype(BF16)

    pa = jnp.dot(xb, w_ref[:, COL_ATTN:COL_ATTN + 3 * w], preferred_element_type=F32)
    cos = cos_ref[...]
    sin = sin_ref[...]
    first_half = (lane % DA_HEAD_DIM) < (DA_HEAD_DIM // 2)

    def rope(u):
        swapped = jnp.where(first_half, pltpu.roll(u, w - DA_HEAD_DIM // 2, axis=1),
                            pltpu.roll(u, DA_HEAD_DIM // 2, axis=1))
        return u * cos + swapped * sin

    q_ref[...] = (rope(pa[:, 0:w]) * (DA_HEAD_DIM ** -0.5 * math.log2(math.e))).astype(BF16)
    k_ref[...] = rope(pa[:, w:2 * w]).astype(BF16)
    vt_ref[...] = pa[:, 2 * w:3 * w].T.astype(BF16)

    uv = _gelu_tanh(jnp.dot(xb, w_ref[:, COL_SG:COL_SG + 2 * w], preferred_element_type=F32))
    u = uv[:, 0:w]
    vn = _layer_norm(uv[:, w:2 * w], lng_ref[...], lnb_ref[...])
    wrow = lax.broadcasted_iota(jnp.int32, (SG_CHUNK, SG_GROUPS * SG_CHUNK), 0)
    wcol = lax.broadcasted_iota(jnp.int32, (SG_CHUNK, SG_GROUPS * SG_CHUNK), 1)
    ws = jnp.where(wrow >= (wcol % SG_CHUNK), sgw_ref[...], jnp.zeros((), BF16))
    cgrp = lax.broadcasted_iota(jnp.int32, (SG_CHUNK, w), 1) // SG_GW
    ys = []
    for c in range(ta // SG_CHUNK):
        vc = vn[c * SG_CHUNK:(c + 1) * SG_CHUNK, :]
        rhs = jnp.concatenate([jnp.where(cgrp == g, vc, 0.0) for g in range(SG_GROUPS)], axis=0).astype(BF16)
        ys.append(jnp.dot(ws, rhs, preferred_element_type=F32) + sgb_ref[...])
    y_d = u * jnp.concatenate(ys, axis=0)
    yloc_ref[:, 2 * w:3 * w] = y_d.astype(BF16)


def _local_mixer(x, w_loc, cos_t, sin_t, pool_bd, pool_scale, conv_w, sg_ln_g, sg_ln_b, sg_wcat, sg_bias, *, ta):
    b, s, d = x.shape
    w = BRANCH_W
    ncol = w_loc.shape[1]
    full = lambda shape: pl.BlockSpec(shape, lambda i, j: (0,) * len(shape))
    seq_blk = lambda width: pl.BlockSpec((None, ta, width), lambda i, j: (i, j, 0))
    return pl.pallas_call(
        _local_mixer_kernel,
        grid=(b, s // ta),
        in_specs=[
            seq_blk(d),
            full((d, ncol)),
            pl.BlockSpec((ta, w), lambda i, j: (j, 0)),
            pl.BlockSpec((ta, w), lambda i, j: (j, 0)),
            full((w, w)), full((1, w)), full((CONV_W, w)), full((1, w)), full((1, w)),
            full((SG_CHUNK, SG_GROUPS * SG_CHUNK)), full((SG_CHUNK, w)),
        ],
        out_specs=[seq_blk(3 * w), seq_blk(w), seq_blk(w), pl.BlockSpec((None, w, ta), lambda i, j: (i, 0, j))],
        out_shape=[
            jax.ShapeDtypeStruct((b, s, 3 * w), BF16),
            jax.ShapeDtypeStruct((b, s, w), BF16),
            jax.ShapeDtypeStruct((b, s, w), BF16),
            jax.ShapeDtypeStruct((b, w, s), BF16),
        ],
        scratch_shapes=[pltpu.VMEM((ta + MAX_POOL, w), F32), pltpu.VMEM((ta + 8, w), F32)],
        compiler_params=pltpu.CompilerParams(dimension_semantics=("parallel", "arbitrary"),
                                             vmem_limit_bytes=V7X_VMEM_LIMIT),
        name="local_mixer",
    )(x, w_loc, cos_t, sin_t, pool_bd, pool_scale, conv_w, sg_ln_g, sg_ln_b, sg_wcat, sg_bias)


ATTN_ONES_ROWS = 16
ATTN_LANE_CHUNK = 256


def _diff_attn_kernel(lam_ref, q_ref, k_ref, vt_ref, g_ref, o_ref, q2_ref, s0_ref, s1_ref, cm0_ref, cm1_ref,
                      m_ref, acc_ref, *, lam_init):
    qi = pl.program_id(1)
    tq = q_ref.shape[0]
    tk = tq
    w = BRANCH_W
    nt_dims = (((1,), (1,)), ((), ()))

    q = q_ref[...]
    sub = lax.broadcasted_iota(jnp.int32, (tq, w), 1) // DA_HEAD_DIM
    zero = jnp.zeros((), BF16)
    for h in range(DA_HEADS):
        q2_ref[h, 0:tq, :] = jnp.where(sub == 2 * h, q, zero)
        q2_ref[h, tq:2 * tq, :] = jnp.where(sub == 2 * h + 1, q, zero)

    lam_rows = lam_ref[...]
    lam = (jnp.exp(jnp.sum(lam_rows[0:1, :] * lam_rows[1:2, :], axis=-1, keepdims=True))
           - jnp.exp(jnp.sum(lam_rows[2:3, :] * lam_rows[3:4, :], axis=-1, keepdims=True)) + lam_init)

    def column_max(sc):
        parts = [sc[r * 8:(r + 1) * 8, :] for r in range(sc.shape[0] // 8)]
        while len(parts) > 1:
            parts = [jnp.maximum(parts[2 * r], parts[2 * r + 1]) for r in range(len(parts) // 2)]
        return jnp.max(parts[0], axis=0, keepdims=True)

    bufs = ((s0_ref, cm0_ref), (s1_ref, cm1_ref))

    def scores(h, kv, dst):
        dst_ref, dst_max_ref = dst
        kt = k_ref[pl.ds(pl.multiple_of(kv * tk, tk), tk), :]
        for c in range(2 * tq // ATTN_LANE_CHUNK):
            cols = slice(c * ATTN_LANE_CHUNK, (c + 1) * ATTN_LANE_CHUNK)
            sc = lax.dot_general(kt, q2_ref[h, cols, :], nt_dims, preferred_element_type=F32)
            dst_ref[:, cols] = sc
            dst_max_ref[:, cols] = column_max(sc)

    def mask_diagonal(cur):
        cur_ref, cur_max_ref = cur
        key = lax.broadcasted_iota(jnp.int32, (tk, ATTN_LANE_CHUNK), 0)
        lane = lax.broadcasted_iota(jnp.int32, (tk, ATTN_LANE_CHUNK), 1)
        for c in range(2 * tq // ATTN_LANE_CHUNK):
            cols = slice(c * ATTN_LANE_CHUNK, (c + 1) * ATTN_LANE_CHUNK)
            qry = lane + (c * ATTN_LANE_CHUNK) % tq
            sc = jnp.where(key <= qry, cur_ref[:, cols], NEG_INF)
            cur_ref[:, cols] = sc
            cur_max_ref[:, cols] = column_max(sc)

    def softmax_pv(h, kv, cur):
        cur_ref, cur_max_ref = cur
        vth = vt_ref[h * DA_V_DIM:(h + 1) * DA_V_DIM, pl.ds(pl.multiple_of(kv * tk, tk), tk)]
        lhs = jnp.concatenate([vth, jnp.ones((ATTN_ONES_ROWS, tk), BF16)], axis=0)
        for c in range(2 * tq // ATTN_LANE_CHUNK):
            cols = slice(c * ATTN_LANE_CHUNK, (c + 1) * ATTN_LANE_CHUNK)
            sc = cur_ref[:, cols]
            m_old = m_ref[h, :, cols]
            m_new = jnp.maximum(m_old, cur_max_ref[:, cols])
            e = jnp.exp2(sc - m_new).astype(BF16)
            pv = jnp.dot(lhs, e, preferred_element_type=F32)
            acc_ref[h, :, cols] = acc_ref[h, :, cols] * jnp.exp2(m_old - m_new) + pv
            m_ref[h, :, cols] = m_new

    def finalize(h):
        o1 = acc_ref[h, 0:DA_V_DIM, 0:tq]
        o2 = acc_ref[h, 0:DA_V_DIM, tq:2 * tq]
        r1 = 1.0 / acc_ref[h, DA_V_DIM:DA_V_DIM + 1, 0:tq]
        r2 = 1.0 / acc_ref[h, DA_V_DIM:DA_V_DIM + 1, tq:2 * tq]
        a = o1 * r1 - lam * (o2 * r2)
        ms = jnp.mean(a * a, axis=0, keepdims=True)
        y = (a * lax.rsqrt(ms + LN_EPS)) * (1.0 - lam_init) * g_ref[...]
        o_ref[h * DA_V_DIM:(h + 1) * DA_V_DIM, :] = y.astype(BF16)

    m_ref[...] = jnp.full(m_ref.shape, NEG_INF, F32)
    acc_ref[...] = jnp.zeros(acc_ref.shape, F32)
    scores(0, 0, bufs[0])

    def full_tile(kv, carry):
        for h in range(DA_HEADS):
            nxt_h, nxt_kv = (h + 1, kv) if h + 1 < DA_HEADS else (0, kv + 1)
            scores(nxt_h, nxt_kv, bufs[(h + 1) % 2])
            softmax_pv(h, kv, bufs[h % 2])
        return carry

    lax.fori_loop(0, qi, full_tile, 0)

    for h in range(DA_HEADS):
        mask_diagonal(bufs[h % 2])
        if h + 1 < DA_HEADS:
            scores(h + 1, qi, bufs[(h + 1) % 2])
        softmax_pv(h, qi, bufs[h % 2])
        finalize(h)


def _diff_attn(lam_rows, q, k, vt, subln_cols, *, lam_init, tq):
    b, s, w = q.shape
    return pl.pallas_call(
        functools.partial(_diff_attn_kernel, lam_init=lam_init),
        grid=(b, s // tq),
        in_specs=[
            pl.BlockSpec((8, 128), lambda i, j: (0, 0)),
            pl.BlockSpec((None, tq, w), lambda i, j: (i, j, 0)),
            pl.BlockSpec((None, s, w), lambda i, j: (i, 0, 0)),
            pl.BlockSpec((None, w, s), lambda i, j: (i, 0, 0)),
            pl.BlockSpec((DA_V_DIM, tq), lambda i, j: (0, 0)),
        ],
        out_specs=pl.BlockSpec((None, w, tq), lambda i, j: (i, 0, j)),
        out_shape=jax.ShapeDtypeStruct((b, w, s), BF16),
        scratch_shapes=[pltpu.VMEM((DA_HEADS, 2 * tq, w), BF16),
                        pltpu.VMEM((tq, 2 * tq), F32), pltpu.VMEM((tq, 2 * tq), F32),
                        pltpu.VMEM((1, 2 * tq), F32), pltpu.VMEM((1, 2 * tq), F32),
                        pltpu.VMEM((DA_HEADS, 1, 2 * tq), F32),
                        pltpu.VMEM((DA_HEADS, DA_V_DIM + ATTN_ONES_ROWS, 2 * tq), F32)],
        compiler_params=pltpu.CompilerParams(dimension_semantics=("parallel", "parallel"),
                                             vmem_limit_bytes=V7X_VMEM_LIMIT),
        name="diff_attn",
    )(lam_rows, q, k, vt, subln_cols)


def _route(x, wr2, bias):
    rows = x.shape[0]
    xh = x.astype(BF16)
    xl = (x - xh.astype(F32)).astype(BF16)
    prod = jnp.dot(jnp.concatenate([xh, xl], axis=1), wr2, preferred_element_type=F32)
    logits = prod[:, 0:ROUTER_LANES] + prod[:, ROUTER_LANES:2 * ROUTER_LANES] + bias
    lt = logits.T

    def first_argmax(vals):
        best = vals[0]
        for v in vals[1:]:
            best = jnp.maximum(best, v)
        idx = jnp.full(best.shape, len(vals) - 1, jnp.int32)
        for i in range(len(vals) - 2, -1, -1):
            idx = jnp.where(vals[i] == best, i, idx)
        return best, idx

    gl = [lt[c:c + 1, :] for c in range(N_GROUPS)]
    gmax, g_sel = first_argmax(gl)
    denom = jnp.exp(gl[0] - gmax)
    for c in range(1, N_GROUPS):
        denom = denom + jnp.exp(gl[c] - gmax)
    p_sel = 1.0 / denom
    el = []
    for e in range(EXPERTS_PER_GROUP):
        row = N_GROUPS + EXPERTS_PER_GROUP * (N_GROUPS - 1) + e
        v = lt[row:row + 1, :]
        for c in range(N_GROUPS - 2, -1, -1):
            row = N_GROUPS + EXPERTS_PER_GROUP * c + e
            v = jnp.where(g_sel == c, lt[row:row + 1, :], v)
        el.append(v)
    v1, i1 = first_argmax(el)
    v2, i2 = first_argmax([jnp.where(i1 == e, NEG_INF, el[e]) for e in range(EXPERTS_PER_GROUP)])
    e2 = jnp.exp(v2 - v1)
    w1 = p_sel / (1.0 + e2)
    w2 = w1 * e2
    gates = [jnp.where(i1 == e, w1, jnp.where(i2 == e, w2, 0.0)) for e in range(EXPERTS_PER_GROUP)]
    hi = [g.astype(BF16).astype(F32) for g in gates]
    lo = [g - h for g, h in zip(gates, hi)]
    pad = jnp.zeros((ROUTER_LANES - 2 * EXPERTS_PER_GROUP, rows), F32)
    aux = jnp.concatenate(hi + lo + [pad], axis=0).T
    return aux, g_sel


def _merge_kernel(x_ref, yloc_ref, yct_ref, wgate_ref, wbr_ref, wo_ref, g_ref, b_ref, wr2_ref, rb_ref,
                  o_ref, xa_ref, gid_ref, h_ref, *, alpha):
    w = BRANCH_W
    tc, d = x_ref.shape

    @pl.when(pl.program_id(0) == 0)
    def _():
        h_ref[...] = jnp.zeros(h_ref.shape, F32)

    x1 = _layer_norm(h_ref[...], g_ref[...], b_ref[...])
    o_ref[...] = x1
    aux, g_row = _route(x1, wr2_ref[...], rb_ref[...])
    xa_ref[:, 0:d] = x1.astype(BF16)
    xa_ref[:, d:d + AUX_LANES] = aux.astype(BF16)
    gid_ref[...] = g_row

    x = x_ref[...]
    xb = x.astype(BF16)
    y_c = yct_ref[...].astype(F32).T.astype(BF16)
    branches = (yloc_ref[:, 0:w], yloc_ref[:, w:2 * w], y_c, yloc_ref[:, 2 * w:3 * w])
    merged = None
    for i in range(N_BRANCH):
        gate = jax.nn.sigmoid(jnp.dot(xb, wgate_ref[:, i * d:(i + 1) * d], preferred_element_type=F32))
        term = gate * jnp.dot(branches[i], wbr_ref[i], preferred_element_type=F32)
        merged = term if merged is None else merged + term
    mix = jnp.dot(merged.astype(BF16), wo_ref[...], preferred_element_type=F32)
    h_ref[...] = alpha * x + mix


def _merge(x, yloc, yct, w_gate_in, w_branch, w_o, ln_g, ln_b, wr2, r_bias, *, alpha, tc):
    n, d = x.shape
    w = BRANCH_W
    s = yct.shape[2]
    tiles_per_row = s // tc
    nt = n // tc
    cur = lambda i: jnp.minimum(i, nt - 1)
    prev = lambda i: jnp.maximum(i - 1, 0)
    tok_in = lambda width: pl.BlockSpec((tc, width), lambda i: (cur(i), 0))
    tok_out = lambda width: pl.BlockSpec((tc, width), lambda i: (prev(i), 0))
    const = lambda shape: pl.BlockSpec(shape, lambda i: (0,) * len(shape))
    return pl.pallas_call(
        functools.partial(_merge_kernel, alpha=alpha),
        grid=(nt + 1,),
        in_specs=[
            tok_in(d), tok_in(3 * w),
            pl.BlockSpec((None, w, tc), lambda i: (cur(i) // tiles_per_row, 0, cur(i) % tiles_per_row)),
            const((d, N_BRANCH * d)), const((N_BRANCH, w, d)), const((d, d)), const((1, d)), const((1, d)),
            const((2 * d, 2 * ROUTER_LANES)), const((1, ROUTER_LANES)),
        ],
        out_specs=[tok_out(d), tok_out(d + AUX_LANES), pl.BlockSpec((None, 1, tc), lambda i: (prev(i), 0, 0))],
        out_shape=[jax.ShapeDtypeStruct((n, d), F32), jax.ShapeDtypeStruct((n, d + AUX_LANES), BF16),
                   jax.ShapeDtypeStruct((n // tc, 1, tc), jnp.int32)],
        scratch_shapes=[pltpu.VMEM((tc, d), F32)],
        compiler_params=pltpu.CompilerParams(dimension_semantics=("arbitrary",),
                                             vmem_limit_bytes=V7X_VMEM_LIMIT),
        name="gated_merge",
    )(x, yloc, yct, w_gate_in, w_branch, w_o, ln_g, ln_b, wr2, r_bias)


DISPATCH_TILE = 256
DISPATCH_CHUNK = 16
DISPATCH_ROWS = 384
AUX_LANES = 128
MOE_TILE = 512
TAB_DEST, TAB_LOFF, TAB_NCH = 0, 1, 2


def _dispatch_tables(gid, n, tm):
    t, ch = DISPATCH_TILE, DISPATCH_CHUNK
    nt = n // t
    g = gid.reshape(nt, t)
    counts = jnp.sum((g[:, :, None] == jnp.arange(N_GROUPS, dtype=jnp.int32)).astype(jnp.int32), axis=1)
    nch = (counts + ch - 1) // ch
    rows = nch * ch
    loff = jnp.cumsum(rows, axis=1) - rows
    coff = jnp.cumsum(rows, axis=0) - rows
    gsize = jnp.sum(rows, axis=0)
    gpad = (gsize + tm - 1) // tm * tm
    gstart = jnp.cumsum(gpad) - gpad
    dest = gstart[None, :] + coff
    tab = jnp.concatenate([dest.reshape(-1), loff.reshape(-1), nch.reshape(-1)]).astype(jnp.int32)
    cap = -(-(n + nt * N_GROUPS * ch + N_GROUPS * tm) // tm) * tm
    used = jnp.sum(gpad)
    gtab = jnp.concatenate([gstart + gsize, (gpad - gsize) // ch, used[None], (cap - used)[None] // tm])
    gtab = gtab.astype(jnp.int32)
    tile = jnp.arange(cap // tm, dtype=jnp.int32)
    tgroup = jnp.minimum(jnp.sum((tile[:, None] * tm >= (gstart + gpad)[None, :]).astype(jnp.int32), axis=1),
                         N_GROUPS - 1)
    tsrc = jnp.minimum(tile, jnp.sum(gpad) // tm - 1)
    return tab, gtab, tgroup, tsrc, cap


def _tab(tab_ref, section, tile, grp):
    n_entries = tab_ref.shape[0] // 3
    return tab_ref[section * n_entries + tile * N_GROUPS + grp]


def _n_chunks(tab_ref, tile):
    total = _tab(tab_ref, TAB_NCH, tile, 0)
    for c in range(1, N_GROUPS):
        total = total + _tab(tab_ref, TAB_NCH, tile, c)
    return total


def _permutation(tab_ref, tile, g_row):
    t = g_row.shape[1]
    src = lax.broadcasted_iota(jnp.int32, (t, t), 0)
    dst = lax.broadcasted_iota(jnp.int32, (t, t), 1)
    before = jnp.where(src < dst, 1.0, 0.0).astype(BF16)
    grp = lax.broadcasted_iota(jnp.int32, (8, t), 0)
    onehot = jnp.where(grp == g_row, 1.0, 0.0)
    earlier = jnp.dot(onehot.astype(BF16), before, preferred_element_type=F32)
    rank = jnp.sum(onehot * earlier, axis=0, keepdims=True).astype(jnp.int32)
    base = jnp.zeros((1, t), jnp.int32)
    for c in range(N_GROUPS):
        base = jnp.where(g_row == c, _tab(tab_ref, TAB_LOFF, tile, c), base)
    rows = lax.broadcasted_iota(jnp.int32, (DISPATCH_ROWS, t), 0)
    return jnp.where(rows == base + rank, 1.0, 0.0)


def _dispatch_kernel(tab_ref, gtab_ref, gid_ref, xa_ref, xs_ref, buf_ref, zero_ref, sem_ref):
    i = pl.program_id(0)
    nt = pl.num_programs(0)
    slot = lax.rem(i, 2)
    ch = DISPATCH_CHUNK

    def chunk_copy(slot_, src_row, dst_row):
        return pltpu.make_async_copy(buf_ref.at[slot_, pl.ds(src_row, ch), :], xs_ref.at[pl.ds(dst_row, ch), :],
                                     sem_ref.at[slot_])

    def wait_tile(tile, slot_):
        def one(k, carry):
            chunk_copy(slot_, 0, 0).wait()
            return carry
        lax.fori_loop(0, _n_chunks(tab_ref, tile), one, 0)

    @pl.when(i >= 2)
    def _():
        wait_tile(i - 2, slot)

    perm = _permutation(tab_ref, i, gid_ref[...]).astype(BF16)
    buf_ref[slot] = jnp.dot(perm, xa_ref[...], preferred_element_type=F32).astype(BF16)
    for c in range(N_GROUPS):
        src0 = _tab(tab_ref, TAB_LOFF, i, c)
        dst0 = _tab(tab_ref, TAB_DEST, i, c)

        def issue(k, carry, src0=src0, dst0=dst0):
            chunk_copy(slot, pl.multiple_of(src0 + k * ch, ch), pl.multiple_of(dst0 + k * ch, ch)).start()
            return carry
        lax.fori_loop(0, _tab(tab_ref, TAB_NCH, i, c), issue, 0)

    @pl.when(i == nt - 1)
    def _():
        zero_ref[...] = jnp.zeros(zero_ref.shape, BF16)

        def tail_copy(dst_row):
            return pltpu.make_async_copy(zero_ref.at[pl.ds(0, ch), :], xs_ref.at[pl.ds(dst_row, ch), :],
                                         sem_ref.at[2])

        tile_rows = zero_ref.shape[0]

        def spare_copy(dst_row):
            return pltpu.make_async_copy(zero_ref, xs_ref.at[pl.ds(dst_row, tile_rows), :], sem_ref.at[3])

        def fill_spare(k, carry):
            spare_copy(pl.multiple_of(gtab_ref[2 * N_GROUPS] + k * tile_rows, tile_rows)).start()
            return carry
        lax.fori_loop(0, gtab_ref[2 * N_GROUPS + 1], fill_spare, 0)

        def drain_spare(k, carry):
            spare_copy(0).wait()
            return carry
        lax.fori_loop(0, gtab_ref[2 * N_GROUPS + 1], drain_spare, 0)

        n_tail = 0
        for c in range(N_GROUPS):
            dst0 = gtab_ref[c]

            def fill(k, carry, dst0=dst0):
                tail_copy(pl.multiple_of(dst0 + k * ch, ch)).start()
                return carry
            lax.fori_loop(0, gtab_ref[N_GROUPS + c], fill, 0)
            n_tail = n_tail + gtab_ref[N_GROUPS + c]

        def drain(k, carry):
            tail_copy(0).wait()
            return carry
        lax.fori_loop(0, n_tail, drain, 0)

        wait_tile(i, slot)

        @pl.when(i >= 1)
        def _():
            wait_tile(i - 1, 1 - slot)


def _dispatch(tab, gtab, gid, xa, cap):
    n, width = xa.shape
    t = DISPATCH_TILE
    grid_spec = pltpu.PrefetchScalarGridSpec(
        num_scalar_prefetch=2,
        grid=(n // t,),
        in_specs=[
            pl.BlockSpec((None, 1, t), lambda i, tab_ref, gtab_ref: (i, 0, 0)),
            pl.BlockSpec((t, width), lambda i, tab_ref, gtab_ref: (i, 0)),
        ],
        out_specs=pl.BlockSpec(memory_space=pl.ANY),
        scratch_shapes=[pltpu.VMEM((2, DISPATCH_ROWS, width), BF16), pltpu.VMEM((MOE_TILE, width), BF16),
                        pltpu.SemaphoreType.DMA((4,))],
    )
    return pl.pallas_call(
        _dispatch_kernel,
        grid_spec=grid_spec,
        out_shape=jax.ShapeDtypeStruct((cap, width), BF16),
        compiler_params=pltpu.CompilerParams(dimension_semantics=("arbitrary",),
                                             vmem_limit_bytes=V7X_VMEM_LIMIT),
        name="moe_dispatch",
    )(tab, gtab, gid.reshape(n // t, 1, t), xa)


def _expert_kernel(tg_ref, ts_ref, xs_ref, wg_ref, wu_ref, wd_ref, ys_ref):
    del tg_ref
    j = pl.program_id(0)
    tm = xs_ref.shape[0]
    d = wg_ref.shape[1]

    @pl.when(ts_ref[j] != j)
    def _():
        ys_ref[...] = jnp.zeros(ys_ref.shape, BF16)

    @pl.when(ts_ref[j] == j)
    def _():
        xt = xs_ref[:, 0:d]
        aux = xs_ref[:, d:d + AUX_LANES].astype(F32)
        lane = lax.broadcasted_iota(jnp.int32, (tm, AUX_LANES), 1)
        acts = []
        for e in range(EXPERTS_PER_GROUP):
            hg = jnp.dot(xt, wg_ref[e], preferred_element_type=F32)
            hu = jnp.dot(xt, wu_ref[e], preferred_element_type=F32)
            two_terms = (lane == e) | (lane == e + EXPERTS_PER_GROUP)
            ge = jnp.sum(jnp.where(two_terms, aux, 0.0), axis=-1, keepdims=True)
            acts.append(((hg * jax.nn.sigmoid(hg)) * hu * ge).astype(BF16))
        act = jnp.concatenate(acts, axis=1)
        ys_ref[...] = jnp.dot(act, wd_ref[...], preferred_element_type=F32).astype(BF16)


def _experts(tgroup, tsrc, xs, wg, wu, wd, *, tm):
    cap, width = xs.shape
    d = wg.shape[1]
    gh = GROUP_HIDDEN
    epg = EXPERTS_PER_GROUP
    grid_spec = pltpu.PrefetchScalarGridSpec(
        num_scalar_prefetch=2,
        grid=(cap // tm,),
        in_specs=[
            pl.BlockSpec((tm, width), lambda j, tg, ts: (ts[j], 0)),
            pl.BlockSpec((epg, d, EXPERT_HIDDEN), lambda j, tg, ts: (tg[j], 0, 0)),
            pl.BlockSpec((epg, d, EXPERT_HIDDEN), lambda j, tg, ts: (tg[j], 0, 0)),
            pl.BlockSpec((None, gh, d), lambda j, tg, ts: (tg[j], 0, 0)),
        ],
        out_specs=pl.BlockSpec((tm, d), lambda j, tg, ts: (j, 0)),
    )
    return pl.pallas_call(
        _expert_kernel,
        grid_spec=grid_spec,
        out_shape=jax.ShapeDtypeStruct((cap, d), BF16),
        compiler_params=pltpu.CompilerParams(dimension_semantics=("arbitrary",),
                                             vmem_limit_bytes=V7X_VMEM_LIMIT),
        name="moe_experts",
    )(tgroup, tsrc, xs, wg, wu, wd)


def _combine_kernel(tab_ref, gid_ref, x_ref, ys_ref, g_ref, b_ref, o_ref, buf_ref, sem_ref, *, alpha):
    i = pl.program_id(0)
    nt = pl.num_programs(0)
    slot = lax.rem(i, 2)
    ch = DISPATCH_CHUNK

    def chunk_copy(slot_, src_row, dst_row):
        return pltpu.make_async_copy(ys_ref.at[pl.ds(src_row, ch), :], buf_ref.at[slot_, pl.ds(dst_row, ch), :],
                                     sem_ref.at[slot_])

    def fetch(tile, slot_):
        for c in range(N_GROUPS):
            src0 = _tab(tab_ref, TAB_DEST, tile, c)
            dst0 = _tab(tab_ref, TAB_LOFF, tile, c)

            def issue(k, carry, src0=src0, dst0=dst0):
                chunk_copy(slot_, pl.multiple_of(src0 + k * ch, ch), pl.multiple_of(dst0 + k * ch, ch)).start()
                return carry
            lax.fori_loop(0, _tab(tab_ref, TAB_NCH, tile, c), issue, 0)

    @pl.when(i == 0)
    def _():
        buf_ref[...] = jnp.zeros(buf_ref.shape, BF16)
        fetch(0, 0)

    @pl.when(i + 1 < nt)
    def _():
        fetch(i + 1, 1 - slot)

    def one(k, carry):
        chunk_copy(slot, 0, 0).wait()
        return carry
    lax.fori_loop(0, _n_chunks(tab_ref, i), one, 0)

    perm_t = _permutation(tab_ref, i, gid_ref[...]).T.astype(BF16)
    y = jnp.dot(perm_t, buf_ref[slot], preferred_element_type=F32)
    o_ref[...] = _layer_norm(alpha * x_ref[...] + y, g_ref[...], b_ref[...])


def _combine(tab, gid, x1, ys, ln_g, ln_b, *, alpha):
    n, d = x1.shape
    t = DISPATCH_TILE
    grid_spec = pltpu.PrefetchScalarGridSpec(
        num_scalar_prefetch=1,
        grid=(n // t,),
        in_specs=[
            pl.BlockSpec((None, 1, t), lambda i, tab_ref: (i, 0, 0)),
            pl.BlockSpec((t, d), lambda i, tab_ref: (i, 0)),
            pl.BlockSpec(memory_space=pl.ANY),
            pl.BlockSpec((1, d), lambda i, tab_ref: (0, 0)),
            pl.BlockSpec((1, d), lambda i, tab_ref: (0, 0)),
        ],
        out_specs=pl.BlockSpec((t, d), lambda i, tab_ref: (i, 0)),
        scratch_shapes=[pltpu.VMEM((2, DISPATCH_ROWS, d), BF16), pltpu.SemaphoreType.DMA((2,))],
    )
    return pl.pallas_call(
        functools.partial(_combine_kernel, alpha=alpha),
        grid_spec=grid_spec,
        out_shape=jax.ShapeDtypeStruct((n, d), F32),
        compiler_params=pltpu.CompilerParams(dimension_semantics=("arbitrary",),
                                             vmem_limit_bytes=V7X_VMEM_LIMIT),
        name="moe_combine",
    )(tab, gid.reshape(n // t, 1, t), x1, ys, ln_g, ln_b)


def _block_diag(blocks):
    n, r, c = blocks.shape
    eye = jnp.eye(n, dtype=blocks.dtype)
    return jnp.einsum("grc,gh->grhc", blocks, eye).reshape(n * r, n * c)


def kernel(x, w_in, pool_w, pool_scale, conv_w, lam_q1, lam_k1, lam_q2, lam_k2, subln_g, sg_ln_g, sg_ln_b, sg_w, sg_b, w_branch, w_o, ln1_g, ln1_b, w_rg, b_rg, w_re, b_re, w_gate, w_up, w_down, ln2_g, ln2_b):
    b, s, d = x.shape
    depth = w_in.shape[0]
    n = b * s
    w = BRANCH_W
    alpha = (2 * depth) ** 0.25
    ta = min(512, s)
    tq = min(512, s)
    tc = min(512, s)
    tm = MOE_TILE

    half = DA_HEAD_DIM // 2
    inv_freq = ROPE_THETA ** (-jnp.arange(half, dtype=F32) / half)
    ang = jnp.arange(s, dtype=F32)[:, None] * inv_freq[None, :]
    cos, sin = jnp.cos(ang), jnp.sin(ang)
    reps = w // DA_HEAD_DIM
    cos_t = jnp.tile(jnp.concatenate([cos, cos], axis=-1), (1, reps))
    sin_t = jnp.tile(jnp.concatenate([-sin, sin], axis=-1), (1, reps))

    for l in range(depth):
        lam_init = 0.8 - 0.6 * math.exp(-0.3 * l)
        w_in_b = w_in[l].astype(BF16)
        w_loc = w_in_b[:, :COL_GATE]
        w_gate_in = w_in_b[:, COL_GATE:]
        pool_bd = _block_diag(pool_w[l]).astype(BF16)
        sg_wcat = jnp.transpose(sg_w[l], (1, 0, 2)).reshape(SG_CHUNK, SG_GROUPS * SG_CHUNK).astype(BF16)
        sg_bias = jnp.repeat(sg_b[l].T, SG_GW, axis=1)
        lam_rows = jnp.zeros((8, 128), F32).at[0:4, 0:DA_HEAD_DIM].set(
            jnp.stack([lam_q1[l], lam_k1[l], lam_q2[l], lam_k2[l]]).astype(F32))
        subln_cols = jnp.broadcast_to(subln_g[l][:, None], (DA_V_DIM, tq))
        w_router = jnp.concatenate([w_rg[l], jnp.transpose(w_re[l], (1, 0, 2)).reshape(d, N_EXPERTS)], axis=1)
        w_router = jnp.pad(w_router, ((0, 0), (0, ROUTER_LANES - w_router.shape[1])))
        wr_hi = w_router.astype(BF16)
        wr_lo = (w_router - wr_hi.astype(F32)).astype(BF16)
        wr2 = jnp.concatenate([jnp.concatenate([wr_hi, wr_lo], axis=1),
                               jnp.concatenate([wr_hi, jnp.zeros_like(wr_lo)], axis=1)], axis=0)
        r_bias = jnp.pad(jnp.concatenate([b_rg[l], b_re[l].reshape(-1)]), (0, ROUTER_LANES - N_GROUPS - N_EXPERTS))[None, :]
        wg = w_gate[l].astype(BF16)
        wu = w_up[l].astype(BF16)
        wd = w_down[l].reshape(N_GROUPS, GROUP_HIDDEN, d).astype(BF16)

        yloc, q, k, vt = _local_mixer(x, w_loc, cos_t, sin_t, pool_bd, pool_scale[l][None, :], conv_w[l],
                                      sg_ln_g[l][None, :], sg_ln_b[l][None, :], sg_wcat, sg_bias, ta=ta)
        yct = _diff_attn(lam_rows, q, k, vt, subln_cols, lam_init=lam_init, tq=tq)
        x1, xa, gid = _merge(x.reshape(n, d), yloc.reshape(n, 3 * w), yct, w_gate_in,
                             w_branch[l].astype(BF16), w_o[l].astype(BF16), ln1_g[l][None, :], ln1_b[l][None, :],
                             wr2, r_bias, alpha=alpha, tc=tc)
        tab, gtab, tgroup, tsrc, cap = _dispatch_tables(gid, n, tm)
        xs = _dispatch(tab, gtab, gid, xa, cap)
        ys = _experts(tgroup, tsrc, xs, wg, wu, wd, tm=tm)
        x2 = _combine(tab, gid, x1, ys, ln2_g[l][None, :], ln2_b[l][None, :], alpha=alpha)
        x = x2.reshape(b, s, d)
    return x
```

```python
import functools
import math

import jax
import jax.numpy as jnp
from jax import lax
from jax.experimental import pallas as pl
from jax.experimental.pallas import tpu as pltpu

F32 = jnp.float32
BF16 = jnp.bfloat16

BRANCH_W = 256
POOL_WINDOWS = (2, 4, 8, 16)
POOL_GW = 64
MAX_POOL = 16
CONV_W = 3
DA_HEADS = 4
DA_HEAD_DIM = 32
DA_V_DIM = 64
ROPE_THETA = 10000.0
SG_CHUNK = 128
SG_GROUPS = 4
SG_GW = 64
N_GROUPS = 4
EXPERTS_PER_GROUP = 4
N_EXPERTS = 16
EXPERT_HIDDEN = 256
GROUP_HIDDEN = EXPERTS_PER_GROUP * EXPERT_HIDDEN
N_BRANCH = 4
LN_EPS = 1e-5
NEG_INF = -1e30
ROUTER_LANES = 128
V7X_VMEM_LIMIT = 56 * 1024 * 1024

COL_POOL = 0
COL_CONV = BRANCH_W
COL_ATTN = 4 * BRANCH_W
COL_SG = 7 * BRANCH_W
COL_GATE = 9 * BRANCH_W


def _layer_norm(h, g, b):
    mu = jnp.mean(h, axis=-1, keepdims=True)
    hc = h - mu
    var = jnp.mean(hc * hc, axis=-1, keepdims=True)
    return hc * lax.rsqrt(var + LN_EPS) * g + b


def _gelu_tanh(x):
    c = math.sqrt(2.0 / math.pi)
    return 0.5 * x * (1.0 + jnp.tanh(c * (x + 0.044715 * (x * x * x))))


def _local_mixer_kernel(x_ref, w_ref, cos_ref, sin_ref, poolw_ref, pscale_ref, convw_ref, lng_ref, lnb_ref,
                        sgw_ref, sgb_ref, yloc_ref, q_ref, k_ref, vt_ref, pext_ref, zext_ref,
                        ppool_ref, pconv_ref, pattn_ref, psg_ref, *, tiles_per_seq):
    step = pl.program_id(0)
    t = lax.rem(jnp.maximum(step - 1, 0), tiles_per_seq)
    ta = x_ref.shape[0]
    w = BRANCH_W
    lane = lax.broadcasted_iota(jnp.int32, (ta, w), 1)
    row = lax.broadcasted_iota(jnp.int32, (ta, w), 0)

    @pl.when(step == 0)
    def _():
        for ref in (ppool_ref, pconv_ref, pattn_ref, psg_ref):
            ref[...] = jnp.zeros(ref.shape, F32)

    @pl.when(t == 0)
    def _():
        pext_ref[0:MAX_POOL, :] = jnp.zeros((MAX_POOL, w), F32)
        zext_ref[0:8, :] = jnp.zeros((8, w), F32)

    @pl.when(t > 0)
    def _():
        pext_ref[0:MAX_POOL, :] = pext_ref[ta:ta + MAX_POOL, :]
        zext_ref[0:8, :] = zext_ref[ta:ta + 8, :]

    p = ppool_ref[...]
    pext_ref[MAX_POOL:MAX_POOL + ta, :] = p

    def prev(kk):
        return pext_ref[pl.ds(MAX_POOL - kk, ta), :]

    s2 = p + prev(1)
    s4 = s2 + (prev(2) + prev(3))
    s8 = s4 + ((prev(4) + prev(5)) + (prev(6) + prev(7)))
    s16 = s8 + (((prev(8) + prev(9)) + (prev(10) + prev(11))) + ((prev(12) + prev(13)) + (prev(14) + prev(15))))
    grp = lane // POOL_GW
    win_sum = jnp.where(grp == 0, s2, jnp.where(grp == 1, s4, jnp.where(grp == 2, s8, s16)))
    win = jnp.where(grp == 0, 2, jnp.where(grp == 1, 4, jnp.where(grp == 2, 8, 16)))
    count = jnp.minimum(t * ta + row + 1, win).astype(F32)
    d = (win_sum / count - p).astype(BF16)
    y_a = jnp.dot(d, poolw_ref[...], preferred_element_type=F32) * pscale_ref[...]
    yloc_ref[:, 0:w] = y_a.astype(BF16)

    pc = pconv_ref[...]
    gb = pc[:, 0:w]
    z = pc[:, w:2 * w] * pc[:, 2 * w:3 * w]
    zext_ref[8:8 + ta, :] = z
    cw = convw_ref[...]
    y_b = zext_ref[pl.ds(6, ta), :] * cw[0:1, :] + zext_ref[pl.ds(7, ta), :] * cw[1:2, :] + z * cw[2:3, :]
    yloc_ref[:, w:2 * w] = (gb * y_b).astype(BF16)

    pa = pattn_ref[...]
    cos = cos_ref[...]
    sin = sin_ref[...]
    first_half = (lane % DA_HEAD_DIM) < (DA_HEAD_DIM // 2)

    def rope(u):
        swapped = jnp.where(first_half, pltpu.roll(u, w - DA_HEAD_DIM // 2, axis=1),
                            pltpu.roll(u, DA_HEAD_DIM // 2, axis=1))
        return u * cos + swapped * sin

    q_ref[...] = (rope(pa[:, 0:w]) * (DA_HEAD_DIM ** -0.5 * math.log2(math.e))).astype(BF16)
    k_ref[...] = rope(pa[:, w:2 * w]).astype(BF16)
    vt_ref[...] = pa[:, 2 * w:3 * w].T.astype(BF16)

    uv = _gelu_tanh(psg_ref[...])
    u = uv[:, 0:w]
    vn = _layer_norm(uv[:, w:2 * w], lng_ref[...], lnb_ref[...])
    wrow = lax.broadcasted_iota(jnp.int32, (SG_CHUNK, SG_GROUPS * SG_CHUNK), 0)
    wcol = lax.broadcasted_iota(jnp.int32, (SG_CHUNK, SG_GROUPS * SG_CHUNK), 1)
    ws = jnp.where(wrow >= (wcol % SG_CHUNK), sgw_ref[...], jnp.zeros((), BF16))
    cgrp = lax.broadcasted_iota(jnp.int32, (SG_CHUNK, w), 1) // SG_GW
    ys = []
    for c in range(ta // SG_CHUNK):
        vc = vn[c * SG_CHUNK:(c + 1) * SG_CHUNK, :]
        rhs = jnp.concatenate([jnp.where(cgrp == g, vc, 0.0) for g in range(SG_GROUPS)], axis=0).astype(BF16)
        ys.append(jnp.dot(ws, rhs, preferred_element_type=F32) + sgb_ref[...])
    y_d = u * jnp.concatenate(ys, axis=0)
    yloc_ref[:, 2 * w:3 * w] = y_d.astype(BF16)

    xb = x_ref[...].astype(BF16)
    for ref, col in ((ppool_ref, COL_POOL), (pconv_ref, COL_CONV), (pattn_ref, COL_ATTN), (psg_ref, COL_SG)):
        ref[...] = jnp.dot(xb, w_ref[:, col:col + ref.shape[1]], preferred_element_type=F32)


def _local_mixer(x, w_loc, cos_t, sin_t, pool_bd, pool_scale, conv_w, sg_ln_g, sg_ln_b, sg_wcat, sg_bias, *, ta):
    b, s, d = x.shape
    w = BRANCH_W
    ncol = w_loc.shape[1]
    tps = s // ta
    nt = b * tps
    cur = lambda i: jnp.minimum(i, nt - 1)
    prev = lambda i: jnp.maximum(i - 1, 0)
    full = lambda shape: pl.BlockSpec(shape, lambda i: (0,) * len(shape))
    out_blk = lambda width: pl.BlockSpec((None, ta, width), lambda i: (prev(i) // tps, prev(i) % tps, 0))
    return pl.pallas_call(
        functools.partial(_local_mixer_kernel, tiles_per_seq=tps),
        grid=(nt + 1,),
        in_specs=[
            pl.BlockSpec((None, ta, d), lambda i: (cur(i) // tps, cur(i) % tps, 0)),
            full((d, ncol)),
            pl.BlockSpec((ta, w), lambda i: (prev(i) % tps, 0)),
            pl.BlockSpec((ta, w), lambda i: (prev(i) % tps, 0)),
            full((w, w)), full((1, w)), full((CONV_W, w)), full((1, w)), full((1, w)),
            full((SG_CHUNK, SG_GROUPS * SG_CHUNK)), full((SG_CHUNK, w)),
        ],
        out_specs=[out_blk(3 * w), out_blk(w), out_blk(w),
                   pl.BlockSpec((None, w, ta), lambda i: (prev(i) // tps, 0, prev(i) % tps))],
        out_shape=[
            jax.ShapeDtypeStruct((b, s, 3 * w), BF16),
            jax.ShapeDtypeStruct((b, s, w), BF16),
            jax.ShapeDtypeStruct((b, s, w), BF16),
            jax.ShapeDtypeStruct((b, w, s), BF16),
        ],
        scratch_shapes=[pltpu.VMEM((ta + MAX_POOL, w), F32), pltpu.VMEM((ta + 8, w), F32),
                        pltpu.VMEM((ta, w), F32), pltpu.VMEM((ta, 3 * w), F32), pltpu.VMEM((ta, 3 * w), F32),
                        pltpu.VMEM((ta, 2 * w), F32)],
        compiler_params=pltpu.CompilerParams(dimension_semantics=("arbitrary",),
                                             vmem_limit_bytes=V7X_VMEM_LIMIT),
        name="local_mixer",
    )(x, w_loc, cos_t, sin_t, pool_bd, pool_scale, conv_w, sg_ln_g, sg_ln_b, sg_wcat, sg_bias)


ATTN_ONES_ROWS = 16
ATTN_LANE_CHUNK = 256


def _diff_attn_kernel(lam_ref, q_ref, k_ref, vt_ref, g_ref, o_ref, q2_ref, s0_ref, s1_ref, cm0_ref, cm1_ref,
                      m_ref, acc_ref, *, lam_init):
    qi = pl.program_id(1)
    tq = q_ref.shape[0]
    tk = tq
    w = BRANCH_W
    nt_dims = (((1,), (1,)), ((), ()))

    q = q_ref[...]
    sub = lax.broadcasted_iota(jnp.int32, (tq, w), 1) // DA_HEAD_DIM
    zero = jnp.zeros((), BF16)
    for h in range(DA_HEADS):
        q2_ref[h, 0:tq, :] = jnp.where(sub == 2 * h, q, zero)
        q2_ref[h, tq:2 * tq, :] = jnp.where(sub == 2 * h + 1, q, zero)

    lam_rows = lam_ref[...]
    lam = (jnp.exp(jnp.sum(lam_rows[0:1, :] * lam_rows[1:2, :], axis=-1, keepdims=True))
           - jnp.exp(jnp.sum(lam_rows[2:3, :] * lam_rows[3:4, :], axis=-1, keepdims=True)) + lam_init)

    def column_max(sc):
        parts = [sc[r * 8:(r + 1) * 8, :] for r in range(sc.shape[0] // 8)]
        while len(parts) > 1:
            parts = [jnp.maximum(parts[2 * r], parts[2 * r + 1]) for r in range(len(parts) // 2)]
        return jnp.max(parts[0], axis=0, keepdims=True)

    bufs = ((s0_ref, cm0_ref), (s1_ref, cm1_ref))

    def causal(c):
        key = lax.broadcasted_iota(jnp.int32, (tk, ATTN_LANE_CHUNK), 0)
        lane = lax.broadcasted_iota(jnp.int32, (tk, ATTN_LANE_CHUNK), 1)
        return key <= lane + (c * ATTN_LANE_CHUNK) % tq

    def scores(h, kv, dst, diagonal=False):
        dst_ref, dst_max_ref = dst
        kt = k_ref[pl.ds(pl.multiple_of(kv * tk, tk), tk), :]
        for c in range(2 * tq // ATTN_LANE_CHUNK):
            cols = slice(c * ATTN_LANE_CHUNK, (c + 1) * ATTN_LANE_CHUNK)
            sc = lax.dot_general(kt, q2_ref[h, cols, :], nt_dims, preferred_element_type=F32)
            if diagonal:
                sc = jnp.where(causal(c), sc, NEG_INF)
            dst_ref[:, cols] = sc
            dst_max_ref[:, cols] = column_max(sc)

    def mask_diagonal(cur):
        cur_ref, cur_max_ref = cur
        for c in range(2 * tq // ATTN_LANE_CHUNK):
            cols = slice(c * ATTN_LANE_CHUNK, (c + 1) * ATTN_LANE_CHUNK)
            sc = jnp.where(causal(c), cur_ref[:, cols], NEG_INF)
            cur_ref[:, cols] = sc
            cur_max_ref[:, cols] = column_max(sc)

    def softmax_pv(h, kv, cur):
        cur_ref, cur_max_ref = cur
        vth = vt_ref[h * DA_V_DIM:(h + 1) * DA_V_DIM, pl.ds(pl.multiple_of(kv * tk, tk), tk)]
        lhs = jnp.concatenate([vth, jnp.ones((ATTN_ONES_ROWS, tk), BF16)], axis=0)
        for c in range(2 * tq // ATTN_LANE_CHUNK):
            cols = slice(c * ATTN_LANE_CHUNK, (c + 1) * ATTN_LANE_CHUNK)
            sc = cur_ref[:, cols]
            m_old = m_ref[h, :, cols]
            m_new = jnp.maximum(m_old, cur_max_ref[:, cols])
            e = jnp.exp2(sc - m_new).astype(BF16)
            pv = jnp.dot(lhs, e, preferred_element_type=F32)
            acc_ref[h, :, cols] = acc_ref[h, :, cols] * jnp.exp2(m_old - m_new) + pv
            m_ref[h, :, cols] = m_new

    def finalize(h):
        o1 = acc_ref[h, 0:DA_V_DIM, 0:tq]
        o2 = acc_ref[h, 0:DA_V_DIM, tq:2 * tq]
        r1 = 1.0 / acc_ref[h, DA_V_DIM:DA_V_DIM + 1, 0:tq]
        r2 = 1.0 / acc_ref[h, DA_V_DIM:DA_V_DIM + 1, tq:2 * tq]
        a = o1 * r1 - lam * (o2 * r2)
        ms = jnp.mean(a * a, axis=0, keepdims=True)
        y = (a * lax.rsqrt(ms + LN_EPS)) * (1.0 - lam_init) * g_ref[...]
        o_ref[h * DA_V_DIM:(h + 1) * DA_V_DIM, :] = y.astype(BF16)

    m_ref[...] = jnp.full(m_ref.shape, NEG_INF, F32)
    acc_ref[...] = jnp.zeros(acc_ref.shape, F32)
    scores(0, 0, bufs[0])

    def full_tile(kv, carry):
        for h in range(DA_HEADS):
            nxt_h, nxt_kv = (h + 1, kv) if h + 1 < DA_HEADS else (0, kv + 1)
            scores(nxt_h, nxt_kv, bufs[(h + 1) % 2])
            softmax_pv(h, kv, bufs[h % 2])
        return carry

    lax.fori_loop(0, qi, full_tile, 0)

    mask_diagonal(bufs[0])
    for h in range(DA_HEADS):
        if h + 1 < DA_HEADS:
            scores(h + 1, qi, bufs[(h + 1) % 2], diagonal=True)
        softmax_pv(h, qi, bufs[h % 2])
        finalize(h)


def _diff_attn(lam_rows, q, k, vt, subln_cols, *, lam_init, tq):
    b, s, w = q.shape
    return pl.pallas_call(
        functools.partial(_diff_attn_kernel, lam_init=lam_init),
        grid=(b, s // tq),
        in_specs=[
            pl.BlockSpec((8, 128), lambda i, j: (0, 0)),
            pl.BlockSpec((None, tq, w), lambda i, j: (i, j, 0)),
            pl.BlockSpec((None, s, w), lambda i, j: (i, 0, 0)),
            pl.BlockSpec((None, w, s), lambda i, j: (i, 0, 0)),
            pl.BlockSpec((DA_V_DIM, tq), lambda i, j: (0, 0)),
        ],
        out_specs=pl.BlockSpec((None, w, tq), lambda i, j: (i, 0, j)),
        out_shape=jax.ShapeDtypeStruct((b, w, s), BF16),
        scratch_shapes=[pltpu.VMEM((DA_HEADS, 2 * tq, w), BF16),
                        pltpu.VMEM((tq, 2 * tq), F32), pltpu.VMEM((tq, 2 * tq), F32),
                        pltpu.VMEM((1, 2 * tq), F32), pltpu.VMEM((1, 2 * tq), F32),
                        pltpu.VMEM((DA_HEADS, 1, 2 * tq), F32),
                        pltpu.VMEM((DA_HEADS, DA_V_DIM + ATTN_ONES_ROWS, 2 * tq), F32)],
        compiler_params=pltpu.CompilerParams(dimension_semantics=("parallel", "parallel"),
                                             vmem_limit_bytes=V7X_VMEM_LIMIT),
        name="diff_attn",
    )(lam_rows, q, k, vt, subln_cols)


def _route(x, wr2, bias):
    rows = x.shape[0]
    xh = x.astype(BF16)
    xl = (x - xh.astype(F32)).astype(BF16)
    prod = jnp.dot(jnp.concatenate([xh, xl], axis=1), wr2, preferred_element_type=F32)
    logits = prod[:, 0:ROUTER_LANES] + prod[:, ROUTER_LANES:2 * ROUTER_LANES] + bias
    lt = logits.T

    def first_argmax(vals):
        best = vals[0]
        for v in vals[1:]:
            best = jnp.maximum(best, v)
        idx = jnp.full(best.shape, len(vals) - 1, jnp.int32)
        for i in range(len(vals) - 2, -1, -1):
            idx = jnp.where(vals[i] == best, i, idx)
        return best, idx

    gl = [lt[c:c + 1, :] for c in range(N_GROUPS)]
    gmax, g_sel = first_argmax(gl)
    denom = jnp.exp(gl[0] - gmax)
    for c in range(1, N_GROUPS):
        denom = denom + jnp.exp(gl[c] - gmax)
    p_sel = 1.0 / denom
    el = []
    for e in range(EXPERTS_PER_GROUP):
        row = N_GROUPS + EXPERTS_PER_GROUP * (N_GROUPS - 1) + e
        v = lt[row:row + 1, :]
        for c in range(N_GROUPS - 2, -1, -1):
            row = N_GROUPS + EXPERTS_PER_GROUP * c + e
            v = jnp.where(g_sel == c, lt[row:row + 1, :], v)
        el.append(v)
    v1, i1 = first_argmax(el)
    v2, i2 = first_argmax([jnp.where(i1 == e, NEG_INF, el[e]) for e in range(EXPERTS_PER_GROUP)])
    e2 = jnp.exp(v2 - v1)
    w1 = p_sel / (1.0 + e2)
    w2 = w1 * e2
    gates = [jnp.where(i1 == e, w1, jnp.where(i2 == e, w2, 0.0)) for e in range(EXPERTS_PER_GROUP)]
    hi = [g.astype(BF16).astype(F32) for g in gates]
    lo = [g - h for g, h in zip(gates, hi)]
    pad = jnp.zeros((ROUTER_LANES - 2 * EXPERTS_PER_GROUP, rows), F32)
    aux = jnp.concatenate(hi + lo + [pad], axis=0).T
    return aux, g_sel


def _merge_kernel(x_ref, yloc_ref, yct_ref, wgate_ref, wbr_ref, wo_ref, g_ref, b_ref, wr2_ref, rb_ref,
                  o_ref, xa_ref, gid_ref, h_ref, *, alpha):
    w = BRANCH_W
    tc, d = x_ref.shape

    @pl.when(pl.program_id(0) == 0)
    def _():
        h_ref[...] = jnp.zeros(h_ref.shape, F32)

    x1 = _layer_norm(h_ref[...], g_ref[...], b_ref[...])
    o_ref[...] = x1
    aux, g_row = _route(x1, wr2_ref[...], rb_ref[...])
    xa_ref[:, 0:d] = x1.astype(BF16)
    xa_ref[:, d:d + AUX_LANES] = aux.astype(BF16)
    gid_ref[...] = g_row

    x = x_ref[...]
    xb = x.astype(BF16)
    y_c = yct_ref[...].astype(F32).T.astype(BF16)
    branches = (yloc_ref[:, 0:w], yloc_ref[:, w:2 * w], y_c, yloc_ref[:, 2 * w:3 * w])
    merged = None
    for i in range(N_BRANCH):
        gate = jax.nn.sigmoid(jnp.dot(xb, wgate_ref[:, i * d:(i + 1) * d], preferred_element_type=F32))
        term = gate * jnp.dot(branches[i], wbr_ref[i], preferred_element_type=F32)
        merged = term if merged is None else merged + term
    mix = jnp.dot(merged.astype(BF16), wo_ref[...], preferred_element_type=F32)
    h_ref[...] = alpha * x + mix


def _merge(x, yloc, yct, w_gate_in, w_branch, w_o, ln_g, ln_b, wr2, r_bias, *, alpha, tc):
    n, d = x.shape
    w = BRANCH_W
    s = yct.shape[2]
    tiles_per_row = s // tc
    nt = n // tc
    cur = lambda i: jnp.minimum(i, nt - 1)
    prev = lambda i: jnp.maximum(i - 1, 0)
    tok_in = lambda width: pl.BlockSpec((tc, width), lambda i: (cur(i), 0))
    tok_out = lambda width: pl.BlockSpec((tc, width), lambda i: (prev(i), 0))
    const = lambda shape: pl.BlockSpec(shape, lambda i: (0,) * len(shape))
    return pl.pallas_call(
        functools.partial(_merge_kernel, alpha=alpha),
        grid=(nt + 1,),
        in_specs=[
            tok_in(d), tok_in(3 * w),
            pl.BlockSpec((None, w, tc), lambda i: (cur(i) // tiles_per_row, 0, cur(i) % tiles_per_row)),
            const((d, N_BRANCH * d)), const((N_BRANCH, w, d)), const((d, d)), const((1, d)), const((1, d)),
            const((2 * d, 2 * ROUTER_LANES)), const((1, ROUTER_LANES)),
        ],
        out_specs=[tok_out(d), tok_out(d + AUX_LANES), pl.BlockSpec((None, 1, tc), lambda i: (prev(i), 0, 0))],
        out_shape=[jax.ShapeDtypeStruct((n, d), F32), jax.ShapeDtypeStruct((n, d + AUX_LANES), BF16),
                   jax.ShapeDtypeStruct((n // tc, 1, tc), jnp.int32)],
        scratch_shapes=[pltpu.VMEM((tc, d), F32)],
        compiler_params=pltpu.CompilerParams(dimension_semantics=("arbitrary",),
                                             vmem_limit_bytes=V7X_VMEM_LIMIT),
        name="gated_merge",
    )(x, yloc, yct, w_gate_in, w_branch, w_o, ln_g, ln_b, wr2, r_bias)


DISPATCH_TILE = 256
DISPATCH_CHUNK = 16
DISPATCH_ROWS = DISPATCH_TILE + N_GROUPS * DISPATCH_CHUNK
COMBINE_ROWS = 384
AUX_LANES = 128
MOE_TILE = 512
TAB_DEST, TAB_LOFF, TAB_NCH = 0, 1, 2


def _dispatch_tables(gid, n, tm):
    t, ch = DISPATCH_TILE, DISPATCH_CHUNK
    nt = n // t
    g = gid.reshape(nt, t)
    counts = jnp.sum((g[:, :, None] == jnp.arange(N_GROUPS, dtype=jnp.int32)).astype(jnp.int32), axis=1)
    nch = (counts + ch - 1) // ch
    rows = nch * ch
    loff = jnp.cumsum(rows, axis=1) - rows
    coff = jnp.cumsum(rows, axis=0) - rows
    gsize = jnp.sum(rows, axis=0)
    gpad = (gsize + tm - 1) // tm * tm
    gstart = jnp.cumsum(gpad) - gpad
    dest = gstart[None, :] + coff
    tab = jnp.concatenate([dest.reshape(-1), loff.reshape(-1), nch.reshape(-1)]).astype(jnp.int32)
    onehot = (g[:, :, None] == jnp.arange(N_GROUPS, dtype=jnp.int32)).astype(jnp.int32)
    rank = jnp.cumsum(onehot, axis=1) - onehot
    pos = jnp.sum(onehot * (rank + loff[:, None, :]), axis=2).astype(jnp.int32).reshape(nt, 1, t)
    cap = -(-(n + nt * N_GROUPS * ch + N_GROUPS * tm) // tm) * tm
    used = jnp.sum(gpad)
    gtab = jnp.concatenate([gstart + gsize, (gpad - gsize) // ch, used[None], (cap - used)[None] // tm])
    gtab = gtab.astype(jnp.int32)
    tile = jnp.arange(cap // tm, dtype=jnp.int32)
    tgroup = jnp.minimum(jnp.sum((tile[:, None] * tm >= (gstart + gpad)[None, :]).astype(jnp.int32), axis=1),
                         N_GROUPS - 1)
    tsrc = jnp.minimum(tile, jnp.sum(gpad) // tm - 1)
    return tab, gtab, pos, tgroup, tsrc, cap


def _tab(tab_ref, section, tile, grp):
    n_entries = tab_ref.shape[0] // 3
    return tab_ref[section * n_entries + tile * N_GROUPS + grp]


def _n_chunks(tab_ref, tile):
    total = _tab(tab_ref, TAB_NCH, tile, 0)
    for c in range(1, N_GROUPS):
        total = total + _tab(tab_ref, TAB_NCH, tile, c)
    return total


def _permutation(pos_row, n_rows):
    rows = lax.broadcasted_iota(jnp.int32, (n_rows, pos_row.shape[1]), 0)
    return jnp.where(rows == pos_row, 1.0, 0.0)


def _dispatch_kernel(tab_ref, gtab_ref, pos_ref, xa_ref, xs_ref, buf_ref, zero_ref, sem_ref):
    i = pl.program_id(0)
    nt = pl.num_programs(0)
    slot = lax.rem(i, 2)
    ch = DISPATCH_CHUNK

    def chunk_copy(slot_, src_row, dst_row):
        return pltpu.make_async_copy(buf_ref.at[slot_, pl.ds(src_row, ch), :], xs_ref.at[pl.ds(dst_row, ch), :],
                                     sem_ref.at[slot_])

    def wait_tile(tile, slot_):
        def one(k, carry):
            chunk_copy(slot_, 0, 0).wait()
            return carry
        lax.fori_loop(0, _n_chunks(tab_ref, tile), one, 0)

    @pl.when(i >= 2)
    def _():
        wait_tile(i - 2, slot)

    perm = _permutation(pos_ref[...], DISPATCH_ROWS).astype(BF16)
    buf_ref[slot] = jnp.dot(perm, xa_ref[...], preferred_element_type=F32).astype(BF16)
    for c in range(N_GROUPS):
        src0 = _tab(tab_ref, TAB_LOFF, i, c)
        dst0 = _tab(tab_ref, TAB_DEST, i, c)

        def issue(k, carry, src0=src0, dst0=dst0):
            chunk_copy(slot, pl.multiple_of(src0 + k * ch, ch), pl.multiple_of(dst0 + k * ch, ch)).start()
            return carry
        lax.fori_loop(0, _tab(tab_ref, TAB_NCH, i, c), issue, 0)

    @pl.when(i == nt - 1)
    def _():
        zero_ref[...] = jnp.zeros(zero_ref.shape, BF16)

        def tail_copy(dst_row):
            return pltpu.make_async_copy(zero_ref.at[pl.ds(0, ch), :], xs_ref.at[pl.ds(dst_row, ch), :],
                                         sem_ref.at[2])

        tile_rows = zero_ref.shape[0]

        def spare_copy(dst_row):
            return pltpu.make_async_copy(zero_ref, xs_ref.at[pl.ds(dst_row, tile_rows), :], sem_ref.at[3])

        def fill_spare(k, carry):
            spare_copy(pl.multiple_of(gtab_ref[2 * N_GROUPS] + k * tile_rows, tile_rows)).start()
            return carry
        lax.fori_loop(0, gtab_ref[2 * N_GROUPS + 1], fill_spare, 0)

        def drain_spare(k, carry):
            spare_copy(0).wait()
            return carry
        lax.fori_loop(0, gtab_ref[2 * N_GROUPS + 1], drain_spare, 0)

        n_tail = 0
        for c in range(N_GROUPS):
            dst0 = gtab_ref[c]

            def fill(k, carry, dst0=dst0):
                tail_copy(pl.multiple_of(dst0 + k * ch, ch)).start()
                return carry
            lax.fori_loop(0, gtab_ref[N_GROUPS + c], fill, 0)
            n_tail = n_tail + gtab_ref[N_GROUPS + c]

        def drain(k, carry):
            tail_copy(0).wait()
            return carry
        lax.fori_loop(0, n_tail, drain, 0)

        wait_tile(i, slot)

        @pl.when(i >= 1)
        def _():
            wait_tile(i - 1, 1 - slot)


def _dispatch(tab, gtab, pos, xa, cap):
    n, width = xa.shape
    t = DISPATCH_TILE
    grid_spec = pltpu.PrefetchScalarGridSpec(
        num_scalar_prefetch=2,
        grid=(n // t,),
        in_specs=[
            pl.BlockSpec((None, 1, t), lambda i, tab_ref, gtab_ref: (i, 0, 0)),
            pl.BlockSpec((t, width), lambda i, tab_ref, gtab_ref: (i, 0)),
        ],
        out_specs=pl.BlockSpec(memory_space=pl.ANY),
        scratch_shapes=[pltpu.VMEM((2, DISPATCH_ROWS, width), BF16), pltpu.VMEM((MOE_TILE, width), BF16),
                        pltpu.SemaphoreType.DMA((4,))],
    )
    return pl.pallas_call(
        _dispatch_kernel,
        grid_spec=grid_spec,
        out_shape=jax.ShapeDtypeStruct((cap, width), BF16),
        compiler_params=pltpu.CompilerParams(dimension_semantics=("arbitrary",),
                                             vmem_limit_bytes=V7X_VMEM_LIMIT),
        name="moe_dispatch",
    )(tab, gtab, pos, xa)


def _expert_kernel(tg_ref, ts_ref, xs_ref, wg_ref, wu_ref, wd_ref, ys_ref):
    del tg_ref
    j = pl.program_id(0)
    tm = xs_ref.shape[0]
    d = wg_ref.shape[1]

    @pl.when(ts_ref[j] != j)
    def _():
        ys_ref[...] = jnp.zeros(ys_ref.shape, BF16)

    @pl.when(ts_ref[j] == j)
    def _():
        xt = xs_ref[:, 0:d]
        aux = xs_ref[:, d:d + AUX_LANES].astype(F32)
        lane = lax.broadcasted_iota(jnp.int32, (tm, AUX_LANES), 1)
        acts = []
        for e in range(EXPERTS_PER_GROUP):
            hg = jnp.dot(xt, wg_ref[e], preferred_element_type=F32)
            hu = jnp.dot(xt, wu_ref[e], preferred_element_type=F32)
            two_terms = (lane == e) | (lane == e + EXPERTS_PER_GROUP)
            ge = jnp.sum(jnp.where(two_terms, aux, 0.0), axis=-1, keepdims=True)
            acts.append(((hg * jax.nn.sigmoid(hg)) * hu * ge).astype(BF16))
        act = jnp.concatenate(acts, axis=1)
        ys_ref[...] = jnp.dot(act, wd_ref[...], preferred_element_type=F32).astype(BF16)


def _experts(tgroup, tsrc, xs, wg, wu, wd, *, tm):
    cap, width = xs.shape
    d = wg.shape[1]
    gh = GROUP_HIDDEN
    epg = EXPERTS_PER_GROUP
    grid_spec = pltpu.PrefetchScalarGridSpec(
        num_scalar_prefetch=2,
        grid=(cap // tm,),
        in_specs=[
            pl.BlockSpec((tm, width), lambda j, tg, ts: (ts[j], 0)),
            pl.BlockSpec((epg, d, EXPERT_HIDDEN), lambda j, tg, ts: (tg[j], 0, 0)),
            pl.BlockSpec((epg, d, EXPERT_HIDDEN), lambda j, tg, ts: (tg[j], 0, 0)),
            pl.BlockSpec((None, gh, d), lambda j, tg, ts: (tg[j], 0, 0)),
        ],
        out_specs=pl.BlockSpec((tm, d), lambda j, tg, ts: (j, 0)),
    )
    return pl.pallas_call(
        _expert_kernel,
        grid_spec=grid_spec,
        out_shape=jax.ShapeDtypeStruct((cap, d), BF16),
        compiler_params=pltpu.CompilerParams(dimension_semantics=("arbitrary",),
                                             vmem_limit_bytes=V7X_VMEM_LIMIT),
        name="moe_experts",
    )(tgroup, tsrc, xs, wg, wu, wd)


def _combine_kernel(tab_ref, pos_ref, x_ref, ys_ref, g_ref, b_ref, o_ref, buf_ref, sem_ref, *, alpha):
    i = pl.program_id(0)
    nt = pl.num_programs(0)
    slot = lax.rem(i, 2)
    ch = DISPATCH_CHUNK

    def chunk_copy(slot_, src_row, dst_row):
        return pltpu.make_async_copy(ys_ref.at[pl.ds(src_row, ch), :], buf_ref.at[slot_, pl.ds(dst_row, ch), :],
                                     sem_ref.at[slot_])

    def fetch(tile, slot_):
        for c in range(N_GROUPS):
            src0 = _tab(tab_ref, TAB_DEST, tile, c)
            dst0 = _tab(tab_ref, TAB_LOFF, tile, c)

            def issue(k, carry, src0=src0, dst0=dst0):
                chunk_copy(slot_, pl.multiple_of(src0 + k * ch, ch), pl.multiple_of(dst0 + k * ch, ch)).start()
                return carry
            lax.fori_loop(0, _tab(tab_ref, TAB_NCH, tile, c), issue, 0)

    @pl.when(i == 0)
    def _():
        buf_ref[...] = jnp.zeros(buf_ref.shape, BF16)
        fetch(0, 0)

    @pl.when(i + 1 < nt)
    def _():
        fetch(i + 1, 1 - slot)

    def one(k, carry):
        chunk_copy(slot, 0, 0).wait()
        return carry
    lax.fori_loop(0, _n_chunks(tab_ref, i), one, 0)

    perm_t = _permutation(pos_ref[...], COMBINE_ROWS).T.astype(BF16)
    y = jnp.dot(perm_t, buf_ref[slot], preferred_element_type=F32)
    o_ref[...] = _layer_norm(alpha * x_ref[...] + y, g_ref[...], b_ref[...])


def _combine(tab, pos, x1, ys, ln_g, ln_b, *, alpha):
    n, d = x1.shape
    t = DISPATCH_TILE
    grid_spec = pltpu.PrefetchScalarGridSpec(
        num_scalar_prefetch=1,
        grid=(n // t,),
        in_specs=[
            pl.BlockSpec((None, 1, t), lambda i, tab_ref: (i, 0, 0)),
            pl.BlockSpec((t, d), lambda i, tab_ref: (i, 0)),
            pl.BlockSpec(memory_space=pl.ANY),
            pl.BlockSpec((1, d), lambda i, tab_ref: (0, 0)),
            pl.BlockSpec((1, d), lambda i, tab_ref: (0, 0)),
        ],
        out_specs=pl.BlockSpec((t, d), lambda i, tab_ref: (i, 0)),
        scratch_shapes=[pltpu.VMEM((2, COMBINE_ROWS, d), BF16), pltpu.SemaphoreType.DMA((2,))],
    )
    return pl.pallas_call(
        functools.partial(_combine_kernel, alpha=alpha),
        grid_spec=grid_spec,
        out_shape=jax.ShapeDtypeStruct((n, d), F32),
        compiler_params=pltpu.CompilerParams(dimension_semantics=("arbitrary",),
                                             vmem_limit_bytes=V7X_VMEM_LIMIT),
        name="moe_combine",
    )(tab, pos, x1, ys, ln_g, ln_b)


def _block_diag(blocks):
    n, r, c = blocks.shape
    eye = jnp.eye(n, dtype=blocks.dtype)
    return jnp.einsum("grc,gh->grhc", blocks, eye).reshape(n * r, n * c)


def kernel(x, w_in, pool_w, pool_scale, conv_w, lam_q1, lam_k1, lam_q2, lam_k2, subln_g, sg_ln_g, sg_ln_b, sg_w, sg_b, w_branch, w_o, ln1_g, ln1_b, w_rg, b_rg, w_re, b_re, w_gate, w_up, w_down, ln2_g, ln2_b):
    b, s, d = x.shape
    depth = w_in.shape[0]
    n = b * s
    w = BRANCH_W
    alpha = (2 * depth) ** 0.25
    ta = min(512, s)
    tq = min(512, s)
    tc = min(512, s)
    tm = MOE_TILE

    half = DA_HEAD_DIM // 2
    inv_freq = ROPE_THETA ** (-jnp.arange(half, dtype=F32) / half)
    ang = jnp.arange(s, dtype=F32)[:, None] * inv_freq[None, :]
    cos, sin = jnp.cos(ang), jnp.sin(ang)
    reps = w // DA_HEAD_DIM
    cos_t = jnp.tile(jnp.concatenate([cos, cos], axis=-1), (1, reps))
    sin_t = jnp.tile(jnp.concatenate([-sin, sin], axis=-1), (1, reps))

    for l in range(depth):
        lam_init = 0.8 - 0.6 * math.exp(-0.3 * l)
        w_loc = w_in[l, :, :COL_GATE].astype(BF16)
        w_gate_in = w_in[l, :, COL_GATE:].astype(BF16)
        pool_bd = _block_diag(pool_w[l]).astype(BF16)
        sg_wcat = jnp.transpose(sg_w[l], (1, 0, 2)).reshape(SG_CHUNK, SG_GROUPS * SG_CHUNK).astype(BF16)
        sg_bias = jnp.repeat(sg_b[l].T, SG_GW, axis=1)
        lam_rows = jnp.zeros((8, 128), F32).at[0:4, 0:DA_HEAD_DIM].set(
            jnp.stack([lam_q1[l], lam_k1[l], lam_q2[l], lam_k2[l]]).astype(F32))
        subln_cols = jnp.broadcast_to(subln_g[l][:, None], (DA_V_DIM, tq))
        w_router = jnp.concatenate([w_rg[l], jnp.transpose(w_re[l], (1, 0, 2)).reshape(d, N_EXPERTS)], axis=1)
        w_router = jnp.pad(w_router, ((0, 0), (0, ROUTER_LANES - w_router.shape[1])))
        wr_hi = w_router.astype(BF16)
        wr_lo = (w_router - wr_hi.astype(F32)).astype(BF16)
        wr2 = jnp.concatenate([jnp.concatenate([wr_hi, wr_lo], axis=1),
                               jnp.concatenate([wr_hi, jnp.zeros_like(wr_lo)], axis=1)], axis=0)
        r_bias = jnp.pad(jnp.concatenate([b_rg[l], b_re[l].reshape(-1)]), (0, ROUTER_LANES - N_GROUPS - N_EXPERTS))[None, :]
        wg = w_gate[l].astype(BF16)
        wu = w_up[l].astype(BF16)
        wd = w_down[l].reshape(N_GROUPS, GROUP_HIDDEN, d).astype(BF16)

        yloc, q, k, vt = _local_mixer(x, w_loc, cos_t, sin_t, pool_bd, pool_scale[l][None, :], conv_w[l],
                                      sg_ln_g[l][None, :], sg_ln_b[l][None, :], sg_wcat, sg_bias, ta=ta)
        yct = _diff_attn(lam_rows, q, k, vt, subln_cols, lam_init=lam_init, tq=tq)
        x1, xa, gid = _merge(x.reshape(n, d), yloc.reshape(n, 3 * w), yct, w_gate_in,
                             w_branch[l].astype(BF16), w_o[l].astype(BF16), ln1_g[l][None, :], ln1_b[l][None, :],
                             wr2, r_bias, alpha=alpha, tc=tc)
        tab, gtab, pos, tgroup, tsrc, cap = _dispatch_tables(gid, n, tm)
        xs = _dispatch(tab, gtab, pos, xa, cap)
        ys = _experts(tgroup, tsrc, xs, wg, wu, wd, tm=tm)
        x2 = _combine(tab, pos, x1, ys, ln2_g[l][None, :], ln2_b[l][None, :], alpha=alpha)
        x = x2.reshape(b, s, d)
    return x
```

```python
import functools
import math

import jax
import jax.numpy as jnp
from jax import lax
from jax.experimental import pallas as pl
from jax.experimental.pallas import tpu as pltpu

F32 = jnp.float32
BF16 = jnp.bfloat16

BRANCH_W = 256
POOL_WINDOWS = (2, 4, 8, 16)
POOL_GW = 64
MAX_POOL = 16
CONV_W = 3
DA_HEADS = 4
DA_HEAD_DIM = 32
DA_V_DIM = 64
ROPE_THETA = 10000.0
SG_CHUNK = 128
SG_GROUPS = 4
SG_GW = 64
N_GROUPS = 4
EXPERTS_PER_GROUP = 4
N_EXPERTS = 16
EXPERT_HIDDEN = 256
GROUP_HIDDEN = EXPERTS_PER_GROUP * EXPERT_HIDDEN
N_BRANCH = 4
LN_EPS = 1e-5
NEG_INF = -1e30
ROUTER_LANES = 128
V7X_VMEM_LIMIT = 56 * 1024 * 1024

COL_POOL = 0
COL_CONV = BRANCH_W
COL_ATTN = 4 * BRANCH_W
COL_SG = 7 * BRANCH_W
COL_GATE = 9 * BRANCH_W


def _layer_norm(h, g, b):
    mu = jnp.mean(h, axis=-1, keepdims=True)
    hc = h - mu
    var = jnp.mean(hc * hc, axis=-1, keepdims=True)
    return hc * lax.rsqrt(var + LN_EPS) * g + b


def _gelu_tanh(x):
    c = math.sqrt(2.0 / math.pi)
    return 0.5 * x * (1.0 + jnp.tanh(c * (x + 0.044715 * (x * x * x))))


def _local_mixer_kernel(x_ref, w_ref, cos_ref, sin_ref, poolw_ref, pscale_ref, convw_ref, lng_ref, lnb_ref,
                        sgw_ref, sgb_ref, yloc_ref, qt_ref, k_ref, vt_ref, pext_ref, zext_ref,
                        ppool_ref, pconv_ref, pattn_ref, psg_ref, *, tiles_per_seq):
    step = pl.program_id(0)
    t = lax.rem(jnp.maximum(step - 1, 0), tiles_per_seq)
    ta = x_ref.shape[0]
    w = BRANCH_W
    lane = lax.broadcasted_iota(jnp.int32, (ta, w), 1)
    row = lax.broadcasted_iota(jnp.int32, (ta, w), 0)

    @pl.when(step == 0)
    def _():
        for ref in (ppool_ref, pconv_ref, pattn_ref, psg_ref):
            ref[...] = jnp.zeros(ref.shape, F32)

    @pl.when(t == 0)
    def _():
        pext_ref[0:MAX_POOL, :] = jnp.zeros((MAX_POOL, w), F32)
        zext_ref[0:8, :] = jnp.zeros((8, w), F32)

    @pl.when(t > 0)
    def _():
        pext_ref[0:MAX_POOL, :] = pext_ref[ta:ta + MAX_POOL, :]
        zext_ref[0:8, :] = zext_ref[ta:ta + 8, :]

    p = ppool_ref[...]
    pext_ref[MAX_POOL:MAX_POOL + ta, :] = p

    def prev(kk):
        return pext_ref[pl.ds(MAX_POOL - kk, ta), :]

    s2 = p + prev(1)
    s4 = s2 + (prev(2) + prev(3))
    s8 = s4 + ((prev(4) + prev(5)) + (prev(6) + prev(7)))
    s16 = s8 + (((prev(8) + prev(9)) + (prev(10) + prev(11))) + ((prev(12) + prev(13)) + (prev(14) + prev(15))))
    grp = lane // POOL_GW
    win_sum = jnp.where(grp == 0, s2, jnp.where(grp == 1, s4, jnp.where(grp == 2, s8, s16)))
    win = jnp.where(grp == 0, 2, jnp.where(grp == 1, 4, jnp.where(grp == 2, 8, 16)))
    count = jnp.minimum(t * ta + row + 1, win).astype(F32)
    d = (win_sum / count - p).astype(BF16)
    y_a = jnp.dot(d, poolw_ref[...], preferred_element_type=F32) * pscale_ref[...]
    yloc_ref[:, 0:w] = y_a.astype(BF16)

    pc = pconv_ref[...]
    gb = pc[:, 0:w]
    z = pc[:, w:2 * w] * pc[:, 2 * w:3 * w]
    zext_ref[8:8 + ta, :] = z
    cw = convw_ref[...]
    y_b = zext_ref[pl.ds(6, ta), :] * cw[0:1, :] + zext_ref[pl.ds(7, ta), :] * cw[1:2, :] + z * cw[2:3, :]
    yloc_ref[:, w:2 * w] = (gb * y_b).astype(BF16)

    pa = pattn_ref[...]
    cos = cos_ref[...]
    sin = sin_ref[...]
    first_half = (lane % DA_HEAD_DIM) < (DA_HEAD_DIM // 2)

    def rope(u):
        swapped = jnp.where(first_half, pltpu.roll(u, w - DA_HEAD_DIM // 2, axis=1),
                            pltpu.roll(u, DA_HEAD_DIM // 2, axis=1))
        return u * cos + swapped * sin

    qt_ref[...] = (rope(pa[:, 0:w]) * (DA_HEAD_DIM ** -0.5 * math.log2(math.e))).T.astype(BF16)
    k_ref[...] = rope(pa[:, w:2 * w]).astype(BF16)
    vt_ref[...] = pa[:, 2 * w:3 * w].T.astype(BF16)

    uv = _gelu_tanh(psg_ref[...])
    u = uv[:, 0:w]
    vn = _layer_norm(uv[:, w:2 * w], lng_ref[...], lnb_ref[...])
    wrow = lax.broadcasted_iota(jnp.int32, (SG_CHUNK, SG_GROUPS * SG_CHUNK), 0)
    wcol = lax.broadcasted_iota(jnp.int32, (SG_CHUNK, SG_GROUPS * SG_CHUNK), 1)
    ws = jnp.where(wrow >= (wcol % SG_CHUNK), sgw_ref[...], jnp.zeros((), BF16))
    cgrp = lax.broadcasted_iota(jnp.int32, (SG_CHUNK, w), 1) // SG_GW
    ys = []
    for c in range(ta // SG_CHUNK):
        vc = vn[c * SG_CHUNK:(c + 1) * SG_CHUNK, :]
        rhs = jnp.concatenate([jnp.where(cgrp == g, vc, 0.0) for g in range(SG_GROUPS)], axis=0).astype(BF16)
        ys.append(jnp.dot(ws, rhs, preferred_element_type=F32) + sgb_ref[...])
    y_d = u * jnp.concatenate(ys, axis=0)
    yloc_ref[:, 2 * w:3 * w] = y_d.astype(BF16)

    xb = x_ref[...].astype(BF16)
    for ref, col in ((ppool_ref, COL_POOL), (pconv_ref, COL_CONV), (pattn_ref, COL_ATTN), (psg_ref, COL_SG)):
        ref[...] = jnp.dot(xb, w_ref[:, col:col + ref.shape[1]], preferred_element_type=F32)


def _local_mixer(x, w_loc, cos_t, sin_t, pool_bd, pool_scale, conv_w, sg_ln_g, sg_ln_b, sg_wcat, sg_bias, *, ta):
    b, s, d = x.shape
    w = BRANCH_W
    ncol = w_loc.shape[1]
    tps = s // ta
    nt = b * tps
    cur = lambda i: jnp.minimum(i, nt - 1)
    prev = lambda i: jnp.maximum(i - 1, 0)
    full = lambda shape: pl.BlockSpec(shape, lambda i: (0,) * len(shape))
    out_blk = lambda width: pl.BlockSpec((None, ta, width), lambda i: (prev(i) // tps, prev(i) % tps, 0))
    out_blk_t = pl.BlockSpec((None, w, ta), lambda i: (prev(i) // tps, 0, prev(i) % tps))
    return pl.pallas_call(
        functools.partial(_local_mixer_kernel, tiles_per_seq=tps),
        grid=(nt + 1,),
        in_specs=[
            pl.BlockSpec((None, ta, d), lambda i: (cur(i) // tps, cur(i) % tps, 0)),
            full((d, ncol)),
            pl.BlockSpec((ta, w), lambda i: (prev(i) % tps, 0)),
            pl.BlockSpec((ta, w), lambda i: (prev(i) % tps, 0)),
            full((w, w)), full((1, w)), full((CONV_W, w)), full((1, w)), full((1, w)),
            full((SG_CHUNK, SG_GROUPS * SG_CHUNK)), full((SG_CHUNK, w)),
        ],
        out_specs=[out_blk(3 * w), out_blk_t, out_blk(w), out_blk_t],
        out_shape=[
            jax.ShapeDtypeStruct((b, s, 3 * w), BF16),
            jax.ShapeDtypeStruct((b, w, s), BF16),
            jax.ShapeDtypeStruct((b, s, w), BF16),
            jax.ShapeDtypeStruct((b, w, s), BF16),
        ],
        scratch_shapes=[pltpu.VMEM((ta + MAX_POOL, w), F32), pltpu.VMEM((ta + 8, w), F32),
                        pltpu.VMEM((ta, w), F32), pltpu.VMEM((ta, 3 * w), F32), pltpu.VMEM((ta, 3 * w), F32),
                        pltpu.VMEM((ta, 2 * w), F32)],
        compiler_params=pltpu.CompilerParams(dimension_semantics=("arbitrary",),
                                             vmem_limit_bytes=V7X_VMEM_LIMIT),
        name="local_mixer",
    )(x, w_loc, cos_t, sin_t, pool_bd, pool_scale, conv_w, sg_ln_g, sg_ln_b, sg_wcat, sg_bias)


ATTN_ONES_ROWS = 16
ATTN_LANE_CHUNK = 256


def _diff_attn_kernel(lam_ref, qt_ref, k_ref, vt_ref, g_ref, o_ref, q2_ref, s0_ref, s1_ref, cm0_ref, cm1_ref,
                      m_ref, acc_ref, *, lam_init):
    qi = pl.program_id(1)
    tq = qt_ref.shape[1]
    tk = tq
    w = BRANCH_W

    qt = qt_ref[...]
    sub = lax.broadcasted_iota(jnp.int32, (w, tq), 0) // DA_HEAD_DIM
    zero = jnp.zeros((), BF16)
    for h in range(DA_HEADS):
        q2_ref[h, :, 0:tq] = jnp.where(sub == 2 * h, qt, zero)
        q2_ref[h, :, tq:2 * tq] = jnp.where(sub == 2 * h + 1, qt, zero)

    lam_rows = lam_ref[...]
    lam = (jnp.exp(jnp.sum(lam_rows[0:1, :] * lam_rows[1:2, :], axis=-1, keepdims=True))
           - jnp.exp(jnp.sum(lam_rows[2:3, :] * lam_rows[3:4, :], axis=-1, keepdims=True)) + lam_init)

    def column_max(sc):
        parts = [sc[r * 8:(r + 1) * 8, :] for r in range(sc.shape[0] // 8)]
        while len(parts) > 1:
            parts = [jnp.maximum(parts[2 * r], parts[2 * r + 1]) for r in range(len(parts) // 2)]
        return jnp.max(parts[0], axis=0, keepdims=True)

    bufs = ((s0_ref, cm0_ref), (s1_ref, cm1_ref))

    def causal(c):
        key = lax.broadcasted_iota(jnp.int32, (tk, ATTN_LANE_CHUNK), 0)
        lane = lax.broadcasted_iota(jnp.int32, (tk, ATTN_LANE_CHUNK), 1)
        return key <= lane + (c * ATTN_LANE_CHUNK) % tq

    def scores(h, kv, dst, diagonal=False):
        dst_ref, dst_max_ref = dst
        kt = k_ref[pl.ds(pl.multiple_of(kv * tk, tk), tk), :]
        for c in range(2 * tq // ATTN_LANE_CHUNK):
            cols = slice(c * ATTN_LANE_CHUNK, (c + 1) * ATTN_LANE_CHUNK)
            sc = jnp.dot(kt, q2_ref[h, :, cols], preferred_element_type=F32)
            if diagonal:
                sc = jnp.where(causal(c), sc, NEG_INF)
            dst_ref[:, cols] = sc
            dst_max_ref[:, cols] = column_max(sc)

    def mask_diagonal(cur):
        cur_ref, cur_max_ref = cur
        for c in range(2 * tq // ATTN_LANE_CHUNK):
            cols = slice(c * ATTN_LANE_CHUNK, (c + 1) * ATTN_LANE_CHUNK)
            sc = jnp.where(causal(c), cur_ref[:, cols], NEG_INF)
            cur_ref[:, cols] = sc
            cur_max_ref[:, cols] = column_max(sc)

    def softmax_pv(h, kv, cur):
        cur_ref, cur_max_ref = cur
        vth = vt_ref[h * DA_V_DIM:(h + 1) * DA_V_DIM, pl.ds(pl.multiple_of(kv * tk, tk), tk)]
        lhs = jnp.concatenate([vth, jnp.ones((ATTN_ONES_ROWS, tk), BF16)], axis=0)
        for c in range(2 * tq // ATTN_LANE_CHUNK):
            cols = slice(c * ATTN_LANE_CHUNK, (c + 1) * ATTN_LANE_CHUNK)
            sc = cur_ref[:, cols]
            m_old = m_ref[h, :, cols]
            m_new = jnp.maximum(m_old, cur_max_ref[:, cols])
            e = jnp.exp2(sc - m_new).astype(BF16)
            pv = jnp.dot(lhs, e, preferred_element_type=F32)
            acc_ref[h, :, cols] = acc_ref[h, :, cols] * jnp.exp2(m_old - m_new) + pv
            m_ref[h, :, cols] = m_new

    def finalize(h):
        o1 = acc_ref[h, 0:DA_V_DIM, 0:tq]
        o2 = acc_ref[h, 0:DA_V_DIM, tq:2 * tq]
        r1 = 1.0 / acc_ref[h, DA_V_DIM:DA_V_DIM + 1, 0:tq]
        r2 = 1.0 / acc_ref[h, DA_V_DIM:DA_V_DIM + 1, tq:2 * tq]
        a = o1 * r1 - lam * (o2 * r2)
        ms = jnp.mean(a * a, axis=0, keepdims=True)
        y = (a * lax.rsqrt(ms + LN_EPS)) * (1.0 - lam_init) * g_ref[...]
        o_ref[h * DA_V_DIM:(h + 1) * DA_V_DIM, :] = y.astype(BF16)

    m_ref[...] = jnp.full(m_ref.shape, NEG_INF, F32)
    acc_ref[...] = jnp.zeros(acc_ref.shape, F32)
    scores(0, 0, bufs[0])

    def full_tile(kv, carry):
        for h in range(DA_HEADS):
            nxt_h, nxt_kv = (h + 1, kv) if h + 1 < DA_HEADS else (0, kv + 1)
            scores(nxt_h, nxt_kv, bufs[(h + 1) % 2])
            softmax_pv(h, kv, bufs[h % 2])
        return carry

    lax.fori_loop(0, qi, full_tile, 0)

    mask_diagonal(bufs[0])
    for h in range(DA_HEADS):
        if h + 1 < DA_HEADS:
            scores(h + 1, qi, bufs[(h + 1) % 2], diagonal=True)
        softmax_pv(h, qi, bufs[h % 2])
        finalize(h)


def _diff_attn(lam_rows, qt, k, vt, subln_cols, *, lam_init, tq):
    b, s, w = k.shape
    return pl.pallas_call(
        functools.partial(_diff_attn_kernel, lam_init=lam_init),
        grid=(b, s // tq),
        in_specs=[
            pl.BlockSpec((8, 128), lambda i, j: (0, 0)),
            pl.BlockSpec((None, w, tq), lambda i, j: (i, 0, j)),
            pl.BlockSpec((None, s, w), lambda i, j: (i, 0, 0)),
            pl.BlockSpec((None, w, s), lambda i, j: (i, 0, 0)),
            pl.BlockSpec((DA_V_DIM, tq), lambda i, j: (0, 0)),
        ],
        out_specs=pl.BlockSpec((None, w, tq), lambda i, j: (i, 0, j)),
        out_shape=jax.ShapeDtypeStruct((b, w, s), BF16),
        scratch_shapes=[pltpu.VMEM((DA_HEADS, w, 2 * tq), BF16),
                        pltpu.VMEM((tq, 2 * tq), F32), pltpu.VMEM((tq, 2 * tq), F32),
                        pltpu.VMEM((1, 2 * tq), F32), pltpu.VMEM((1, 2 * tq), F32),
                        pltpu.VMEM((DA_HEADS, 1, 2 * tq), F32),
                        pltpu.VMEM((DA_HEADS, DA_V_DIM + ATTN_ONES_ROWS, 2 * tq), F32)],
        compiler_params=pltpu.CompilerParams(dimension_semantics=("parallel", "parallel"),
                                             vmem_limit_bytes=V7X_VMEM_LIMIT),
        name="diff_attn",
    )(lam_rows, qt, k, vt, subln_cols)


def _route(x, wr2, bias):
    rows = x.shape[0]
    xh = x.astype(BF16)
    xl = (x - xh.astype(F32)).astype(BF16)
    prod = jnp.dot(jnp.concatenate([xh, xl], axis=1), wr2, preferred_element_type=F32)
    logits = prod[:, 0:ROUTER_LANES] + prod[:, ROUTER_LANES:2 * ROUTER_LANES] + bias
    lt = logits.T

    def first_argmax(vals):
        best = vals[0]
        for v in vals[1:]:
            best = jnp.maximum(best, v)
        idx = jnp.full(best.shape, len(vals) - 1, jnp.int32)
        for i in range(len(vals) - 2, -1, -1):
            idx = jnp.where(vals[i] == best, i, idx)
        return best, idx

    gl = [lt[c:c + 1, :] for c in range(N_GROUPS)]
    gmax, g_sel = first_argmax(gl)
    denom = jnp.exp(gl[0] - gmax)
    for c in range(1, N_GROUPS):
        denom = denom + jnp.exp(gl[c] - gmax)
    p_sel = 1.0 / denom
    el = []
    for e in range(EXPERTS_PER_GROUP):
        row = N_GROUPS + EXPERTS_PER_GROUP * (N_GROUPS - 1) + e
        v = lt[row:row + 1, :]
        for c in range(N_GROUPS - 2, -1, -1):
            row = N_GROUPS + EXPERTS_PER_GROUP * c + e
            v = jnp.where(g_sel == c, lt[row:row + 1, :], v)
        el.append(v)
    v1, i1 = first_argmax(el)
    v2, i2 = first_argmax([jnp.where(i1 == e, NEG_INF, el[e]) for e in range(EXPERTS_PER_GROUP)])
    e2 = jnp.exp(v2 - v1)
    w1 = p_sel / (1.0 + e2)
    w2 = w1 * e2
    gates = [jnp.where(i1 == e, w1, jnp.where(i2 == e, w2, 0.0)) for e in range(EXPERTS_PER_GROUP)]
    hi = [g.astype(BF16).astype(F32) for g in gates]
    lo = [g - h for g, h in zip(gates, hi)]
    pad = jnp.zeros((ROUTER_LANES - 2 * EXPERTS_PER_GROUP, rows), F32)
    aux = jnp.concatenate(hi + lo + [pad], axis=0).T
    return aux, g_sel


def _merge_kernel(x_ref, yloc_ref, yct_ref, wgate_ref, wbr_ref, wo_ref, g_ref, b_ref, wr2_ref, rb_ref,
                  o_ref, xa_ref, gid_ref, h_ref, *, alpha):
    w = BRANCH_W
    tc, d = x_ref.shape

    @pl.when(pl.program_id(0) == 0)
    def _():
        h_ref[...] = jnp.zeros(h_ref.shape, F32)

    x1 = _layer_norm(h_ref[...], g_ref[...], b_ref[...])
    o_ref[...] = x1
    aux, g_row = _route(x1, wr2_ref[...], rb_ref[...])
    xa_ref[:, 0:d] = x1.astype(BF16)
    xa_ref[:, d:d + AUX_LANES] = aux.astype(BF16)
    gid_ref[...] = g_row

    x = x_ref[...]
    xb = x.astype(BF16)
    y_c = yct_ref[...].astype(F32).T.astype(BF16)
    branches = (yloc_ref[:, 0:w], yloc_ref[:, w:2 * w], y_c, yloc_ref[:, 2 * w:3 * w])
    merged = None
    for i in range(N_BRANCH):
        gate = jax.nn.sigmoid(jnp.dot(xb, wgate_ref[:, i * d:(i + 1) * d], preferred_element_type=F32))
        term = gate * jnp.dot(branches[i], wbr_ref[i], preferred_element_type=F32)
        merged = term if merged is None else merged + term
    mix = jnp.dot(merged.astype(BF16), wo_ref[...], preferred_element_type=F32)
    h_ref[...] = alpha * x + mix


def _merge(x, yloc, yct, w_gate_in, w_branch, w_o, ln_g, ln_b, wr2, r_bias, *, alpha, tc):
    n, d = x.shape
    w = BRANCH_W
    s = yct.shape[2]
    tiles_per_row = s // tc
    nt = n // tc
    cur = lambda i: jnp.minimum(i, nt - 1)
    prev = lambda i: jnp.maximum(i - 1, 0)
    tok_in = lambda width: pl.BlockSpec((tc, width), lambda i: (cur(i), 0))
    tok_out = lambda width: pl.BlockSpec((tc, width), lambda i: (prev(i), 0))
    const = lambda shape: pl.BlockSpec(shape, lambda i: (0,) * len(shape))
    return pl.pallas_call(
        functools.partial(_merge_kernel, alpha=alpha),
        grid=(nt + 1,),
        in_specs=[
            tok_in(d), tok_in(3 * w),
            pl.BlockSpec((None, w, tc), lambda i: (cur(i) // tiles_per_row, 0, cur(i) % tiles_per_row)),
            const((d, N_BRANCH * d)), const((N_BRANCH, w, d)), const((d, d)), const((1, d)), const((1, d)),
            const((2 * d, 2 * ROUTER_LANES)), const((1, ROUTER_LANES)),
        ],
        out_specs=[tok_out(d), tok_out(d + AUX_LANES), pl.BlockSpec((None, 1, tc), lambda i: (prev(i), 0, 0))],
        out_shape=[jax.ShapeDtypeStruct((n, d), F32), jax.ShapeDtypeStruct((n, d + AUX_LANES), BF16),
                   jax.ShapeDtypeStruct((n // tc, 1, tc), jnp.int32)],
        scratch_shapes=[pltpu.VMEM((tc, d), F32)],
        compiler_params=pltpu.CompilerParams(dimension_semantics=("arbitrary",),
                                             vmem_limit_bytes=V7X_VMEM_LIMIT),
        name="gated_merge",
    )(x, yloc, yct, w_gate_in, w_branch, w_o, ln_g, ln_b, wr2, r_bias)


DISPATCH_TILE = 256
DISPATCH_CHUNK = 16
DISPATCH_ROWS = DISPATCH_TILE + N_GROUPS * DISPATCH_CHUNK
COMBINE_ROWS = 384
AUX_LANES = 128
MOE_TILE = 512
TAB_DEST, TAB_LOFF, TAB_NCH = 0, 1, 2


def _dispatch_tables(gid, n, tm):
    t, ch = DISPATCH_TILE, DISPATCH_CHUNK
    nt = n // t
    g = gid.reshape(nt, t)
    counts = jnp.sum((g[:, :, None] == jnp.arange(N_GROUPS, dtype=jnp.int32)).astype(jnp.int32), axis=1)
    nch = (counts + ch - 1) // ch
    rows = nch * ch
    loff = jnp.cumsum(rows, axis=1) - rows
    coff = jnp.cumsum(rows, axis=0) - rows
    gsize = jnp.sum(rows, axis=0)
    gpad = (gsize + tm - 1) // tm * tm
    gstart = jnp.cumsum(gpad) - gpad
    dest = gstart[None, :] + coff
    tab = jnp.concatenate([dest.reshape(-1), loff.reshape(-1), nch.reshape(-1)]).astype(jnp.int32)
    onehot = (g[:, :, None] == jnp.arange(N_GROUPS, dtype=jnp.int32)).astype(jnp.int32)
    rank = jnp.cumsum(onehot, axis=1) - onehot
    pos = jnp.sum(onehot * (rank + loff[:, None, :]), axis=2).astype(jnp.int32).reshape(nt, 1, t)
    cap = -(-(n + nt * N_GROUPS * ch + N_GROUPS * tm) // tm) * tm
    used = jnp.sum(gpad)
    gtab = jnp.concatenate([gstart + gsize, (gpad - gsize) // ch, used[None], (cap - used)[None] // tm])
    gtab = gtab.astype(jnp.int32)
    tile = jnp.arange(cap // tm, dtype=jnp.int32)
    tgroup = jnp.minimum(jnp.sum((tile[:, None] * tm >= (gstart + gpad)[None, :]).astype(jnp.int32), axis=1),
                         N_GROUPS - 1)
    tsrc = jnp.minimum(tile, jnp.sum(gpad) // tm - 1)
    return tab, gtab, pos, tgroup, tsrc, cap


def _tab(tab_ref, section, tile, grp):
    n_entries = tab_ref.shape[0] // 3
    return tab_ref[section * n_entries + tile * N_GROUPS + grp]


def _n_chunks(tab_ref, tile):
    total = _tab(tab_ref, TAB_NCH, tile, 0)
    for c in range(1, N_GROUPS):
        total = total + _tab(tab_ref, TAB_NCH, tile, c)
    return total


def _permutation(pos_row, n_rows):
    rows = lax.broadcasted_iota(jnp.int32, (n_rows, pos_row.shape[1]), 0)
    return jnp.where(rows == pos_row, 1.0, 0.0)


def _dispatch_kernel(tab_ref, gtab_ref, pos_ref, xa_ref, xs_ref, buf_ref, zero_ref, sem_ref):
    i = pl.program_id(0)
    nt = pl.num_programs(0)
    slot = lax.rem(i, 2)
    ch = DISPATCH_CHUNK

    def chunk_copy(slot_, src_row, dst_row):
        return pltpu.make_async_copy(buf_ref.at[slot_, pl.ds(src_row, ch), :], xs_ref.at[pl.ds(dst_row, ch), :],
                                     sem_ref.at[slot_])

    def wait_tile(tile, slot_):
        def one(k, carry):
            chunk_copy(slot_, 0, 0).wait()
            return carry
        lax.fori_loop(0, _n_chunks(tab_ref, tile), one, 0)

    @pl.when(i >= 2)
    def _():
        wait_tile(i - 2, slot)

    perm = _permutation(pos_ref[...], DISPATCH_ROWS).astype(BF16)
    buf_ref[slot] = jnp.dot(perm, xa_ref[...], preferred_element_type=F32).astype(BF16)
    for c in range(N_GROUPS):
        src0 = _tab(tab_ref, TAB_LOFF, i, c)
        dst0 = _tab(tab_ref, TAB_DEST, i, c)

        def issue(k, carry, src0=src0, dst0=dst0):
            chunk_copy(slot, pl.multiple_of(src0 + k * ch, ch), pl.multiple_of(dst0 + k * ch, ch)).start()
            return carry
        lax.fori_loop(0, _tab(tab_ref, TAB_NCH, i, c), issue, 0)

    @pl.when(i == nt - 1)
    def _():
        zero_ref[...] = jnp.zeros(zero_ref.shape, BF16)

        def tail_copy(dst_row):
            return pltpu.make_async_copy(zero_ref.at[pl.ds(0, ch), :], xs_ref.at[pl.ds(dst_row, ch), :],
                                         sem_ref.at[2])

        tile_rows = zero_ref.shape[0]

        def spare_copy(dst_row):
            return pltpu.make_async_copy(zero_ref, xs_ref.at[pl.ds(dst_row, tile_rows), :], sem_ref.at[3])

        def fill_spare(k, carry):
            spare_copy(pl.multiple_of(gtab_ref[2 * N_GROUPS] + k * tile_rows, tile_rows)).start()
            return carry
        lax.fori_loop(0, gtab_ref[2 * N_GROUPS + 1], fill_spare, 0)

        def drain_spare(k, carry):
            spare_copy(0).wait()
            return carry
        lax.fori_loop(0, gtab_ref[2 * N_GROUPS + 1], drain_spare, 0)

        n_tail = 0
        for c in range(N_GROUPS):
            dst0 = gtab_ref[c]

            def fill(k, carry, dst0=dst0):
                tail_copy(pl.multiple_of(dst0 + k * ch, ch)).start()
                return carry
            lax.fori_loop(0, gtab_ref[N_GROUPS + c], fill, 0)
            n_tail = n_tail + gtab_ref[N_GROUPS + c]

        def drain(k, carry):
            tail_copy(0).wait()
            return carry
        lax.fori_loop(0, n_tail, drain, 0)

        wait_tile(i, slot)

        @pl.when(i >= 1)
        def _():
            wait_tile(i - 1, 1 - slot)


def _dispatch(tab, gtab, pos, xa, cap):
    n, width = xa.shape
    t = DISPATCH_TILE
    grid_spec = pltpu.PrefetchScalarGridSpec(
        num_scalar_prefetch=2,
        grid=(n // t,),
        in_specs=[
            pl.BlockSpec((None, 1, t), lambda i, tab_ref, gtab_ref: (i, 0, 0)),
            pl.BlockSpec((t, width), lambda i, tab_ref, gtab_ref: (i, 0)),
        ],
        out_specs=pl.BlockSpec(memory_space=pl.ANY),
        scratch_shapes=[pltpu.VMEM((2, DISPATCH_ROWS, width), BF16), pltpu.VMEM((MOE_TILE, width), BF16),
                        pltpu.SemaphoreType.DMA((4,))],
    )
    return pl.pallas_call(
        _dispatch_kernel,
        grid_spec=grid_spec,
        out_shape=jax.ShapeDtypeStruct((cap, width), BF16),
        compiler_params=pltpu.CompilerParams(dimension_semantics=("arbitrary",),
                                             vmem_limit_bytes=V7X_VMEM_LIMIT),
        name="moe_dispatch",
    )(tab, gtab, pos, xa)


def _expert_kernel(tg_ref, ts_ref, xs_ref, wg_ref, wu_ref, wd_ref, ys_ref):
    del tg_ref
    j = pl.program_id(0)
    tm = xs_ref.shape[0]
    d = wg_ref.shape[1]

    @pl.when(ts_ref[j] != j)
    def _():
        ys_ref[...] = jnp.zeros(ys_ref.shape, BF16)

    @pl.when(ts_ref[j] == j)
    def _():
        xt = xs_ref[:, 0:d]
        aux = xs_ref[:, d:d + AUX_LANES].astype(F32)
        lane = lax.broadcasted_iota(jnp.int32, (tm, AUX_LANES), 1)
        acts = []
        for e in range(EXPERTS_PER_GROUP):
            hg = jnp.dot(xt, wg_ref[e], preferred_element_type=F32)
            hu = jnp.dot(xt, wu_ref[e], preferred_element_type=F32)
            two_terms = (lane == e) | (lane == e + EXPERTS_PER_GROUP)
            ge = jnp.sum(jnp.where(two_terms, aux, 0.0), axis=-1, keepdims=True)
            acts.append(((hg * jax.nn.sigmoid(hg)) * hu * ge).astype(BF16))
        act = jnp.concatenate(acts, axis=1)
        ys_ref[...] = jnp.dot(act, wd_ref[...], preferred_element_type=F32).astype(BF16)


def _experts(tgroup, tsrc, xs, wg, wu, wd, *, tm):
    cap, width = xs.shape
    d = wg.shape[1]
    gh = GROUP_HIDDEN
    epg = EXPERTS_PER_GROUP
    grid_spec = pltpu.PrefetchScalarGridSpec(
        num_scalar_prefetch=2,
        grid=(cap // tm,),
        in_specs=[
            pl.BlockSpec((tm, width), lambda j, tg, ts: (ts[j], 0)),
            pl.BlockSpec((epg, d, EXPERT_HIDDEN), lambda j, tg, ts: (tg[j], 0, 0)),
            pl.BlockSpec((epg, d, EXPERT_HIDDEN), lambda j, tg, ts: (tg[j], 0, 0)),
            pl.BlockSpec((None, gh, d), lambda j, tg, ts: (tg[j], 0, 0)),
        ],
        out_specs=pl.BlockSpec((tm, d), lambda j, tg, ts: (j, 0)),
    )
    return pl.pallas_call(
        _expert_kernel,
        grid_spec=grid_spec,
        out_shape=jax.ShapeDtypeStruct((cap, d), BF16),
        compiler_params=pltpu.CompilerParams(dimension_semantics=("arbitrary",),
                                             vmem_limit_bytes=V7X_VMEM_LIMIT),
        name="moe_experts",
    )(tgroup, tsrc, xs, wg, wu, wd)


def _combine_kernel(tab_ref, pos_ref, x_ref, ys_ref, g_ref, b_ref, o_ref, buf_ref, sem_ref, *, alpha):
    i = pl.program_id(0)
    nt = pl.num_programs(0)
    slot = lax.rem(i, 2)
    ch = DISPATCH_CHUNK

    def chunk_copy(slot_, src_row, dst_row):
        return pltpu.make_async_copy(ys_ref.at[pl.ds(src_row, ch), :], buf_ref.at[slot_, pl.ds(dst_row, ch), :],
                                     sem_ref.at[slot_])

    def fetch(tile, slot_):
        for c in range(N_GROUPS):
            src0 = _tab(tab_ref, TAB_DEST, tile, c)
            dst0 = _tab(tab_ref, TAB_LOFF, tile, c)

            def issue(k, carry, src0=src0, dst0=dst0):
                chunk_copy(slot_, pl.multiple_of(src0 + k * ch, ch), pl.multiple_of(dst0 + k * ch, ch)).start()
                return carry
            lax.fori_loop(0, _tab(tab_ref, TAB_NCH, tile, c), issue, 0)

    @pl.when(i == 0)
    def _():
        buf_ref[...] = jnp.zeros(buf_ref.shape, BF16)
        fetch(0, 0)

    @pl.when(i + 1 < nt)
    def _():
        fetch(i + 1, 1 - slot)

    def one(k, carry):
        chunk_copy(slot, 0, 0).wait()
        return carry
    lax.fori_loop(0, _n_chunks(tab_ref, i), one, 0)

    perm_t = _permutation(pos_ref[...], COMBINE_ROWS).T.astype(BF16)
    y = jnp.dot(perm_t, buf_ref[slot], preferred_element_type=F32)
    o_ref[...] = _layer_norm(alpha * x_ref[...] + y, g_ref[...], b_ref[...])


def _combine(tab, pos, x1, ys, ln_g, ln_b, *, alpha):
    n, d = x1.shape
    t = DISPATCH_TILE
    grid_spec = pltpu.PrefetchScalarGridSpec(
        num_scalar_prefetch=1,
        grid=(n // t,),
        in_specs=[
            pl.BlockSpec((None, 1, t), lambda i, tab_ref: (i, 0, 0)),
            pl.BlockSpec((t, d), lambda i, tab_ref: (i, 0)),
            pl.BlockSpec(memory_space=pl.ANY),
            pl.BlockSpec((1, d), lambda i, tab_ref: (0, 0)),
            pl.BlockSpec((1, d), lambda i, tab_ref: (0, 0)),
        ],
        out_specs=pl.BlockSpec((t, d), lambda i, tab_ref: (i, 0)),
        scratch_shapes=[pltpu.VMEM((2, COMBINE_ROWS, d), BF16), pltpu.SemaphoreType.DMA((2,))],
    )
    return pl.pallas_call(
        functools.partial(_combine_kernel, alpha=alpha),
        grid_spec=grid_spec,
        out_shape=jax.ShapeDtypeStruct((n, d), F32),
        compiler_params=pltpu.CompilerParams(dimension_semantics=("arbitrary",),
                                             vmem_limit_bytes=V7X_VMEM_LIMIT),
        name="moe_combine",
    )(tab, pos, x1, ys, ln_g, ln_b)


def _block_diag(blocks):
    n, r, c = blocks.shape
    eye = jnp.eye(n, dtype=blocks.dtype)
    return jnp.einsum("grc,gh->grhc", blocks, eye).reshape(n * r, n * c)


def kernel(x, w_in, pool_w, pool_scale, conv_w, lam_q1, lam_k1, lam_q2, lam_k2, subln_g, sg_ln_g, sg_ln_b, sg_w, sg_b, w_branch, w_o, ln1_g, ln1_b, w_rg, b_rg, w_re, b_re, w_gate, w_up, w_down, ln2_g, ln2_b):
    b, s, d = x.shape
    depth = w_in.shape[0]
    n = b * s
    w = BRANCH_W
    alpha = (2 * depth) ** 0.25
    ta = min(512, s)
    tq = min(512, s)
    tc = min(512, s)
    tm = MOE_TILE

    half = DA_HEAD_DIM // 2
    inv_freq = ROPE_THETA ** (-jnp.arange(half, dtype=F32) / half)
    ang = jnp.arange(s, dtype=F32)[:, None] * inv_freq[None, :]
    cos, sin = jnp.cos(ang), jnp.sin(ang)
    reps = w // DA_HEAD_DIM
    cos_t = jnp.tile(jnp.concatenate([cos, cos], axis=-1), (1, reps))
    sin_t = jnp.tile(jnp.concatenate([-sin, sin], axis=-1), (1, reps))

    for l in range(depth):
        lam_init = 0.8 - 0.6 * math.exp(-0.3 * l)
        w_loc = w_in[l, :, :COL_GATE].astype(BF16)
        w_gate_in = w_in[l, :, COL_GATE:].astype(BF16)
        pool_bd = _block_diag(pool_w[l]).astype(BF16)
        sg_wcat = jnp.transpose(sg_w[l], (1, 0, 2)).reshape(SG_CHUNK, SG_GROUPS * SG_CHUNK).astype(BF16)
        sg_bias = jnp.repeat(sg_b[l].T, SG_GW, axis=1)
        lam_rows = jnp.zeros((8, 128), F32).at[0:4, 0:DA_HEAD_DIM].set(
            jnp.stack([lam_q1[l], lam_k1[l], lam_q2[l], lam_k2[l]]).astype(F32))
        subln_cols = jnp.broadcast_to(subln_g[l][:, None], (DA_V_DIM, tq))
        w_router = jnp.concatenate([w_rg[l], jnp.transpose(w_re[l], (1, 0, 2)).reshape(d, N_EXPERTS)], axis=1)
        w_router = jnp.pad(w_router, ((0, 0), (0, ROUTER_LANES - w_router.shape[1])))
        wr_hi = w_router.astype(BF16)
        wr_lo = (w_router - wr_hi.astype(F32)).astype(BF16)
        wr2 = jnp.concatenate([jnp.concatenate([wr_hi, wr_lo], axis=1),
                               jnp.concatenate([wr_hi, jnp.zeros_like(wr_lo)], axis=1)], axis=0)
        r_bias = jnp.pad(jnp.concatenate([b_rg[l], b_re[l].reshape(-1)]), (0, ROUTER_LANES - N_GROUPS - N_EXPERTS))[None, :]
        wg = w_gate[l].astype(BF16)
        wu = w_up[l].astype(BF16)
        wd = w_down[l].reshape(N_GROUPS, GROUP_HIDDEN, d).astype(BF16)

        yloc, q, k, vt = _local_mixer(x, w_loc, cos_t, sin_t, pool_bd, pool_scale[l][None, :], conv_w[l],
                                      sg_ln_g[l][None, :], sg_ln_b[l][None, :], sg_wcat, sg_bias, ta=ta)
        yct = _diff_attn(lam_rows, q, k, vt, subln_cols, lam_init=lam_init, tq=tq)
        x1, xa, gid = _merge(x.reshape(n, d), yloc.reshape(n, 3 * w), yct, w_gate_in,
                             w_branch[l].astype(BF16), w_o[l].astype(BF16), ln1_g[l][None, :], ln1_b[l][None, :],
                             wr2, r_bias, alpha=alpha, tc=tc)
        tab, gtab, pos, tgroup, tsrc, cap = _dispatch_tables(gid, n, tm)
        xs = _dispatch(tab, gtab, pos, xa, cap)
        ys = _experts(tgroup, tsrc, xs, wg, wu, wd, tm=tm)
        x2 = _combine(tab, pos, x1, ys, ln2_g[l][None, :], ln2_b[l][None, :], alpha=alpha)
        x = x2.reshape(b, s, d)
    return x
```

```python
import functools
import math

import jax
import jax.numpy as jnp
from jax import lax
from jax.experimental import pallas as pl
from jax.experimental.pallas import tpu as pltpu

F32 = jnp.float32
BF16 = jnp.bfloat16

BRANCH_W = 256
POOL_WINDOWS = (2, 4, 8, 16)
POOL_GW = 64
MAX_POOL = 16
CONV_W = 3
DA_HEADS = 4
DA_HEAD_DIM = 32
DA_V_DIM = 64
ROPE_THETA = 10000.0
SG_CHUNK = 128
SG_GROUPS = 4
SG_GW = 64
N_GROUPS = 4
EXPERTS_PER_GROUP = 4
N_EXPERTS = 16
EXPERT_HIDDEN = 256
GROUP_HIDDEN = EXPERTS_PER_GROUP * EXPERT_HIDDEN
N_BRANCH = 4
LN_EPS = 1e-5
NEG_INF = -1e30
ROUTER_LANES = 128
V7X_VMEM_LIMIT = 56 * 1024 * 1024

COL_POOL = 0
COL_CONV = BRANCH_W
COL_ATTN = 4 * BRANCH_W
COL_SG = 7 * BRANCH_W
COL_GATE = 9 * BRANCH_W


def _layer_norm(h, g, b):
    mu = jnp.mean(h, axis=-1, keepdims=True)
    hc = h - mu
    var = jnp.mean(hc * hc, axis=-1, keepdims=True)
    return hc * lax.rsqrt(var + LN_EPS) * g + b


def _gelu_tanh(x):
    c = math.sqrt(2.0 / math.pi)
    return 0.5 * x * (1.0 + jnp.tanh(c * (x + 0.044715 * (x * x * x))))


def _local_mixer_kernel(x_ref, w_ref, cos_ref, sin_ref, poolw_ref, pscale_ref, convw_ref, lng_ref, lnb_ref,
                        sgw_ref, sgb_ref, yloc_ref, qt_ref, k_ref, vt_ref, pext_ref, zext_ref,
                        ppool_ref, pconv_ref, pattn_ref, psg_ref, *, tiles_per_seq):
    step = pl.program_id(0)
    t = lax.rem(jnp.maximum(step - 1, 0), tiles_per_seq)
    ta = x_ref.shape[0]
    w = BRANCH_W
    lane = lax.broadcasted_iota(jnp.int32, (ta, w), 1)
    row = lax.broadcasted_iota(jnp.int32, (ta, w), 0)

    @pl.when(step == 0)
    def _():
        for ref in (ppool_ref, pconv_ref, pattn_ref, psg_ref):
            ref[...] = jnp.zeros(ref.shape, F32)

    @pl.when(t == 0)
    def _():
        pext_ref[0:MAX_POOL, :] = jnp.zeros((MAX_POOL, w), F32)
        zext_ref[0:8, :] = jnp.zeros((8, w), F32)

    @pl.when(t > 0)
    def _():
        pext_ref[0:MAX_POOL, :] = pext_ref[ta:ta + MAX_POOL, :]
        zext_ref[0:8, :] = zext_ref[ta:ta + 8, :]

    p = ppool_ref[...]
    pext_ref[MAX_POOL:MAX_POOL + ta, :] = p

    def prev(kk):
        return pext_ref[pl.ds(MAX_POOL - kk, ta), :]

    s2 = p + prev(1)
    s4 = s2 + (prev(2) + prev(3))
    s8 = s4 + ((prev(4) + prev(5)) + (prev(6) + prev(7)))
    s16 = s8 + (((prev(8) + prev(9)) + (prev(10) + prev(11))) + ((prev(12) + prev(13)) + (prev(14) + prev(15))))
    grp = lane // POOL_GW
    win_sum = jnp.where(grp == 0, s2, jnp.where(grp == 1, s4, jnp.where(grp == 2, s8, s16)))
    win = jnp.where(grp == 0, 2, jnp.where(grp == 1, 4, jnp.where(grp == 2, 8, 16)))
    count = jnp.minimum(t * ta + row + 1, win).astype(F32)
    d = (win_sum / count - p).astype(BF16)
    y_a = jnp.dot(d, poolw_ref[...], preferred_element_type=F32) * pscale_ref[...]
    yloc_ref[:, 0:w] = y_a.astype(BF16)

    pc = pconv_ref[...]
    gb = pc[:, 0:w]
    z = pc[:, w:2 * w] * pc[:, 2 * w:3 * w]
    zext_ref[8:8 + ta, :] = z
    cw = convw_ref[...]
    y_b = zext_ref[pl.ds(6, ta), :] * cw[0:1, :] + zext_ref[pl.ds(7, ta), :] * cw[1:2, :] + z * cw[2:3, :]
    yloc_ref[:, w:2 * w] = (gb * y_b).astype(BF16)

    pa = pattn_ref[...]
    cos = cos_ref[...]
    sin = sin_ref[...]
    first_half = (lane % DA_HEAD_DIM) < (DA_HEAD_DIM // 2)

    def rope(u):
        swapped = jnp.where(first_half, pltpu.roll(u, w - DA_HEAD_DIM // 2, axis=1),
                            pltpu.roll(u, DA_HEAD_DIM // 2, axis=1))
        return u * cos + swapped * sin

    qt_ref[...] = (rope(pa[:, 0:w]) * (DA_HEAD_DIM ** -0.5 * math.log2(math.e))).T.astype(BF16)
    k_ref[...] = rope(pa[:, w:2 * w]).astype(BF16)
    vt_ref[...] = pa[:, 2 * w:3 * w].T.astype(BF16)

    uv = _gelu_tanh(psg_ref[...])
    u = uv[:, 0:w]
    vn = _layer_norm(uv[:, w:2 * w], lng_ref[...], lnb_ref[...])
    wrow = lax.broadcasted_iota(jnp.int32, (SG_CHUNK, SG_GROUPS * SG_CHUNK), 0)
    wcol = lax.broadcasted_iota(jnp.int32, (SG_CHUNK, SG_GROUPS * SG_CHUNK), 1)
    ws = jnp.where(wrow >= (wcol % SG_CHUNK), sgw_ref[...], jnp.zeros((), BF16))
    cgrp = lax.broadcasted_iota(jnp.int32, (SG_CHUNK, w), 1) // SG_GW
    ys = []
    for c in range(ta // SG_CHUNK):
        vc = vn[c * SG_CHUNK:(c + 1) * SG_CHUNK, :]
        rhs = jnp.concatenate([jnp.where(cgrp == g, vc, 0.0) for g in range(SG_GROUPS)], axis=0).astype(BF16)
        ys.append(jnp.dot(ws, rhs, preferred_element_type=F32) + sgb_ref[...])
    y_d = u * jnp.concatenate(ys, axis=0)
    yloc_ref[:, 2 * w:3 * w] = y_d.astype(BF16)

    xb = x_ref[...].astype(BF16)
    for ref, col in ((ppool_ref, COL_POOL), (pconv_ref, COL_CONV), (pattn_ref, COL_ATTN), (psg_ref, COL_SG)):
        ref[...] = jnp.dot(xb, w_ref[:, col:col + ref.shape[1]], preferred_element_type=F32)


def _local_mixer(x, w_loc, cos_t, sin_t, pool_bd, pool_scale, conv_w, sg_ln_g, sg_ln_b, sg_wcat, sg_bias, *, ta):
    b, s, d = x.shape
    w = BRANCH_W
    ncol = w_loc.shape[1]
    tps = s // ta
    nt = b * tps
    cur = lambda i: jnp.minimum(i, nt - 1)
    prev = lambda i: jnp.maximum(i - 1, 0)
    full = lambda shape: pl.BlockSpec(shape, lambda i: (0,) * len(shape))
    out_blk = lambda width: pl.BlockSpec((None, ta, width), lambda i: (prev(i) // tps, prev(i) % tps, 0))
    out_blk_t = pl.BlockSpec((None, w, ta), lambda i: (prev(i) // tps, 0, prev(i) % tps))
    return pl.pallas_call(
        functools.partial(_local_mixer_kernel, tiles_per_seq=tps),
        grid=(nt + 1,),
        in_specs=[
            pl.BlockSpec((None, ta, d), lambda i: (cur(i) // tps, cur(i) % tps, 0)),
            full((d, ncol)),
            pl.BlockSpec((ta, w), lambda i: (prev(i) % tps, 0)),
            pl.BlockSpec((ta, w), lambda i: (prev(i) % tps, 0)),
            full((w, w)), full((1, w)), full((CONV_W, w)), full((1, w)), full((1, w)),
            full((SG_CHUNK, SG_GROUPS * SG_CHUNK)), full((SG_CHUNK, w)),
        ],
        out_specs=[out_blk(3 * w), out_blk_t, out_blk(w), out_blk_t],
        out_shape=[
            jax.ShapeDtypeStruct((b, s, 3 * w), BF16),
            jax.ShapeDtypeStruct((b, w, s), BF16),
            jax.ShapeDtypeStruct((b, s, w), BF16),
            jax.ShapeDtypeStruct((b, w, s), BF16),
        ],
        scratch_shapes=[pltpu.VMEM((ta + MAX_POOL, w), F32), pltpu.VMEM((ta + 8, w), F32),
                        pltpu.VMEM((ta, w), F32), pltpu.VMEM((ta, 3 * w), F32), pltpu.VMEM((ta, 3 * w), F32),
                        pltpu.VMEM((ta, 2 * w), F32)],
        compiler_params=pltpu.CompilerParams(dimension_semantics=("arbitrary",),
                                             vmem_limit_bytes=V7X_VMEM_LIMIT),
        name="local_mixer",
    )(x, w_loc, cos_t, sin_t, pool_bd, pool_scale, conv_w, sg_ln_g, sg_ln_b, sg_wcat, sg_bias)


ATTN_ONES_ROWS = 16
ATTN_LANE_CHUNK = 256


def _diff_attn_kernel(lam_ref, qt_ref, k_ref, vt_ref, g_ref, o_ref, q2_ref, s0_ref, s1_ref, cm0_ref, cm1_ref,
                      m_ref, acc_ref, *, lam_init):
    qi = pl.program_id(1)
    tq = qt_ref.shape[1]
    tk = tq
    w = BRANCH_W

    qt = qt_ref[...]
    sub = lax.broadcasted_iota(jnp.int32, (w, tq), 0) // DA_HEAD_DIM
    zero = jnp.zeros((), BF16)
    for h in range(DA_HEADS):
        q2_ref[h, :, 0:tq] = jnp.where(sub == 2 * h, qt, zero)
        q2_ref[h, :, tq:2 * tq] = jnp.where(sub == 2 * h + 1, qt, zero)

    lam_rows = lam_ref[...]
    lam = (jnp.exp(jnp.sum(lam_rows[0:1, :] * lam_rows[1:2, :], axis=-1, keepdims=True))
           - jnp.exp(jnp.sum(lam_rows[2:3, :] * lam_rows[3:4, :], axis=-1, keepdims=True)) + lam_init)

    def column_max(sc):
        parts = [sc[r * 8:(r + 1) * 8, :] for r in range(sc.shape[0] // 8)]
        while len(parts) > 1:
            parts = [jnp.maximum(parts[2 * r], parts[2 * r + 1]) for r in range(len(parts) // 2)]
        return jnp.max(parts[0], axis=0, keepdims=True)

    bufs = ((s0_ref, cm0_ref), (s1_ref, cm1_ref))

    def causal(c):
        key = lax.broadcasted_iota(jnp.int32, (tk, ATTN_LANE_CHUNK), 0)
        lane = lax.broadcasted_iota(jnp.int32, (tk, ATTN_LANE_CHUNK), 1)
        return key <= lane + (c * ATTN_LANE_CHUNK) % tq

    def scores(h, kv, dst, diagonal=False):
        dst_ref, dst_max_ref = dst
        kt = k_ref[pl.ds(pl.multiple_of(kv * tk, tk), tk), :]
        for c in range(2 * tq // ATTN_LANE_CHUNK):
            cols = slice(c * ATTN_LANE_CHUNK, (c + 1) * ATTN_LANE_CHUNK)
            sc = jnp.dot(kt, q2_ref[h, :, cols], preferred_element_type=F32)
            if diagonal:
                sc = jnp.where(causal(c), sc, NEG_INF)
            dst_ref[:, cols] = sc
            dst_max_ref[:, cols] = column_max(sc)

    def mask_diagonal(cur):
        cur_ref, cur_max_ref = cur
        for c in range(2 * tq // ATTN_LANE_CHUNK):
            cols = slice(c * ATTN_LANE_CHUNK, (c + 1) * ATTN_LANE_CHUNK)
            sc = jnp.where(causal(c), cur_ref[:, cols], NEG_INF)
            cur_ref[:, cols] = sc
            cur_max_ref[:, cols] = column_max(sc)

    def softmax_pv(h, kv, cur):
        cur_ref, cur_max_ref = cur
        vth = vt_ref[h * DA_V_DIM:(h + 1) * DA_V_DIM, pl.ds(pl.multiple_of(kv * tk, tk), tk)]
        lhs = jnp.concatenate([vth, jnp.ones((ATTN_ONES_ROWS, tk), BF16)], axis=0)
        for c in range(2 * tq // ATTN_LANE_CHUNK):
            cols = slice(c * ATTN_LANE_CHUNK, (c + 1) * ATTN_LANE_CHUNK)
            sc = cur_ref[:, cols]
            m_old = m_ref[h, :, cols]
            m_new = jnp.maximum(m_old, cur_max_ref[:, cols])
            e = jnp.exp2(sc - m_new).astype(BF16)
            pv = jnp.dot(lhs, e, preferred_element_type=F32)
            acc_ref[h, :, cols] = acc_ref[h, :, cols] * jnp.exp2(m_old - m_new) + pv
            m_ref[h, :, cols] = m_new

    def finalize(h):
        o1 = acc_ref[h, 0:DA_V_DIM, 0:tq]
        o2 = acc_ref[h, 0:DA_V_DIM, tq:2 * tq]
        r1 = 1.0 / acc_ref[h, DA_V_DIM:DA_V_DIM + 1, 0:tq]
        r2 = 1.0 / acc_ref[h, DA_V_DIM:DA_V_DIM + 1, tq:2 * tq]
        a = o1 * r1 - lam * (o2 * r2)
        ms = jnp.mean(a * a, axis=0, keepdims=True)
        y = (a * lax.rsqrt(ms + LN_EPS)) * (1.0 - lam_init) * g_ref[...]
        o_ref[h * DA_V_DIM:(h + 1) * DA_V_DIM, :] = y.astype(BF16)

    m_ref[...] = jnp.full(m_ref.shape, NEG_INF, F32)
    acc_ref[...] = jnp.zeros(acc_ref.shape, F32)
    scores(0, 0, bufs[0])

    def full_tile(kv, carry):
        for h in range(DA_HEADS):
            nxt_h, nxt_kv = (h + 1, kv) if h + 1 < DA_HEADS else (0, kv + 1)
            scores(nxt_h, nxt_kv, bufs[(h + 1) % 2])
            softmax_pv(h, kv, bufs[h % 2])
        return carry

    lax.fori_loop(0, qi, full_tile, 0)

    mask_diagonal(bufs[0])
    for h in range(DA_HEADS):
        if h + 1 < DA_HEADS:
            scores(h + 1, qi, bufs[(h + 1) % 2], diagonal=True)
        softmax_pv(h, qi, bufs[h % 2])
        finalize(h)


def _diff_attn(lam_rows, qt, k, vt, subln_cols, *, lam_init, tq):
    b, s, w = k.shape
    return pl.pallas_call(
        functools.partial(_diff_attn_kernel, lam_init=lam_init),
        grid=(b, s // tq),
        in_specs=[
            pl.BlockSpec((8, 128), lambda i, j: (0, 0)),
            pl.BlockSpec((None, w, tq), lambda i, j: (i, 0, j)),
            pl.BlockSpec((None, s, w), lambda i, j: (i, 0, 0)),
            pl.BlockSpec((None, w, s), lambda i, j: (i, 0, 0)),
            pl.BlockSpec((DA_V_DIM, tq), lambda i, j: (0, 0)),
        ],
        out_specs=pl.BlockSpec((None, w, tq), lambda i, j: (i, 0, j)),
        out_shape=jax.ShapeDtypeStruct((b, w, s), BF16),
        scratch_shapes=[pltpu.VMEM((DA_HEADS, w, 2 * tq), BF16),
                        pltpu.VMEM((tq, 2 * tq), F32), pltpu.VMEM((tq, 2 * tq), F32),
                        pltpu.VMEM((1, 2 * tq), F32), pltpu.VMEM((1, 2 * tq), F32),
                        pltpu.VMEM((DA_HEADS, 1, 2 * tq), F32),
                        pltpu.VMEM((DA_HEADS, DA_V_DIM + ATTN_ONES_ROWS, 2 * tq), F32)],
        compiler_params=pltpu.CompilerParams(dimension_semantics=("parallel", "parallel"),
                                             vmem_limit_bytes=V7X_VMEM_LIMIT),
        name="diff_attn",
    )(lam_rows, qt, k, vt, subln_cols)


def _route(x, wr2, bias):
    rows = x.shape[0]
    xh = x.astype(BF16)
    xl = (x - xh.astype(F32)).astype(BF16)
    prod = jnp.dot(jnp.concatenate([xh, xl], axis=1), wr2, preferred_element_type=F32)
    logits = prod[:, 0:ROUTER_LANES] + prod[:, ROUTER_LANES:2 * ROUTER_LANES] + bias
    lt = logits.T

    def first_argmax(vals):
        best = vals[0]
        for v in vals[1:]:
            best = jnp.maximum(best, v)
        idx = jnp.full(best.shape, len(vals) - 1, jnp.int32)
        for i in range(len(vals) - 2, -1, -1):
            idx = jnp.where(vals[i] == best, i, idx)
        return best, idx

    gl = [lt[c:c + 1, :] for c in range(N_GROUPS)]
    gmax, g_sel = first_argmax(gl)
    denom = jnp.exp(gl[0] - gmax)
    for c in range(1, N_GROUPS):
        denom = denom + jnp.exp(gl[c] - gmax)
    p_sel = 1.0 / denom
    el = []
    for e in range(EXPERTS_PER_GROUP):
        row = N_GROUPS + EXPERTS_PER_GROUP * (N_GROUPS - 1) + e
        v = lt[row:row + 1, :]
        for c in range(N_GROUPS - 2, -1, -1):
            row = N_GROUPS + EXPERTS_PER_GROUP * c + e
            v = jnp.where(g_sel == c, lt[row:row + 1, :], v)
        el.append(v)
    v1, i1 = first_argmax(el)
    v2, i2 = first_argmax([jnp.where(i1 == e, NEG_INF, el[e]) for e in range(EXPERTS_PER_GROUP)])
    e2 = jnp.exp(v2 - v1)
    w1 = p_sel / (1.0 + e2)
    w2 = w1 * e2
    gates = [jnp.where(i1 == e, w1, jnp.where(i2 == e, w2, 0.0)) for e in range(EXPERTS_PER_GROUP)]
    hi = [g.astype(BF16).astype(F32) for g in gates]
    lo = [g - h for g, h in zip(gates, hi)]
    pad = jnp.zeros((ROUTER_LANES - 2 * EXPERTS_PER_GROUP, rows), F32)
    aux = jnp.concatenate(hi + lo + [pad], axis=0).T
    return aux, g_sel


def _merge_kernel(x_ref, yloc_ref, yct_ref, wgate_ref, wbr_ref, wo_ref, g_ref, b_ref, wr2_ref, rb_ref,
                  o_ref, xa_ref, gid_ref, h_ref, *, alpha):
    w = BRANCH_W
    tc, d = x_ref.shape

    @pl.when(pl.program_id(0) == 0)
    def _():
        h_ref[...] = jnp.zeros(h_ref.shape, F32)

    x1 = _layer_norm(h_ref[...], g_ref[...], b_ref[...])
    o_ref[...] = x1
    aux, g_row = _route(x1, wr2_ref[...], rb_ref[...])
    xa_ref[:, 0:d] = x1.astype(BF16)
    xa_ref[:, d:d + AUX_LANES] = aux.astype(BF16)
    gid_ref[...] = g_row

    x = x_ref[...]
    xb = x.astype(BF16)
    y_c = yct_ref[...].astype(F32).T.astype(BF16)
    branches = (yloc_ref[:, 0:w], yloc_ref[:, w:2 * w], y_c, yloc_ref[:, 2 * w:3 * w])
    merged = None
    for i in range(N_BRANCH):
        gate = jax.nn.sigmoid(jnp.dot(xb, wgate_ref[:, i * d:(i + 1) * d], preferred_element_type=F32))
        term = gate * jnp.dot(branches[i], wbr_ref[i], preferred_element_type=F32)
        merged = term if merged is None else merged + term
    mix = jnp.dot(merged.astype(BF16), wo_ref[...], preferred_element_type=F32)
    h_ref[...] = alpha * x + mix


def _merge(x, yloc, yct, w_gate_in, w_branch, w_o, ln_g, ln_b, wr2, r_bias, *, alpha, tc):
    n, d = x.shape
    w = BRANCH_W
    s = yct.shape[2]
    tiles_per_row = s // tc
    nt = n // tc
    cur = lambda i: jnp.minimum(i, nt - 1)
    prev = lambda i: jnp.maximum(i - 1, 0)
    tok_in = lambda width: pl.BlockSpec((tc, width), lambda i: (cur(i), 0))
    tok_out = lambda width: pl.BlockSpec((tc, width), lambda i: (prev(i), 0))
    const = lambda shape: pl.BlockSpec(shape, lambda i: (0,) * len(shape))
    return pl.pallas_call(
        functools.partial(_merge_kernel, alpha=alpha),
        grid=(nt + 1,),
        in_specs=[
            tok_in(d), tok_in(3 * w),
            pl.BlockSpec((None, w, tc), lambda i: (cur(i) // tiles_per_row, 0, cur(i) % tiles_per_row)),
            const((d, N_BRANCH * d)), const((N_BRANCH, w, d)), const((d, d)), const((1, d)), const((1, d)),
            const((2 * d, 2 * ROUTER_LANES)), const((1, ROUTER_LANES)),
        ],
        out_specs=[tok_out(d), tok_out(d + AUX_LANES), pl.BlockSpec((None, 1, tc), lambda i: (prev(i), 0, 0))],
        out_shape=[jax.ShapeDtypeStruct((n, d), F32), jax.ShapeDtypeStruct((n, d + AUX_LANES), BF16),
                   jax.ShapeDtypeStruct((n // tc, 1, tc), jnp.int32)],
        scratch_shapes=[pltpu.VMEM((tc, d), F32)],
        compiler_params=pltpu.CompilerParams(dimension_semantics=("arbitrary",),
                                             vmem_limit_bytes=V7X_VMEM_LIMIT),
        name="gated_merge",
    )(x, yloc, yct, w_gate_in, w_branch, w_o, ln_g, ln_b, wr2, r_bias)


DISPATCH_TILE = 512
DISPATCH_CHUNK = 16
DISPATCH_ROWS = DISPATCH_TILE + N_GROUPS * DISPATCH_CHUNK
COMBINE_ROWS = -(-DISPATCH_ROWS // 128) * 128
AUX_LANES = 128
MOE_TILE = 512
TAB_DEST, TAB_LOFF, TAB_NCH = 0, 1, 2


def _dispatch_tables(gid, n, tm):
    t, ch = DISPATCH_TILE, DISPATCH_CHUNK
    nt = n // t
    g = gid.reshape(nt, t)
    counts = jnp.sum((g[:, :, None] == jnp.arange(N_GROUPS, dtype=jnp.int32)).astype(jnp.int32), axis=1)
    nch = (counts + ch - 1) // ch
    rows = nch * ch
    loff = jnp.cumsum(rows, axis=1) - rows
    coff = jnp.cumsum(rows, axis=0) - rows
    gsize = jnp.sum(rows, axis=0)
    gpad = (gsize + tm - 1) // tm * tm
    gstart = jnp.cumsum(gpad) - gpad
    dest = gstart[None, :] + coff
    tab = jnp.concatenate([dest.reshape(-1), loff.reshape(-1), nch.reshape(-1)]).astype(jnp.int32)
    onehot = (g[:, :, None] == jnp.arange(N_GROUPS, dtype=jnp.int32)).astype(jnp.int32)
    rank = jnp.cumsum(onehot, axis=1) - onehot
    pos = jnp.sum(onehot * (rank + loff[:, None, :]), axis=2).astype(jnp.int32).reshape(nt, 1, t)
    cap = -(-(n + nt * N_GROUPS * ch + N_GROUPS * tm) // tm) * tm
    used = jnp.sum(gpad)
    gtab = jnp.concatenate([gstart + gsize, (gpad - gsize) // ch, used[None], (cap - used)[None] // tm])
    gtab = gtab.astype(jnp.int32)
    tile = jnp.arange(cap // tm, dtype=jnp.int32)
    tgroup = jnp.minimum(jnp.sum((tile[:, None] * tm >= (gstart + gpad)[None, :]).astype(jnp.int32), axis=1),
                         N_GROUPS - 1)
    tsrc = jnp.minimum(tile, jnp.sum(gpad) // tm - 1)
    return tab, gtab, pos, tgroup, tsrc, cap


def _tab(tab_ref, section, tile, grp):
    n_entries = tab_ref.shape[0] // 3
    return tab_ref[section * n_entries + tile * N_GROUPS + grp]


def _n_chunks(tab_ref, tile):
    total = _tab(tab_ref, TAB_NCH, tile, 0)
    for c in range(1, N_GROUPS):
        total = total + _tab(tab_ref, TAB_NCH, tile, c)
    return total


def _permutation(pos_row, n_rows):
    rows = lax.broadcasted_iota(jnp.int32, (n_rows, pos_row.shape[1]), 0)
    return jnp.where(rows == pos_row, 1.0, 0.0)


def _dispatch_kernel(tab_ref, gtab_ref, pos_ref, xa_ref, xs_ref, buf_ref, zero_ref, sem_ref):
    i = pl.program_id(0)
    nt = pl.num_programs(0)
    slot = lax.rem(i, 2)
    ch = DISPATCH_CHUNK

    def chunk_copy(slot_, src_row, dst_row):
        return pltpu.make_async_copy(buf_ref.at[slot_, pl.ds(src_row, ch), :], xs_ref.at[pl.ds(dst_row, ch), :],
                                     sem_ref.at[slot_])

    def wait_tile(tile, slot_):
        def one(k, carry):
            chunk_copy(slot_, 0, 0).wait()
            return carry
        lax.fori_loop(0, _n_chunks(tab_ref, tile), one, 0)

    @pl.when(i >= 2)
    def _():
        wait_tile(i - 2, slot)

    perm = _permutation(pos_ref[...], DISPATCH_ROWS).astype(BF16)
    buf_ref[slot] = jnp.dot(perm, xa_ref[...], preferred_element_type=F32).astype(BF16)
    for c in range(N_GROUPS):
        src0 = _tab(tab_ref, TAB_LOFF, i, c)
        dst0 = _tab(tab_ref, TAB_DEST, i, c)

        def issue(k, carry, src0=src0, dst0=dst0):
            chunk_copy(slot, pl.multiple_of(src0 + k * ch, ch), pl.multiple_of(dst0 + k * ch, ch)).start()
            return carry
        lax.fori_loop(0, _tab(tab_ref, TAB_NCH, i, c), issue, 0)

    @pl.when(i == nt - 1)
    def _():
        zero_ref[...] = jnp.zeros(zero_ref.shape, BF16)

        def tail_copy(dst_row):
            return pltpu.make_async_copy(zero_ref.at[pl.ds(0, ch), :], xs_ref.at[pl.ds(dst_row, ch), :],
                                         sem_ref.at[2])

        tile_rows = zero_ref.shape[0]

        def spare_copy(dst_row):
            return pltpu.make_async_copy(zero_ref, xs_ref.at[pl.ds(dst_row, tile_rows), :], sem_ref.at[3])

        def fill_spare(k, carry):
            spare_copy(pl.multiple_of(gtab_ref[2 * N_GROUPS] + k * tile_rows, tile_rows)).start()
            return carry
        lax.fori_loop(0, gtab_ref[2 * N_GROUPS + 1], fill_spare, 0)

        def drain_spare(k, carry):
            spare_copy(0).wait()
            return carry
        lax.fori_loop(0, gtab_ref[2 * N_GROUPS + 1], drain_spare, 0)

        n_tail = 0
        for c in range(N_GROUPS):
            dst0 = gtab_ref[c]

            def fill(k, carry, dst0=dst0):
                tail_copy(pl.multiple_of(dst0 + k * ch, ch)).start()
                return carry
            lax.fori_loop(0, gtab_ref[N_GROUPS + c], fill, 0)
            n_tail = n_tail + gtab_ref[N_GROUPS + c]

        def drain(k, carry):
            tail_copy(0).wait()
            return carry
        lax.fori_loop(0, n_tail, drain, 0)

        wait_tile(i, slot)

        @pl.when(i >= 1)
        def _():
            wait_tile(i - 1, 1 - slot)


def _dispatch(tab, gtab, pos, xa, cap):
    n, width = xa.shape
    t = DISPATCH_TILE
    grid_spec = pltpu.PrefetchScalarGridSpec(
        num_scalar_prefetch=2,
        grid=(n // t,),
        in_specs=[
            pl.BlockSpec((None, 1, t), lambda i, tab_ref, gtab_ref: (i, 0, 0)),
            pl.BlockSpec((t, width), lambda i, tab_ref, gtab_ref: (i, 0)),
        ],
        out_specs=pl.BlockSpec(memory_space=pl.ANY),
        scratch_shapes=[pltpu.VMEM((2, DISPATCH_ROWS, width), BF16), pltpu.VMEM((MOE_TILE, width), BF16),
                        pltpu.SemaphoreType.DMA((4,))],
    )
    return pl.pallas_call(
        _dispatch_kernel,
        grid_spec=grid_spec,
        out_shape=jax.ShapeDtypeStruct((cap, width), BF16),
        compiler_params=pltpu.CompilerParams(dimension_semantics=("arbitrary",),
                                             vmem_limit_bytes=V7X_VMEM_LIMIT),
        name="moe_dispatch",
    )(tab, gtab, pos, xa)


def _expert_kernel(tg_ref, ts_ref, xs_ref, wg_ref, wu_ref, wd_ref, ys_ref):
    del tg_ref
    j = pl.program_id(0)
    tm = xs_ref.shape[0]
    d = wg_ref.shape[1]

    @pl.when(ts_ref[j] != j)
    def _():
        ys_ref[...] = jnp.zeros(ys_ref.shape, BF16)

    @pl.when(ts_ref[j] == j)
    def _():
        xt = xs_ref[:, 0:d]
        aux = xs_ref[:, d:d + AUX_LANES].astype(F32)
        lane = lax.broadcasted_iota(jnp.int32, (tm, AUX_LANES), 1)
        acts = []
        for e in range(EXPERTS_PER_GROUP):
            hg = jnp.dot(xt, wg_ref[e], preferred_element_type=F32)
            hu = jnp.dot(xt, wu_ref[e], preferred_element_type=F32)
            two_terms = (lane == e) | (lane == e + EXPERTS_PER_GROUP)
            ge = jnp.sum(jnp.where(two_terms, aux, 0.0), axis=-1, keepdims=True)
            acts.append(((hg * jax.nn.sigmoid(hg)) * hu * ge).astype(BF16))
        act = jnp.concatenate(acts, axis=1)
        ys_ref[...] = jnp.dot(act, wd_ref[...], preferred_element_type=F32).astype(BF16)


def _experts(tgroup, tsrc, xs, wg, wu, wd, *, tm):
    cap, width = xs.shape
    d = wg.shape[1]
    gh = GROUP_HIDDEN
    epg = EXPERTS_PER_GROUP
    grid_spec = pltpu.PrefetchScalarGridSpec(
        num_scalar_prefetch=2,
        grid=(cap // tm,),
        in_specs=[
            pl.BlockSpec((tm, width), lambda j, tg, ts: (ts[j], 0)),
            pl.BlockSpec((epg, d, EXPERT_HIDDEN), lambda j, tg, ts: (tg[j], 0, 0)),
            pl.BlockSpec((epg, d, EXPERT_HIDDEN), lambda j, tg, ts: (tg[j], 0, 0)),
            pl.BlockSpec((None, gh, d), lambda j, tg, ts: (tg[j], 0, 0)),
        ],
        out_specs=pl.BlockSpec((tm, d), lambda j, tg, ts: (j, 0)),
    )
    return pl.pallas_call(
        _expert_kernel,
        grid_spec=grid_spec,
        out_shape=jax.ShapeDtypeStruct((cap, d), BF16),
        compiler_params=pltpu.CompilerParams(dimension_semantics=("arbitrary",),
                                             vmem_limit_bytes=V7X_VMEM_LIMIT),
        name="moe_experts",
    )(tgroup, tsrc, xs, wg, wu, wd)


def _combine_kernel(tab_ref, pos_ref, x_ref, ys_ref, g_ref, b_ref, o_ref, buf_ref, sem_ref, *, alpha):
    i = pl.program_id(0)
    nt = pl.num_programs(0)
    slot = lax.rem(i, 2)
    ch = DISPATCH_CHUNK

    def chunk_copy(slot_, src_row, dst_row):
        return pltpu.make_async_copy(ys_ref.at[pl.ds(src_row, ch), :], buf_ref.at[slot_, pl.ds(dst_row, ch), :],
                                     sem_ref.at[slot_])

    def fetch(tile, slot_):
        for c in range(N_GROUPS):
            src0 = _tab(tab_ref, TAB_DEST, tile, c)
            dst0 = _tab(tab_ref, TAB_LOFF, tile, c)

            def issue(k, carry, src0=src0, dst0=dst0):
                chunk_copy(slot_, pl.multiple_of(src0 + k * ch, ch), pl.multiple_of(dst0 + k * ch, ch)).start()
                return carry
            lax.fori_loop(0, _tab(tab_ref, TAB_NCH, tile, c), issue, 0)

    @pl.when(i == 0)
    def _():
        buf_ref[...] = jnp.zeros(buf_ref.shape, BF16)
        fetch(0, 0)

    @pl.when(i + 1 < nt)
    def _():
        fetch(i + 1, 1 - slot)

    def one(k, carry):
        chunk_copy(slot, 0, 0).wait()
        return carry
    lax.fori_loop(0, _n_chunks(tab_ref, i), one, 0)

    perm_t = _permutation(pos_ref[...], COMBINE_ROWS).T.astype(BF16)
    y = jnp.dot(perm_t, buf_ref[slot], preferred_element_type=F32)
    o_ref[...] = _layer_norm(alpha * x_ref[...] + y, g_ref[...], b_ref[...])


def _combine(tab, pos, x1, ys, ln_g, ln_b, *, alpha):
    n, d = x1.shape
    t = DISPATCH_TILE
    grid_spec = pltpu.PrefetchScalarGridSpec(
        num_scalar_prefetch=1,
        grid=(n // t,),
        in_specs=[
            pl.BlockSpec((None, 1, t), lambda i, tab_ref: (i, 0, 0)),
            pl.BlockSpec((t, d), lambda i, tab_ref: (i, 0)),
            pl.BlockSpec(memory_space=pl.ANY),
            pl.BlockSpec((1, d), lambda i, tab_ref: (0, 0)),
            pl.BlockSpec((1, d), lambda i, tab_ref: (0, 0)),
        ],
        out_specs=pl.BlockSpec((t, d), lambda i, tab_ref: (i, 0)),
        scratch_shapes=[pltpu.VMEM((2, COMBINE_ROWS, d), BF16), pltpu.SemaphoreType.DMA((2,))],
    )
    return pl.pallas_call(
        functools.partial(_combine_kernel, alpha=alpha),
        grid_spec=grid_spec,
        out_shape=jax.ShapeDtypeStruct((n, d), F32),
        compiler_params=pltpu.CompilerParams(dimension_semantics=("arbitrary",),
                                             vmem_limit_bytes=V7X_VMEM_LIMIT),
        name="moe_combine",
    )(tab, pos, x1, ys, ln_g, ln_b)


def _block_diag(blocks):
    n, r, c = blocks.shape
    eye = jnp.eye(n, dtype=blocks.dtype)
    return jnp.einsum("grc,gh->grhc", blocks, eye).reshape(n * r, n * c)


def kernel(x, w_in, pool_w, pool_scale, conv_w, lam_q1, lam_k1, lam_q2, lam_k2, subln_g, sg_ln_g, sg_ln_b, sg_w, sg_b, w_branch, w_o, ln1_g, ln1_b, w_rg, b_rg, w_re, b_re, w_gate, w_up, w_down, ln2_g, ln2_b):
    b, s, d = x.shape
    depth = w_in.shape[0]
    n = b * s
    w = BRANCH_W
    alpha = (2 * depth) ** 0.25
    ta = min(512, s)
    tq = min(512, s)
    tc = min(512, s)
    tm = MOE_TILE

    half = DA_HEAD_DIM // 2
    inv_freq = ROPE_THETA ** (-jnp.arange(half, dtype=F32) / half)
    ang = jnp.arange(s, dtype=F32)[:, None] * inv_freq[None, :]
    cos, sin = jnp.cos(ang), jnp.sin(ang)
    reps = w // DA_HEAD_DIM
    cos_t = jnp.tile(jnp.concatenate([cos, cos], axis=-1), (1, reps))
    sin_t = jnp.tile(jnp.concatenate([-sin, sin], axis=-1), (1, reps))

    for l in range(depth):
        lam_init = 0.8 - 0.6 * math.exp(-0.3 * l)
        w_loc = w_in[l, :, :COL_GATE].astype(BF16)
        w_gate_in = w_in[l, :, COL_GATE:].astype(BF16)
        pool_bd = _block_diag(pool_w[l]).astype(BF16)
        sg_wcat = jnp.transpose(sg_w[l], (1, 0, 2)).reshape(SG_CHUNK, SG_GROUPS * SG_CHUNK).astype(BF16)
        sg_bias = jnp.repeat(sg_b[l].T, SG_GW, axis=1)
        lam_rows = jnp.zeros((8, 128), F32).at[0:4, 0:DA_HEAD_DIM].set(
            jnp.stack([lam_q1[l], lam_k1[l], lam_q2[l], lam_k2[l]]).astype(F32))
        subln_cols = jnp.broadcast_to(subln_g[l][:, None], (DA_V_DIM, tq))
        w_router = jnp.concatenate([w_rg[l], jnp.transpose(w_re[l], (1, 0, 2)).reshape(d, N_EXPERTS)], axis=1)
        w_router = jnp.pad(w_router, ((0, 0), (0, ROUTER_LANES - w_router.shape[1])))
        wr_hi = w_router.astype(BF16)
        wr_lo = (w_router - wr_hi.astype(F32)).astype(BF16)
        wr2 = jnp.concatenate([jnp.concatenate([wr_hi, wr_lo], axis=1),
                               jnp.concatenate([wr_hi, jnp.zeros_like(wr_lo)], axis=1)], axis=0)
        r_bias = jnp.pad(jnp.concatenate([b_rg[l], b_re[l].reshape(-1)]), (0, ROUTER_LANES - N_GROUPS - N_EXPERTS))[None, :]
        wg = w_gate[l].astype(BF16)
        wu = w_up[l].astype(BF16)
        wd = w_down[l].reshape(N_GROUPS, GROUP_HIDDEN, d).astype(BF16)

        yloc, q, k, vt = _local_mixer(x, w_loc, cos_t, sin_t, pool_bd, pool_scale[l][None, :], conv_w[l],
                                      sg_ln_g[l][None, :], sg_ln_b[l][None, :], sg_wcat, sg_bias, ta=ta)
        yct = _diff_attn(lam_rows, q, k, vt, subln_cols, lam_init=lam_init, tq=tq)
        x1, xa, gid = _merge(x.reshape(n, d), yloc.reshape(n, 3 * w), yct, w_gate_in,
                             w_branch[l].astype(BF16), w_o[l].astype(BF16), ln1_g[l][None, :], ln1_b[l][None, :],
                             wr2, r_bias, alpha=alpha, tc=tc)
        tab, gtab, pos, tgroup, tsrc, cap = _dispatch_tables(gid, n, tm)
        xs = _dispatch(tab, gtab, pos, xa, cap)
        ys = _experts(tgroup, tsrc, xs, wg, wu, wd, tm=tm)
        x2 = _combine(tab, pos, x1, ys, ln2_g[l][None, :], ln2_b[l][None, :], alpha=alpha)
        x = x2.reshape(b, s, d)
    return x
```

```python
import functools
import math

import jax
import jax.numpy as jnp
from jax import lax
from jax.experimental import pallas as pl
from jax.experimental.pallas import tpu as pltpu

F32 = jnp.float32
BF16 = jnp.bfloat16

BRANCH_W = 256
POOL_WINDOWS = (2, 4, 8, 16)
POOL_GW = 64
MAX_POOL = 16
CONV_W = 3
DA_HEADS = 4
DA_HEAD_DIM = 32
DA_V_DIM = 64
ROPE_THETA = 10000.0
SG_CHUNK = 128
SG_GROUPS = 4
SG_GW = 64
N_GROUPS = 4
EXPERTS_PER_GROUP = 4
N_EXPERTS = 16
EXPERT_HIDDEN = 256
GROUP_HIDDEN = EXPERTS_PER_GROUP * EXPERT_HIDDEN
N_BRANCH = 4
LN_EPS = 1e-5
NEG_INF = -1e30
ROUTER_LANES = 128
V7X_VMEM_LIMIT = 56 * 1024 * 1024

COL_POOL = 0
COL_CONV = BRANCH_W
COL_ATTN = 4 * BRANCH_W
COL_SG = 7 * BRANCH_W
COL_GATE = 9 * BRANCH_W


def _layer_norm(h, g, b):
    mu = jnp.mean(h, axis=-1, keepdims=True)
    hc = h - mu
    var = jnp.mean(hc * hc, axis=-1, keepdims=True)
    return hc * lax.rsqrt(var + LN_EPS) * g + b


def _gelu_tanh(x):
    c = math.sqrt(2.0 / math.pi)
    return 0.5 * x * (1.0 + jnp.tanh(c * (x + 0.044715 * (x * x * x))))


def _local_mixer_kernel(x_ref, w_ref, cos_ref, sin_ref, poolw_ref, pscale_ref, convw_ref, lng_ref, lnb_ref,
                        sgw_ref, sgb_ref, yloc_ref, qt_ref, k_ref, vt_ref, pext_ref, zext_ref,
                        ppool_ref, pconv_ref, pattn_ref, psg_ref, *, tiles_per_seq):
    step = pl.program_id(0)
    t = lax.rem(jnp.maximum(step - 1, 0), tiles_per_seq)
    ta = x_ref.shape[0]
    w = BRANCH_W
    lane = lax.broadcasted_iota(jnp.int32, (ta, w), 1)
    row = lax.broadcasted_iota(jnp.int32, (ta, w), 0)

    @pl.when(step == 0)
    def _():
        for ref in (ppool_ref, pconv_ref, pattn_ref, psg_ref):
            ref[...] = jnp.zeros(ref.shape, F32)

    @pl.when(t == 0)
    def _():
        pext_ref[0:MAX_POOL, :] = jnp.zeros((MAX_POOL, w), F32)
        zext_ref[0:8, :] = jnp.zeros((8, w), F32)

    @pl.when(t > 0)
    def _():
        pext_ref[0:MAX_POOL, :] = pext_ref[ta:ta + MAX_POOL, :]
        zext_ref[0:8, :] = zext_ref[ta:ta + 8, :]

    p = ppool_ref[...]
    pext_ref[MAX_POOL:MAX_POOL + ta, :] = p

    def prev(kk):
        return pext_ref[pl.ds(MAX_POOL - kk, ta), :]

    s2 = p + prev(1)
    s4 = s2 + (prev(2) + prev(3))
    s8 = s4 + ((prev(4) + prev(5)) + (prev(6) + prev(7)))
    s16 = s8 + (((prev(8) + prev(9)) + (prev(10) + prev(11))) + ((prev(12) + prev(13)) + (prev(14) + prev(15))))
    grp = lane // POOL_GW
    win_sum = jnp.where(grp == 0, s2, jnp.where(grp == 1, s4, jnp.where(grp == 2, s8, s16)))
    win = jnp.where(grp == 0, 2, jnp.where(grp == 1, 4, jnp.where(grp == 2, 8, 16)))
    count = jnp.minimum(t * ta + row + 1, win).astype(F32)
    d = (win_sum / count - p).astype(BF16)
    y_a = jnp.dot(d, poolw_ref[...], preferred_element_type=F32) * pscale_ref[...]
    yloc_ref[:, 0:w] = y_a.astype(BF16)

    pc = pconv_ref[...]
    gb = pc[:, 0:w]
    z = pc[:, w:2 * w] * pc[:, 2 * w:3 * w]
    zext_ref[8:8 + ta, :] = z
    cw = convw_ref[...]
    y_b = zext_ref[pl.ds(6, ta), :] * cw[0:1, :] + zext_ref[pl.ds(7, ta), :] * cw[1:2, :] + z * cw[2:3, :]
    yloc_ref[:, w:2 * w] = (gb * y_b).astype(BF16)

    pa = pattn_ref[...]
    cos = cos_ref[...]
    sin = sin_ref[...]
    first_half = (lane % DA_HEAD_DIM) < (DA_HEAD_DIM // 2)

    def rope(u):
        swapped = jnp.where(first_half, pltpu.roll(u, w - DA_HEAD_DIM // 2, axis=1),
                            pltpu.roll(u, DA_HEAD_DIM // 2, axis=1))
        return u * cos + swapped * sin

    qt_ref[...] = (rope(pa[:, 0:w]) * (DA_HEAD_DIM ** -0.5 * math.log2(math.e))).T.astype(BF16)
    k_ref[...] = rope(pa[:, w:2 * w]).astype(BF16)
    vt_ref[...] = pa[:, 2 * w:3 * w].T.astype(BF16)

    uv = _gelu_tanh(psg_ref[...])
    u = uv[:, 0:w]
    vn = _layer_norm(uv[:, w:2 * w], lng_ref[...], lnb_ref[...])
    wrow = lax.broadcasted_iota(jnp.int32, (SG_CHUNK, SG_GROUPS * SG_CHUNK), 0)
    wcol = lax.broadcasted_iota(jnp.int32, (SG_CHUNK, SG_GROUPS * SG_CHUNK), 1)
    ws = jnp.where(wrow >= (wcol % SG_CHUNK), sgw_ref[...], jnp.zeros((), BF16))
    cgrp = lax.broadcasted_iota(jnp.int32, (SG_CHUNK, w), 1) // SG_GW
    ys = []
    for c in range(ta // SG_CHUNK):
        vc = vn[c * SG_CHUNK:(c + 1) * SG_CHUNK, :]
        rhs = jnp.concatenate([jnp.where(cgrp == g, vc, 0.0) for g in range(SG_GROUPS)], axis=0).astype(BF16)
        ys.append(jnp.dot(ws, rhs, preferred_element_type=F32) + sgb_ref[...])
    y_d = u * jnp.concatenate(ys, axis=0)
    yloc_ref[:, 2 * w:3 * w] = y_d.astype(BF16)

    xb = x_ref[...].astype(BF16)
    for ref, col in ((ppool_ref, COL_POOL), (pconv_ref, COL_CONV), (pattn_ref, COL_ATTN), (psg_ref, COL_SG)):
        ref[...] = jnp.dot(xb, w_ref[:, col:col + ref.shape[1]], preferred_element_type=F32)


def _local_mixer(x, w_loc, cos_t, sin_t, pool_bd, pool_scale, conv_w, sg_ln_g, sg_ln_b, sg_wcat, sg_bias, *, ta):
    b, s, d = x.shape
    w = BRANCH_W
    ncol = w_loc.shape[1]
    tps = s // ta
    nt = b * tps
    cur = lambda i: jnp.minimum(i, nt - 1)
    prev = lambda i: jnp.maximum(i - 1, 0)
    full = lambda shape: pl.BlockSpec(shape, lambda i: (0,) * len(shape))
    out_blk = lambda width: pl.BlockSpec((None, ta, width), lambda i: (prev(i) // tps, prev(i) % tps, 0))
    out_blk_t = pl.BlockSpec((None, w, ta), lambda i: (prev(i) // tps, 0, prev(i) % tps))
    return pl.pallas_call(
        functools.partial(_local_mixer_kernel, tiles_per_seq=tps),
        grid=(nt + 1,),
        in_specs=[
            pl.BlockSpec((None, ta, d), lambda i: (cur(i) // tps, cur(i) % tps, 0)),
            full((d, ncol)),
            pl.BlockSpec((ta, w), lambda i: (prev(i) % tps, 0)),
            pl.BlockSpec((ta, w), lambda i: (prev(i) % tps, 0)),
            full((w, w)), full((1, w)), full((CONV_W, w)), full((1, w)), full((1, w)),
            full((SG_CHUNK, SG_GROUPS * SG_CHUNK)), full((SG_CHUNK, w)),
        ],
        out_specs=[out_blk(3 * w), out_blk_t, out_blk(w), out_blk_t],
        out_shape=[
            jax.ShapeDtypeStruct((b, s, 3 * w), BF16),
            jax.ShapeDtypeStruct((b, w, s), BF16),
            jax.ShapeDtypeStruct((b, s, w), BF16),
            jax.ShapeDtypeStruct((b, w, s), BF16),
        ],
        scratch_shapes=[pltpu.VMEM((ta + MAX_POOL, w), F32), pltpu.VMEM((ta + 8, w), F32),
                        pltpu.VMEM((ta, w), F32), pltpu.VMEM((ta, 3 * w), F32), pltpu.VMEM((ta, 3 * w), F32),
                        pltpu.VMEM((ta, 2 * w), F32)],
        compiler_params=pltpu.CompilerParams(dimension_semantics=("arbitrary",),
                                             vmem_limit_bytes=V7X_VMEM_LIMIT),
        name="local_mixer",
    )(x, w_loc, cos_t, sin_t, pool_bd, pool_scale, conv_w, sg_ln_g, sg_ln_b, sg_wcat, sg_bias)


ATTN_ONES_ROWS = 16
ATTN_LANE_CHUNK = 256


def _diff_attn_kernel(lam_ref, qt_ref, k_ref, vt_ref, g_ref, o_ref, q2_ref, s0_ref, s1_ref, cm0_ref, cm1_ref,
                      m_ref, acc_ref, *, lam_init):
    qi = pl.program_id(1)
    tq = qt_ref.shape[1]
    tk = tq
    w = BRANCH_W

    qt = qt_ref[...]
    sub = lax.broadcasted_iota(jnp.int32, (w, tq), 0) // DA_HEAD_DIM
    zero = jnp.zeros((), BF16)
    for h in range(DA_HEADS):
        q2_ref[h, :, 0:tq] = jnp.where(sub == 2 * h, qt, zero)
        q2_ref[h, :, tq:2 * tq] = jnp.where(sub == 2 * h + 1, qt, zero)

    lam_rows = lam_ref[...]
    lam = (jnp.exp(jnp.sum(lam_rows[0:1, :] * lam_rows[1:2, :], axis=-1, keepdims=True))
           - jnp.exp(jnp.sum(lam_rows[2:3, :] * lam_rows[3:4, :], axis=-1, keepdims=True)) + lam_init)

    def column_max(sc):
        parts = [sc[r * 8:(r + 1) * 8, :] for r in range(sc.shape[0] // 8)]
        while len(parts) > 1:
            parts = [jnp.maximum(parts[2 * r], parts[2 * r + 1]) for r in range(len(parts) // 2)]
        return jnp.max(parts[0], axis=0, keepdims=True)

    bufs = ((s0_ref, cm0_ref), (s1_ref, cm1_ref))

    def first_query(c):
        return (c * ATTN_LANE_CHUNK) % tq

    def keys_seen(c, diagonal):
        return first_query(c) + ATTN_LANE_CHUNK if diagonal else tk

    def causal(c):
        shape = (keys_seen(c, True), ATTN_LANE_CHUNK)
        return lax.broadcasted_iota(jnp.int32, shape, 0) <= lax.broadcasted_iota(jnp.int32, shape, 1) + first_query(c)

    def scores(h, kv, dst, diagonal=False):
        dst_ref, dst_max_ref = dst
        chans = slice((h // 2) * 2 * DA_V_DIM, (h // 2 + 1) * 2 * DA_V_DIM)
        kt = k_ref[pl.ds(pl.multiple_of(kv * tk, tk), tk), chans]
        for c in range(2 * tq // ATTN_LANE_CHUNK):
            cols = slice(c * ATTN_LANE_CHUNK, (c + 1) * ATTN_LANE_CHUNK)
            nk = keys_seen(c, diagonal)
            sc = jnp.dot(kt[0:nk, :], q2_ref[h, chans, cols], preferred_element_type=F32)
            if diagonal:
                sc = jnp.where(causal(c), sc, NEG_INF)
            dst_ref[0:nk, cols] = sc
            dst_max_ref[:, cols] = column_max(sc)

    def mask_diagonal(cur):
        cur_ref, cur_max_ref = cur
        for c in range(2 * tq // ATTN_LANE_CHUNK):
            cols = slice(c * ATTN_LANE_CHUNK, (c + 1) * ATTN_LANE_CHUNK)
            nk = keys_seen(c, True)
            sc = jnp.where(causal(c), cur_ref[0:nk, cols], NEG_INF)
            cur_ref[0:nk, cols] = sc
            cur_max_ref[:, cols] = column_max(sc)

    def softmax_pv(h, kv, cur, diagonal=False):
        cur_ref, cur_max_ref = cur
        vth = vt_ref[h * DA_V_DIM:(h + 1) * DA_V_DIM, pl.ds(pl.multiple_of(kv * tk, tk), tk)]
        lhs = jnp.concatenate([vth, jnp.ones((ATTN_ONES_ROWS, tk), BF16)], axis=0)
        for c in range(2 * tq // ATTN_LANE_CHUNK):
            cols = slice(c * ATTN_LANE_CHUNK, (c + 1) * ATTN_LANE_CHUNK)
            nk = keys_seen(c, diagonal)
            sc = cur_ref[0:nk, cols]
            m_old = m_ref[h, :, cols]
            m_new = jnp.maximum(m_old, cur_max_ref[:, cols])
            e = jnp.exp2(sc - m_new).astype(BF16)
            pv = jnp.dot(lhs[:, 0:nk], e, preferred_element_type=F32)
            acc_ref[h, :, cols] = acc_ref[h, :, cols] * jnp.exp2(m_old - m_new) + pv
            m_ref[h, :, cols] = m_new

    def finalize(h):
        o1 = acc_ref[h, 0:DA_V_DIM, 0:tq]
        o2 = acc_ref[h, 0:DA_V_DIM, tq:2 * tq]
        r1 = 1.0 / acc_ref[h, DA_V_DIM:DA_V_DIM + 1, 0:tq]
        r2 = 1.0 / acc_ref[h, DA_V_DIM:DA_V_DIM + 1, tq:2 * tq]
        a = o1 * r1 - lam * (o2 * r2)
        ms = jnp.mean(a * a, axis=0, keepdims=True)
        y = (a * lax.rsqrt(ms + LN_EPS)) * (1.0 - lam_init) * g_ref[...]
        o_ref[h * DA_V_DIM:(h + 1) * DA_V_DIM, :] = y.astype(BF16)

    m_ref[...] = jnp.full(m_ref.shape, NEG_INF, F32)
    acc_ref[...] = jnp.zeros(acc_ref.shape, F32)
    scores(0, 0, bufs[0])

    def full_tile(kv, carry):
        for h in range(DA_HEADS):
            nxt_h, nxt_kv = (h + 1, kv) if h + 1 < DA_HEADS else (0, kv + 1)
            scores(nxt_h, nxt_kv, bufs[(h + 1) % 2])
            softmax_pv(h, kv, bufs[h % 2])
        return carry

    lax.fori_loop(0, qi, full_tile, 0)

    mask_diagonal(bufs[0])
    for h in range(DA_HEADS):
        if h + 1 < DA_HEADS:
            scores(h + 1, qi, bufs[(h + 1) % 2], diagonal=True)
        softmax_pv(h, qi, bufs[h % 2], diagonal=True)
        finalize(h)


def _diff_attn(lam_rows, qt, k, vt, subln_cols, *, lam_init, tq):
    b, s, w = k.shape
    return pl.pallas_call(
        functools.partial(_diff_attn_kernel, lam_init=lam_init),
        grid=(b, s // tq),
        in_specs=[
            pl.BlockSpec((8, 128), lambda i, j: (0, 0)),
            pl.BlockSpec((None, w, tq), lambda i, j: (i, 0, j)),
            pl.BlockSpec((None, s, w), lambda i, j: (i, 0, 0)),
            pl.BlockSpec((None, w, s), lambda i, j: (i, 0, 0)),
            pl.BlockSpec((DA_V_DIM, tq), lambda i, j: (0, 0)),
        ],
        out_specs=pl.BlockSpec((None, w, tq), lambda i, j: (i, 0, j)),
        out_shape=jax.ShapeDtypeStruct((b, w, s), BF16),
        scratch_shapes=[pltpu.VMEM((DA_HEADS, w, 2 * tq), BF16),
                        pltpu.VMEM((tq, 2 * tq), F32), pltpu.VMEM((tq, 2 * tq), F32),
                        pltpu.VMEM((1, 2 * tq), F32), pltpu.VMEM((1, 2 * tq), F32),
                        pltpu.VMEM((DA_HEADS, 1, 2 * tq), F32),
                        pltpu.VMEM((DA_HEADS, DA_V_DIM + ATTN_ONES_ROWS, 2 * tq), F32)],
        compiler_params=pltpu.CompilerParams(dimension_semantics=("parallel", "parallel"),
                                             vmem_limit_bytes=V7X_VMEM_LIMIT),
        name="diff_attn",
    )(lam_rows, qt, k, vt, subln_cols)


def _route(x, wr2, bias):
    rows = x.shape[0]
    xh = x.astype(BF16)
    xl = (x - xh.astype(F32)).astype(BF16)
    prod = jnp.dot(jnp.concatenate([xh, xl], axis=1), wr2, preferred_element_type=F32)
    logits = prod[:, 0:ROUTER_LANES] + prod[:, ROUTER_LANES:2 * ROUTER_LANES] + bias
    lt = logits.T

    def first_argmax(vals):
        best = vals[0]
        for v in vals[1:]:
            best = jnp.maximum(best, v)
        idx = jnp.full(best.shape, len(vals) - 1, jnp.int32)
        for i in range(len(vals) - 2, -1, -1):
            idx = jnp.where(vals[i] == best, i, idx)
        return best, idx

    gl = [lt[c:c + 1, :] for c in range(N_GROUPS)]
    gmax, g_sel = first_argmax(gl)
    denom = jnp.exp(gl[0] - gmax)
    for c in range(1, N_GROUPS):
        denom = denom + jnp.exp(gl[c] - gmax)
    p_sel = 1.0 / denom
    el = []
    for e in range(EXPERTS_PER_GROUP):
        row = N_GROUPS + EXPERTS_PER_GROUP * (N_GROUPS - 1) + e
        v = lt[row:row + 1, :]
        for c in range(N_GROUPS - 2, -1, -1):
            row = N_GROUPS + EXPERTS_PER_GROUP * c + e
            v = jnp.where(g_sel == c, lt[row:row + 1, :], v)
        el.append(v)
    v1, i1 = first_argmax(el)
    v2, i2 = first_argmax([jnp.where(i1 == e, NEG_INF, el[e]) for e in range(EXPERTS_PER_GROUP)])
    e2 = jnp.exp(v2 - v1)
    w1 = p_sel / (1.0 + e2)
    w2 = w1 * e2
    gates = [jnp.where(i1 == e, w1, jnp.where(i2 == e, w2, 0.0)) for e in range(EXPERTS_PER_GROUP)]
    hi = [g.astype(BF16).astype(F32) for g in gates]
    lo = [g - h for g, h in zip(gates, hi)]
    pad = jnp.zeros((ROUTER_LANES - 2 * EXPERTS_PER_GROUP, rows), F32)
    aux = jnp.concatenate(hi + lo + [pad], axis=0).T
    return aux, g_sel


def _merge_kernel(x_ref, yloc_ref, yct_ref, wgate_ref, wbr_ref, wo_ref, g_ref, b_ref, wr2_ref, rb_ref,
                  o_ref, xa_ref, gid_ref, h_ref, *, alpha):
    w = BRANCH_W
    tc, d = x_ref.shape

    @pl.when(pl.program_id(0) == 0)
    def _():
        h_ref[...] = jnp.zeros(h_ref.shape, F32)

    x1 = _layer_norm(h_ref[...], g_ref[...], b_ref[...])
    o_ref[...] = x1
    aux, g_row = _route(x1, wr2_ref[...], rb_ref[...])
    xa_ref[:, 0:d] = x1.astype(BF16)
    xa_ref[:, d:d + AUX_LANES] = aux.astype(BF16)
    gid_ref[...] = g_row

    x = x_ref[...]
    xb = x.astype(BF16)
    y_c = yct_ref[...].astype(F32).T.astype(BF16)
    branches = (yloc_ref[:, 0:w], yloc_ref[:, w:2 * w], y_c, yloc_ref[:, 2 * w:3 * w])
    merged = None
    for i in range(N_BRANCH):
        gate = jax.nn.sigmoid(jnp.dot(xb, wgate_ref[:, i * d:(i + 1) * d], preferred_element_type=F32))
        term = gate * jnp.dot(branches[i], wbr_ref[i], preferred_element_type=F32)
        merged = term if merged is None else merged + term
    mix = jnp.dot(merged.astype(BF16), wo_ref[...], preferred_element_type=F32)
    h_ref[...] = alpha * x + mix


def _merge(x, yloc, yct, w_gate_in, w_branch, w_o, ln_g, ln_b, wr2, r_bias, *, alpha, tc):
    n, d = x.shape
    w = BRANCH_W
    s = yct.shape[2]
    tiles_per_row = s // tc
    nt = n // tc
    cur = lambda i: jnp.minimum(i, nt - 1)
    prev = lambda i: jnp.maximum(i - 1, 0)
    tok_in = lambda width: pl.BlockSpec((tc, width), lambda i: (cur(i), 0))
    tok_out = lambda width: pl.BlockSpec((tc, width), lambda i: (prev(i), 0))
    const = lambda shape: pl.BlockSpec(shape, lambda i: (0,) * len(shape))
    return pl.pallas_call(
        functools.partial(_merge_kernel, alpha=alpha),
        grid=(nt + 1,),
        in_specs=[
            tok_in(d), tok_in(3 * w),
            pl.BlockSpec((None, w, tc), lambda i: (cur(i) // tiles_per_row, 0, cur(i) % tiles_per_row)),
            const((d, N_BRANCH * d)), const((N_BRANCH, w, d)), const((d, d)), const((1, d)), const((1, d)),
            const((2 * d, 2 * ROUTER_LANES)), const((1, ROUTER_LANES)),
        ],
        out_specs=[tok_out(d), tok_out(d + AUX_LANES), pl.BlockSpec((None, 1, tc), lambda i: (prev(i), 0, 0))],
        out_shape=[jax.ShapeDtypeStruct((n, d), F32), jax.ShapeDtypeStruct((n, d + AUX_LANES), BF16),
                   jax.ShapeDtypeStruct((n // tc, 1, tc), jnp.int32)],
        scratch_shapes=[pltpu.VMEM((tc, d), F32)],
        compiler_params=pltpu.CompilerParams(dimension_semantics=("arbitrary",),
                                             vmem_limit_bytes=V7X_VMEM_LIMIT),
        name="gated_merge",
    )(x, yloc, yct, w_gate_in, w_branch, w_o, ln_g, ln_b, wr2, r_bias)


DISPATCH_TILE = 512
DISPATCH_CHUNK = 16
DISPATCH_ROWS = DISPATCH_TILE + N_GROUPS * DISPATCH_CHUNK
COMBINE_ROWS = -(-DISPATCH_ROWS // 128) * 128
AUX_LANES = 128
MOE_TILE = 512
TAB_DEST, TAB_LOFF, TAB_NCH = 0, 1, 2


def _dispatch_tables(gid, n, tm):
    t, ch = DISPATCH_TILE, DISPATCH_CHUNK
    nt = n // t
    g = gid.reshape(nt, t)
    counts = jnp.sum((g[:, :, None] == jnp.arange(N_GROUPS, dtype=jnp.int32)).astype(jnp.int32), axis=1)
    nch = (counts + ch - 1) // ch
    rows = nch * ch
    loff = jnp.cumsum(rows, axis=1) - rows
    coff = jnp.cumsum(rows, axis=0) - rows
    gsize = jnp.sum(rows, axis=0)
    gpad = (gsize + tm - 1) // tm * tm
    gstart = jnp.cumsum(gpad) - gpad
    dest = gstart[None, :] + coff
    tab = jnp.concatenate([dest.reshape(-1), loff.reshape(-1), nch.reshape(-1)]).astype(jnp.int32)
    onehot = (g[:, :, None] == jnp.arange(N_GROUPS, dtype=jnp.int32)).astype(jnp.int32)
    rank = jnp.cumsum(onehot, axis=1) - onehot
    pos = jnp.sum(onehot * (rank + loff[:, None, :]), axis=2).astype(jnp.int32).reshape(nt, 1, t)
    cap = -(-(n + nt * N_GROUPS * ch + N_GROUPS * tm) // tm) * tm
    used = jnp.sum(gpad)
    gtab = jnp.concatenate([gstart + gsize, (gpad - gsize) // ch, used[None], (cap - used)[None] // tm])
    gtab = gtab.astype(jnp.int32)
    tile = jnp.arange(cap // tm, dtype=jnp.int32)
    tgroup = jnp.minimum(jnp.sum((tile[:, None] * tm >= (gstart + gpad)[None, :]).astype(jnp.int32), axis=1),
                         N_GROUPS - 1)
    tsrc = jnp.minimum(tile, jnp.sum(gpad) // tm - 1)
    return tab, gtab, pos, tgroup, tsrc, cap


def _tab(tab_ref, section, tile, grp):
    n_entries = tab_ref.shape[0] // 3
    return tab_ref[section * n_entries + tile * N_GROUPS + grp]


def _n_chunks(tab_ref, tile):
    total = _tab(tab_ref, TAB_NCH, tile, 0)
    for c in range(1, N_GROUPS):
        total = total + _tab(tab_ref, TAB_NCH, tile, c)
    return total


def _permutation(pos_row, n_rows):
    rows = lax.broadcasted_iota(jnp.int32, (n_rows, pos_row.shape[1]), 0)
    return jnp.where(rows == pos_row, 1.0, 0.0)


def _dispatch_kernel(tab_ref, gtab_ref, pos_ref, xa_ref, xs_ref, buf_ref, zero_ref, sem_ref):
    i = pl.program_id(0)
    nt = pl.num_programs(0)
    slot = lax.rem(i, 2)
    ch = DISPATCH_CHUNK

    def chunk_copy(slot_, src_row, dst_row):
        return pltpu.make_async_copy(buf_ref.at[slot_, pl.ds(src_row, ch), :], xs_ref.at[pl.ds(dst_row, ch), :],
                                     sem_ref.at[slot_])

    def wait_tile(tile, slot_):
        def one(k, carry):
            chunk_copy(slot_, 0, 0).wait()
            return carry
        lax.fori_loop(0, _n_chunks(tab_ref, tile), one, 0)

    @pl.when(i >= 2)
    def _():
        wait_tile(i - 2, slot)

    perm = _permutation(pos_ref[...], DISPATCH_ROWS).astype(BF16)
    buf_ref[slot] = jnp.dot(perm, xa_ref[...], preferred_element_type=F32).astype(BF16)
    for c in range(N_GROUPS):
        src0 = _tab(tab_ref, TAB_LOFF, i, c)
        dst0 = _tab(tab_ref, TAB_DEST, i, c)

        def issue(k, carry, src0=src0, dst0=dst0):
            chunk_copy(slot, pl.multiple_of(src0 + k * ch, ch), pl.multiple_of(dst0 + k * ch, ch)).start()
            return carry
        lax.fori_loop(0, _tab(tab_ref, TAB_NCH, i, c), issue, 0)

    @pl.when(i == nt - 1)
    def _():
        zero_ref[...] = jnp.zeros(zero_ref.shape, BF16)

        def tail_copy(dst_row):
            return pltpu.make_async_copy(zero_ref.at[pl.ds(0, ch), :], xs_ref.at[pl.ds(dst_row, ch), :],
                                         sem_ref.at[2])

        tile_rows = zero_ref.shape[0]

        def spare_copy(dst_row):
            return pltpu.make_async_copy(zero_ref, xs_ref.at[pl.ds(dst_row, tile_rows), :], sem_ref.at[3])

        def fill_spare(k, carry):
            spare_copy(pl.multiple_of(gtab_ref[2 * N_GROUPS] + k * tile_rows, tile_rows)).start()
            return carry
        lax.fori_loop(0, gtab_ref[2 * N_GROUPS + 1], fill_spare, 0)

        def drain_spare(k, carry):
            spare_copy(0).wait()
            return carry
        lax.fori_loop(0, gtab_ref[2 * N_GROUPS + 1], drain_spare, 0)

        n_tail = 0
        for c in range(N_GROUPS):
            dst0 = gtab_ref[c]

            def fill(k, carry, dst0=dst0):
                tail_copy(pl.multiple_of(dst0 + k * ch, ch)).start()
                return carry
            lax.fori_loop(0, gtab_ref[N_GROUPS + c], fill, 0)
            n_tail = n_tail + gtab_ref[N_GROUPS + c]

        def drain(k, carry):
            tail_copy(0).wait()
            return carry
        lax.fori_loop(0, n_tail, drain, 0)

        wait_tile(i, slot)

        @pl.when(i >= 1)
        def _():
            wait_tile(i - 1, 1 - slot)


def _dispatch(tab, gtab, pos, xa, cap):
    n, width = xa.shape
    t = DISPATCH_TILE
    grid_spec = pltpu.PrefetchScalarGridSpec(
        num_scalar_prefetch=2,
        grid=(n // t,),
        in_specs=[
            pl.BlockSpec((None, 1, t), lambda i, tab_ref, gtab_ref: (i, 0, 0)),
            pl.BlockSpec((t, width), lambda i, tab_ref, gtab_ref: (i, 0)),
        ],
        out_specs=pl.BlockSpec(memory_space=pl.ANY),
        scratch_shapes=[pltpu.VMEM((2, DISPATCH_ROWS, width), BF16), pltpu.VMEM((MOE_TILE, width), BF16),
                        pltpu.SemaphoreType.DMA((4,))],
    )
    return pl.pallas_call(
        _dispatch_kernel,
        grid_spec=grid_spec,
        out_shape=jax.ShapeDtypeStruct((cap, width), BF16),
        compiler_params=pltpu.CompilerParams(dimension_semantics=("arbitrary",),
                                             vmem_limit_bytes=V7X_VMEM_LIMIT),
        name="moe_dispatch",
    )(tab, gtab, pos, xa)


def _expert_kernel(tg_ref, ts_ref, xs_ref, wg_ref, wu_ref, wd_ref, ys_ref):
    del tg_ref
    j = pl.program_id(0)
    tm = xs_ref.shape[0]
    d = wg_ref.shape[1]

    @pl.when(ts_ref[j] != j)
    def _():
        ys_ref[...] = jnp.zeros(ys_ref.shape, BF16)

    @pl.when(ts_ref[j] == j)
    def _():
        xt = xs_ref[:, 0:d]
        aux = xs_ref[:, d:d + AUX_LANES].astype(F32)
        lane = lax.broadcasted_iota(jnp.int32, (tm, AUX_LANES), 1)
        acts = []
        for e in range(EXPERTS_PER_GROUP):
            hg = jnp.dot(xt, wg_ref[e], preferred_element_type=F32)
            hu = jnp.dot(xt, wu_ref[e], preferred_element_type=F32)
            two_terms = (lane == e) | (lane == e + EXPERTS_PER_GROUP)
            ge = jnp.sum(jnp.where(two_terms, aux, 0.0), axis=-1, keepdims=True)
            acts.append(((hg * jax.nn.sigmoid(hg)) * hu * ge).astype(BF16))
        act = jnp.concatenate(acts, axis=1)
        ys_ref[...] = jnp.dot(act, wd_ref[...], preferred_element_type=F32).astype(BF16)


def _experts(tgroup, tsrc, xs, wg, wu, wd, *, tm):
    cap, width = xs.shape
    d = wg.shape[1]
    gh = GROUP_HIDDEN
    epg = EXPERTS_PER_GROUP
    grid_spec = pltpu.PrefetchScalarGridSpec(
        num_scalar_prefetch=2,
        grid=(cap // tm,),
        in_specs=[
            pl.BlockSpec((tm, width), lambda j, tg, ts: (ts[j], 0)),
            pl.BlockSpec((epg, d, EXPERT_HIDDEN), lambda j, tg, ts: (tg[j], 0, 0)),
            pl.BlockSpec((epg, d, EXPERT_HIDDEN), lambda j, tg, ts: (tg[j], 0, 0)),
            pl.BlockSpec((None, gh, d), lambda j, tg, ts: (tg[j], 0, 0)),
        ],
        out_specs=pl.BlockSpec((tm, d), lambda j, tg, ts: (j, 0)),
    )
    return pl.pallas_call(
        _expert_kernel,
        grid_spec=grid_spec,
        out_shape=jax.ShapeDtypeStruct((cap, d), BF16),
        compiler_params=pltpu.CompilerParams(dimension_semantics=("arbitrary",),
                                             vmem_limit_bytes=V7X_VMEM_LIMIT),
        name="moe_experts",
    )(tgroup, tsrc, xs, wg, wu, wd)


def _combine_kernel(tab_ref, pos_ref, x_ref, ys_ref, g_ref, b_ref, o_ref, buf_ref, sem_ref, *, alpha):
    i = pl.program_id(0)
    nt = pl.num_programs(0)
    slot = lax.rem(i, 2)
    ch = DISPATCH_CHUNK

    def chunk_copy(slot_, src_row, dst_row):
        return pltpu.make_async_copy(ys_ref.at[pl.ds(src_row, ch), :], buf_ref.at[slot_, pl.ds(dst_row, ch), :],
                                     sem_ref.at[slot_])

    def fetch(tile, slot_):
        for c in range(N_GROUPS):
            src0 = _tab(tab_ref, TAB_DEST, tile, c)
            dst0 = _tab(tab_ref, TAB_LOFF, tile, c)

            def issue(k, carry, src0=src0, dst0=dst0):
                chunk_copy(slot_, pl.multiple_of(src0 + k * ch, ch), pl.multiple_of(dst0 + k * ch, ch)).start()
                return carry
            lax.fori_loop(0, _tab(tab_ref, TAB_NCH, tile, c), issue, 0)

    @pl.when(i == 0)
    def _():
        buf_ref[...] = jnp.zeros(buf_ref.shape, BF16)
        fetch(0, 0)

    @pl.when(i + 1 < nt)
    def _():
        fetch(i + 1, 1 - slot)

    def one(k, carry):
        chunk_copy(slot, 0, 0).wait()
        return carry
    lax.fori_loop(0, _n_chunks(tab_ref, i), one, 0)

    perm_t = _permutation(pos_ref[...], COMBINE_ROWS).T.astype(BF16)
    y = jnp.dot(perm_t, buf_ref[slot], preferred_element_type=F32)
    o_ref[...] = _layer_norm(alpha * x_ref[...] + y, g_ref[...], b_ref[...])


def _combine(tab, pos, x1, ys, ln_g, ln_b, *, alpha):
    n, d = x1.shape
    t = DISPATCH_TILE
    grid_spec = pltpu.PrefetchScalarGridSpec(
        num_scalar_prefetch=1,
        grid=(n // t,),
        in_specs=[
            pl.BlockSpec((None, 1, t), lambda i, tab_ref: (i, 0, 0)),
            pl.BlockSpec((t, d), lambda i, tab_ref: (i, 0)),
            pl.BlockSpec(memory_space=pl.ANY),
            pl.BlockSpec((1, d), lambda i, tab_ref: (0, 0)),
            pl.BlockSpec((1, d), lambda i, tab_ref: (0, 0)),
        ],
        out_specs=pl.BlockSpec((t, d), lambda i, tab_ref: (i, 0)),
        scratch_shapes=[pltpu.VMEM((2, COMBINE_ROWS, d), BF16), pltpu.SemaphoreType.DMA((2,))],
    )
    return pl.pallas_call(
        functools.partial(_combine_kernel, alpha=alpha),
        grid_spec=grid_spec,
        out_shape=jax.ShapeDtypeStruct((n, d), F32),
        compiler_params=pltpu.CompilerParams(dimension_semantics=("arbitrary",),
                                             vmem_limit_bytes=V7X_VMEM_LIMIT),
        name="moe_combine",
    )(tab, pos, x1, ys, ln_g, ln_b)


def _block_diag(blocks):
    n, r, c = blocks.shape
    eye = jnp.eye(n, dtype=blocks.dtype)
    return jnp.einsum("grc,gh->grhc", blocks, eye).reshape(n * r, n * c)


def kernel(x, w_in, pool_w, pool_scale, conv_w, lam_q1, lam_k1, lam_q2, lam_k2, subln_g, sg_ln_g, sg_ln_b, sg_w, sg_b, w_branch, w_o, ln1_g, ln1_b, w_rg, b_rg, w_re, b_re, w_gate, w_up, w_down, ln2_g, ln2_b):
    b, s, d = x.shape
    depth = w_in.shape[0]
    n = b * s
    w = BRANCH_W
    alpha = (2 * depth) ** 0.25
    ta = min(512, s)
    tq = min(512, s)
    tc = min(512, s)
    tm = MOE_TILE

    half = DA_HEAD_DIM // 2
    inv_freq = ROPE_THETA ** (-jnp.arange(half, dtype=F32) / half)
    ang = jnp.arange(s, dtype=F32)[:, None] * inv_freq[None, :]
    cos, sin = jnp.cos(ang), jnp.sin(ang)
    reps = w // DA_HEAD_DIM
    cos_t = jnp.tile(jnp.concatenate([cos, cos], axis=-1), (1, reps))
    sin_t = jnp.tile(jnp.concatenate([-sin, sin], axis=-1), (1, reps))

    for l in range(depth):
        lam_init = 0.8 - 0.6 * math.exp(-0.3 * l)
        w_loc = w_in[l, :, :COL_GATE].astype(BF16)
        w_gate_in = w_in[l, :, COL_GATE:].astype(BF16)
        pool_bd = _block_diag(pool_w[l]).astype(BF16)
        sg_wcat = jnp.transpose(sg_w[l], (1, 0, 2)).reshape(SG_CHUNK, SG_GROUPS * SG_CHUNK).astype(BF16)
        sg_bias = jnp.repeat(sg_b[l].T, SG_GW, axis=1)
        lam_rows = jnp.zeros((8, 128), F32).at[0:4, 0:DA_HEAD_DIM].set(
            jnp.stack([lam_q1[l], lam_k1[l], lam_q2[l], lam_k2[l]]).astype(F32))
        subln_cols = jnp.broadcast_to(subln_g[l][:, None], (DA_V_DIM, tq))
        w_router = jnp.concatenate([w_rg[l], jnp.transpose(w_re[l], (1, 0, 2)).reshape(d, N_EXPERTS)], axis=1)
        w_router = jnp.pad(w_router, ((0, 0), (0, ROUTER_LANES - w_router.shape[1])))
        wr_hi = w_router.astype(BF16)
        wr_lo = (w_router - wr_hi.astype(F32)).astype(BF16)
        wr2 = jnp.concatenate([jnp.concatenate([wr_hi, wr_lo], axis=1),
                               jnp.concatenate([wr_hi, jnp.zeros_like(wr_lo)], axis=1)], axis=0)
        r_bias = jnp.pad(jnp.concatenate([b_rg[l], b_re[l].reshape(-1)]), (0, ROUTER_LANES - N_GROUPS - N_EXPERTS))[None, :]
        wg = w_gate[l].astype(BF16)
        wu = w_up[l].astype(BF16)
        wd = w_down[l].reshape(N_GROUPS, GROUP_HIDDEN, d).astype(BF16)

        yloc, q, k, vt = _local_mixer(x, w_loc, cos_t, sin_t, pool_bd, pool_scale[l][None, :], conv_w[l],
                                      sg_ln_g[l][None, :], sg_ln_b[l][None, :], sg_wcat, sg_bias, ta=ta)
        yct = _diff_attn(lam_rows, q, k, vt, subln_cols, lam_init=lam_init, tq=tq)
        x1, xa, gid = _merge(x.reshape(n, d), yloc.reshape(n, 3 * w), yct, w_gate_in,
                             w_branch[l].astype(BF16), w_o[l].astype(BF16), ln1_g[l][None, :], ln1_b[l][None, :],
                             wr2, r_bias, alpha=alpha, tc=tc)
        tab, gtab, pos, tgroup, tsrc, cap = _dispatch_tables(gid, n, tm)
        xs = _dispatch(tab, gtab, pos, xa, cap)
        ys = _experts(tgroup, tsrc, xs, wg, wu, wd, tm=tm)
        x2 = _combine(tab, pos, x1, ys, ln2_g[l][None, :], ln2_b[l][None, :], alpha=alpha)
        x = x2.reshape(b, s, d)
    return x
```

```python
import functools
import math

import jax
import jax.numpy as jnp
from jax import lax
from jax.experimental import pallas as pl
from jax.experimental.pallas import tpu as pltpu

F32 = jnp.float32
BF16 = jnp.bfloat16

BRANCH_W = 256
POOL_WINDOWS = (2, 4, 8, 16)
POOL_GW = 64
MAX_POOL = 16
CONV_W = 3
DA_HEADS = 4
DA_HEAD_DIM = 32
DA_V_DIM = 64
ROPE_THETA = 10000.0
SG_CHUNK = 128
SG_GROUPS = 4
SG_GW = 64
N_GROUPS = 4
EXPERTS_PER_GROUP = 4
N_EXPERTS = 16
EXPERT_HIDDEN = 256
GROUP_HIDDEN = EXPERTS_PER_GROUP * EXPERT_HIDDEN
N_BRANCH = 4
LN_EPS = 1e-5
NEG_INF = -1e30
ROUTER_LANES = 128
V7X_VMEM_LIMIT = 56 * 1024 * 1024

COL_POOL = 0
COL_CONV = BRANCH_W
COL_ATTN = 4 * BRANCH_W
COL_SG = 7 * BRANCH_W
COL_GATE = 9 * BRANCH_W


def _layer_norm(h, g, b):
    mu = jnp.mean(h, axis=-1, keepdims=True)
    hc = h - mu
    var = jnp.mean(hc * hc, axis=-1, keepdims=True)
    return hc * lax.rsqrt(var + LN_EPS) * g + b


def _gelu_tanh(x):
    c = math.sqrt(2.0 / math.pi)
    return 0.5 * x * (1.0 + jnp.tanh(c * (x + 0.044715 * (x * x * x))))


def _local_mixer_kernel(x_ref, w_ref, cos_ref, sin_ref, poolw_ref, pscale_ref, convw_ref, lng_ref, lnb_ref,
                        sgw_ref, sgb_ref, yloc_ref, qt_ref, k_ref, vt_ref, pext_ref, zext_ref,
                        ppool_ref, pconv_ref, pattn_ref, psg_ref, *, tiles_per_seq):
    step = pl.program_id(0)
    t = lax.rem(jnp.maximum(step - 1, 0), tiles_per_seq)
    ta = x_ref.shape[0]
    w = BRANCH_W
    lane = lax.broadcasted_iota(jnp.int32, (ta, w), 1)
    row = lax.broadcasted_iota(jnp.int32, (ta, w), 0)

    @pl.when(step == 0)
    def _():
        for ref in (ppool_ref, pconv_ref, pattn_ref, psg_ref):
            ref[...] = jnp.zeros(ref.shape, F32)

    @pl.when(t == 0)
    def _():
        pext_ref[0:MAX_POOL, :] = jnp.zeros((MAX_POOL, w), F32)
        zext_ref[0:8, :] = jnp.zeros((8, w), F32)

    @pl.when(t > 0)
    def _():
        pext_ref[0:MAX_POOL, :] = pext_ref[ta:ta + MAX_POOL, :]
        zext_ref[0:8, :] = zext_ref[ta:ta + 8, :]

    p = ppool_ref[...]
    pext_ref[MAX_POOL:MAX_POOL + ta, :] = p

    def prev(kk):
        return pext_ref[pl.ds(MAX_POOL - kk, ta), :]

    s2 = p + prev(1)
    s4 = s2 + (prev(2) + prev(3))
    s8 = s4 + ((prev(4) + prev(5)) + (prev(6) + prev(7)))
    s16 = s8 + (((prev(8) + prev(9)) + (prev(10) + prev(11))) + ((prev(12) + prev(13)) + (prev(14) + prev(15))))
    grp = lane // POOL_GW
    win_sum = jnp.where(grp == 0, s2, jnp.where(grp == 1, s4, jnp.where(grp == 2, s8, s16)))
    win = jnp.where(grp == 0, 2, jnp.where(grp == 1, 4, jnp.where(grp == 2, 8, 16)))
    count = jnp.minimum(t * ta + row + 1, win).astype(F32)
    d = (win_sum / count - p).astype(BF16)
    y_a = jnp.dot(d, poolw_ref[...], preferred_element_type=F32) * pscale_ref[...]
    yloc_ref[:, 0:w] = y_a.astype(BF16)

    pc = pconv_ref[...]
    gb = pc[:, 0:w]
    z = pc[:, w:2 * w] * pc[:, 2 * w:3 * w]
    zext_ref[8:8 + ta, :] = z
    cw = convw_ref[...]
    y_b = zext_ref[pl.ds(6, ta), :] * cw[0:1, :] + zext_ref[pl.ds(7, ta), :] * cw[1:2, :] + z * cw[2:3, :]
    yloc_ref[:, w:2 * w] = (gb * y_b).astype(BF16)

    pa = pattn_ref[...]
    cos = cos_ref[...]
    sin = sin_ref[...]
    first_half = (lane % DA_HEAD_DIM) < (DA_HEAD_DIM // 2)

    def rope(u):
        swapped = jnp.where(first_half, pltpu.roll(u, w - DA_HEAD_DIM // 2, axis=1),
                            pltpu.roll(u, DA_HEAD_DIM // 2, axis=1))
        return u * cos + swapped * sin

    qt_ref[...] = (rope(pa[:, 0:w]) * (DA_HEAD_DIM ** -0.5 * math.log2(math.e))).T.astype(BF16)
    k_ref[...] = rope(pa[:, w:2 * w]).astype(BF16)
    vt_ref[...] = pa[:, 2 * w:3 * w].T.astype(BF16)

    uv = _gelu_tanh(psg_ref[...])
    u = uv[:, 0:w]
    vn = _layer_norm(uv[:, w:2 * w], lng_ref[...], lnb_ref[...])
    wrow = lax.broadcasted_iota(jnp.int32, (SG_CHUNK, SG_GROUPS * SG_CHUNK), 0)
    wcol = lax.broadcasted_iota(jnp.int32, (SG_CHUNK, SG_GROUPS * SG_CHUNK), 1)
    ws = jnp.where(wrow >= (wcol % SG_CHUNK), sgw_ref[...], jnp.zeros((), BF16))
    cgrp = lax.broadcasted_iota(jnp.int32, (SG_CHUNK, w), 1) // SG_GW
    ys = []
    for c in range(ta // SG_CHUNK):
        vc = vn[c * SG_CHUNK:(c + 1) * SG_CHUNK, :]
        rhs = jnp.concatenate([jnp.where(cgrp == g, vc, 0.0) for g in range(SG_GROUPS)], axis=0).astype(BF16)
        ys.append(jnp.dot(ws, rhs, preferred_element_type=F32) + sgb_ref[...])
    y_d = u * jnp.concatenate(ys, axis=0)
    yloc_ref[:, 2 * w:3 * w] = y_d.astype(BF16)

    xb = x_ref[...].astype(BF16)
    for ref, col in ((ppool_ref, COL_POOL), (pconv_ref, COL_CONV), (pattn_ref, COL_ATTN), (psg_ref, COL_SG)):
        ref[...] = jnp.dot(xb, w_ref[:, col:col + ref.shape[1]], preferred_element_type=F32)


def _local_mixer(x, w_loc, cos_t, sin_t, pool_bd, pool_scale, conv_w, sg_ln_g, sg_ln_b, sg_wcat, sg_bias, *, ta):
    b, s, d = x.shape
    w = BRANCH_W
    ncol = w_loc.shape[1]
    tps = s // ta
    nt = b * tps
    cur = lambda i: jnp.minimum(i, nt - 1)
    prev = lambda i: jnp.maximum(i - 1, 0)
    full = lambda shape: pl.BlockSpec(shape, lambda i: (0,) * len(shape))
    out_blk = lambda width: pl.BlockSpec((None, ta, width), lambda i: (prev(i) // tps, prev(i) % tps, 0))
    out_blk_t = pl.BlockSpec((None, w, ta), lambda i: (prev(i) // tps, 0, prev(i) % tps))
    return pl.pallas_call(
        functools.partial(_local_mixer_kernel, tiles_per_seq=tps),
        grid=(nt + 1,),
        in_specs=[
            pl.BlockSpec((None, ta, d), lambda i: (cur(i) // tps, cur(i) % tps, 0)),
            full((d, ncol)),
            pl.BlockSpec((ta, w), lambda i: (prev(i) % tps, 0)),
            pl.BlockSpec((ta, w), lambda i: (prev(i) % tps, 0)),
            full((w, w)), full((1, w)), full((CONV_W, w)), full((1, w)), full((1, w)),
            full((SG_CHUNK, SG_GROUPS * SG_CHUNK)), full((SG_CHUNK, w)),
        ],
        out_specs=[out_blk(3 * w), out_blk_t, out_blk(w), out_blk_t],
        out_shape=[
            jax.ShapeDtypeStruct((b, s, 3 * w), BF16),
            jax.ShapeDtypeStruct((b, w, s), BF16),
            jax.ShapeDtypeStruct((b, s, w), BF16),
            jax.ShapeDtypeStruct((b, w, s), BF16),
        ],
        scratch_shapes=[pltpu.VMEM((ta + MAX_POOL, w), F32), pltpu.VMEM((ta + 8, w), F32),
                        pltpu.VMEM((ta, w), F32), pltpu.VMEM((ta, 3 * w), F32), pltpu.VMEM((ta, 3 * w), F32),
                        pltpu.VMEM((ta, 2 * w), F32)],
        compiler_params=pltpu.CompilerParams(dimension_semantics=("arbitrary",),
                                             vmem_limit_bytes=V7X_VMEM_LIMIT),
        name="local_mixer",
    )(x, w_loc, cos_t, sin_t, pool_bd, pool_scale, conv_w, sg_ln_g, sg_ln_b, sg_wcat, sg_bias)


ATTN_ONES_ROWS = 16
ATTN_LANE_CHUNK = 256


def _diff_attn_kernel(lam_ref, qt_ref, k_ref, vt_ref, g_ref, o_ref, q2_ref, s0_ref, s1_ref, cm0_ref, cm1_ref,
                      m_ref, acc_ref, *, lam_init):
    qi = pl.program_id(1)
    tq = qt_ref.shape[1]
    tk = tq
    w = BRANCH_W

    qt = qt_ref[...]
    sub = lax.broadcasted_iota(jnp.int32, (w, tq), 0) // DA_HEAD_DIM
    zero = jnp.zeros((), BF16)
    for h in range(DA_HEADS):
        q2_ref[h, :, 0:tq] = jnp.where(sub == 2 * h, qt, zero)
        q2_ref[h, :, tq:2 * tq] = jnp.where(sub == 2 * h + 1, qt, zero)

    lam_rows = lam_ref[...]
    lam = (jnp.exp(jnp.sum(lam_rows[0:1, :] * lam_rows[1:2, :], axis=-1, keepdims=True))
           - jnp.exp(jnp.sum(lam_rows[2:3, :] * lam_rows[3:4, :], axis=-1, keepdims=True)) + lam_init)

    def column_max(sc):
        parts = [sc[r * 8:(r + 1) * 8, :] for r in range(sc.shape[0] // 8)]
        while len(parts) > 1:
            parts = [jnp.maximum(parts[2 * r], parts[2 * r + 1]) for r in range(len(parts) // 2)]
        return jnp.max(parts[0], axis=0, keepdims=True)

    bufs = ((s0_ref, cm0_ref), (s1_ref, cm1_ref))

    def first_query(c):
        return (c * ATTN_LANE_CHUNK) % tq

    def keys_seen(c, diagonal):
        return first_query(c) + ATTN_LANE_CHUNK if diagonal else tk

    def causal(c):
        shape = (keys_seen(c, True), ATTN_LANE_CHUNK)
        return lax.broadcasted_iota(jnp.int32, shape, 0) <= lax.broadcasted_iota(jnp.int32, shape, 1) + first_query(c)

    def scores(h, kv, dst, diagonal=False):
        dst_ref, dst_max_ref = dst
        chans = slice((h // 2) * 2 * DA_V_DIM, (h // 2 + 1) * 2 * DA_V_DIM)
        kt = k_ref[pl.ds(pl.multiple_of(kv * tk, tk), tk), chans]
        for c in range(2 * tq // ATTN_LANE_CHUNK):
            cols = slice(c * ATTN_LANE_CHUNK, (c + 1) * ATTN_LANE_CHUNK)
            nk = keys_seen(c, diagonal)
            sc = jnp.dot(kt[0:nk, :], q2_ref[h, chans, cols], preferred_element_type=F32)
            if diagonal:
                sc = jnp.where(causal(c), sc, NEG_INF)
            dst_ref[0:nk, cols] = sc
            dst_max_ref[:, cols] = column_max(sc)

    def mask_diagonal(cur):
        cur_ref, cur_max_ref = cur
        for c in range(2 * tq // ATTN_LANE_CHUNK):
            cols = slice(c * ATTN_LANE_CHUNK, (c + 1) * ATTN_LANE_CHUNK)
            nk = keys_seen(c, True)
            sc = jnp.where(causal(c), cur_ref[0:nk, cols], NEG_INF)
            cur_ref[0:nk, cols] = sc
            cur_max_ref[:, cols] = column_max(sc)

    def softmax_pv(h, kv, cur, diagonal=False):
        cur_ref, cur_max_ref = cur
        vth = vt_ref[h * DA_V_DIM:(h + 1) * DA_V_DIM, pl.ds(pl.multiple_of(kv * tk, tk), tk)]
        lhs = jnp.concatenate([vth, jnp.ones((ATTN_ONES_ROWS, tk), BF16)], axis=0)
        for c in range(2 * tq // ATTN_LANE_CHUNK):
            cols = slice(c * ATTN_LANE_CHUNK, (c + 1) * ATTN_LANE_CHUNK)
            nk = keys_seen(c, diagonal)
            sc = cur_ref[0:nk, cols]
            m_old = m_ref[h, :, cols]
            m_new = jnp.maximum(m_old, cur_max_ref[:, cols])
            e = jnp.exp2(sc - m_new).astype(BF16)
            pv = jnp.dot(lhs[:, 0:nk], e, preferred_element_type=F32)
            acc_ref[h, :, cols] = acc_ref[h, :, cols] * jnp.exp2(m_old - m_new) + pv
            m_ref[h, :, cols] = m_new

    def finalize(h):
        o1 = acc_ref[h, 0:DA_V_DIM, 0:tq]
        o2 = acc_ref[h, 0:DA_V_DIM, tq:2 * tq]
        r1 = 1.0 / acc_ref[h, DA_V_DIM:DA_V_DIM + 1, 0:tq]
        r2 = 1.0 / acc_ref[h, DA_V_DIM:DA_V_DIM + 1, tq:2 * tq]
        a = o1 * r1 - lam * (o2 * r2)
        ms = jnp.mean(a * a, axis=0, keepdims=True)
        y = (a * lax.rsqrt(ms + LN_EPS)) * (1.0 - lam_init) * g_ref[...]
        o_ref[h * DA_V_DIM:(h + 1) * DA_V_DIM, :] = y.astype(BF16)

    m_ref[...] = jnp.full(m_ref.shape, NEG_INF, F32)
    acc_ref[...] = jnp.zeros(acc_ref.shape, F32)
    scores(0, 0, bufs[0])

    def full_tile(kv, carry):
        for h in range(DA_HEADS):
            nxt_h, nxt_kv = (h + 1, kv) if h + 1 < DA_HEADS else (0, kv + 1)
            scores(nxt_h, nxt_kv, bufs[(h + 1) % 2])
            softmax_pv(h, kv, bufs[h % 2])
        return carry

    lax.fori_loop(0, qi, full_tile, 0)

    mask_diagonal(bufs[0])
    for h in range(DA_HEADS):
        if h + 1 < DA_HEADS:
            scores(h + 1, qi, bufs[(h + 1) % 2], diagonal=True)
        softmax_pv(h, qi, bufs[h % 2], diagonal=True)
        finalize(h)


def _diff_attn(lam_rows, qt, k, vt, subln_cols, *, lam_init, tq):
    b, s, w = k.shape
    return pl.pallas_call(
        functools.partial(_diff_attn_kernel, lam_init=lam_init),
        grid=(b, s // tq),
        in_specs=[
            pl.BlockSpec((8, 128), lambda i, j: (0, 0)),
            pl.BlockSpec((None, w, tq), lambda i, j: (i, 0, j)),
            pl.BlockSpec((None, s, w), lambda i, j: (i, 0, 0)),
            pl.BlockSpec((None, w, s), lambda i, j: (i, 0, 0)),
            pl.BlockSpec((DA_V_DIM, tq), lambda i, j: (0, 0)),
        ],
        out_specs=pl.BlockSpec((None, w, tq), lambda i, j: (i, 0, j)),
        out_shape=jax.ShapeDtypeStruct((b, w, s), BF16),
        scratch_shapes=[pltpu.VMEM((DA_HEADS, w, 2 * tq), BF16),
                        pltpu.VMEM((tq, 2 * tq), F32), pltpu.VMEM((tq, 2 * tq), F32),
                        pltpu.VMEM((1, 2 * tq), F32), pltpu.VMEM((1, 2 * tq), F32),
                        pltpu.VMEM((DA_HEADS, 1, 2 * tq), F32),
                        pltpu.VMEM((DA_HEADS, DA_V_DIM + ATTN_ONES_ROWS, 2 * tq), F32)],
        compiler_params=pltpu.CompilerParams(dimension_semantics=("parallel", "parallel"),
                                             vmem_limit_bytes=V7X_VMEM_LIMIT),
        name="diff_attn",
    )(lam_rows, qt, k, vt, subln_cols)


def _route(x, wr2, bias):
    rows = x.shape[0]
    xh = x.astype(BF16)
    xl = (x - xh.astype(F32)).astype(BF16)
    prod = jnp.dot(jnp.concatenate([xh, xl], axis=1), wr2, preferred_element_type=F32)
    logits = prod[:, 0:ROUTER_LANES] + prod[:, ROUTER_LANES:2 * ROUTER_LANES] + bias
    lt = logits.T

    def first_argmax(vals):
        best = vals[0]
        for v in vals[1:]:
            best = jnp.maximum(best, v)
        idx = jnp.full(best.shape, len(vals) - 1, jnp.int32)
        for i in range(len(vals) - 2, -1, -1):
            idx = jnp.where(vals[i] == best, i, idx)
        return best, idx

    gl = [lt[c:c + 1, :] for c in range(N_GROUPS)]
    gmax, g_sel = first_argmax(gl)
    denom = jnp.exp(gl[0] - gmax)
    for c in range(1, N_GROUPS):
        denom = denom + jnp.exp(gl[c] - gmax)
    p_sel = 1.0 / denom
    el = []
    for e in range(EXPERTS_PER_GROUP):
        row = N_GROUPS + EXPERTS_PER_GROUP * (N_GROUPS - 1) + e
        v = lt[row:row + 1, :]
        for c in range(N_GROUPS - 2, -1, -1):
            row = N_GROUPS + EXPERTS_PER_GROUP * c + e
            v = jnp.where(g_sel == c, lt[row:row + 1, :], v)
        el.append(v)
    v1, i1 = first_argmax(el)
    v2, i2 = first_argmax([jnp.where(i1 == e, NEG_INF, el[e]) for e in range(EXPERTS_PER_GROUP)])
    e2 = jnp.exp(v2 - v1)
    w1 = p_sel / (1.0 + e2)
    w2 = w1 * e2
    gates = [jnp.where(i1 == e, w1, jnp.where(i2 == e, w2, 0.0)) for e in range(EXPERTS_PER_GROUP)]
    hi = [g.astype(BF16).astype(F32) for g in gates]
    lo = [g - h for g, h in zip(gates, hi)]
    pad = jnp.zeros((ROUTER_LANES - 2 * EXPERTS_PER_GROUP, rows), F32)
    aux = jnp.concatenate(hi + lo + [pad], axis=0).T
    return aux, g_sel


def _merge_kernel(x_ref, yloc_ref, yct_ref, wgate_ref, wbr_ref, wo_ref, g_ref, b_ref, wr2_ref, rb_ref,
                  o_ref, xa_ref, gid_ref, h_ref, *, alpha):
    w = BRANCH_W
    tc, d = x_ref.shape

    @pl.when(pl.program_id(0) == 0)
    def _():
        h_ref[...] = jnp.zeros(h_ref.shape, F32)

    x1 = _layer_norm(h_ref[...], g_ref[...], b_ref[...])
    o_ref[...] = x1
    aux, g_row = _route(x1, wr2_ref[...], rb_ref[...])
    xa_ref[:, 0:d] = x1.astype(BF16)
    xa_ref[:, d:d + AUX_LANES] = aux.astype(BF16)
    gid_ref[...] = g_row

    x = x_ref[...]
    xb = x.astype(BF16)
    y_c = yct_ref[...].astype(F32).T.astype(BF16)
    branches = (yloc_ref[:, 0:w], yloc_ref[:, w:2 * w], y_c, yloc_ref[:, 2 * w:3 * w])
    merged = None
    for i in range(N_BRANCH):
        gate = jax.nn.sigmoid(jnp.dot(xb, wgate_ref[:, i * d:(i + 1) * d], preferred_element_type=F32))
        term = gate * jnp.dot(branches[i], wbr_ref[i], preferred_element_type=F32)
        merged = term if merged is None else merged + term
    mix = jnp.dot(merged.astype(BF16), wo_ref[...], preferred_element_type=F32)
    h_ref[...] = alpha * x + mix


def _merge(x, yloc, yct, w_gate_in, w_branch, w_o, ln_g, ln_b, wr2, r_bias, *, alpha, tc):
    n, d = x.shape
    w = BRANCH_W
    s = yct.shape[2]
    tiles_per_row = s // tc
    nt = n // tc
    cur = lambda i: jnp.minimum(i, nt - 1)
    prev = lambda i: jnp.maximum(i - 1, 0)
    tok_in = lambda width: pl.BlockSpec((tc, width), lambda i: (cur(i), 0))
    tok_out = lambda width: pl.BlockSpec((tc, width), lambda i: (prev(i), 0))
    const = lambda shape: pl.BlockSpec(shape, lambda i: (0,) * len(shape))
    return pl.pallas_call(
        functools.partial(_merge_kernel, alpha=alpha),
        grid=(nt + 1,),
        in_specs=[
            tok_in(d), tok_in(3 * w),
            pl.BlockSpec((None, w, tc), lambda i: (cur(i) // tiles_per_row, 0, cur(i) % tiles_per_row)),
            const((d, N_BRANCH * d)), const((N_BRANCH, w, d)), const((d, d)), const((1, d)), const((1, d)),
            const((2 * d, 2 * ROUTER_LANES)), const((1, ROUTER_LANES)),
        ],
        out_specs=[tok_out(d), tok_out(d + AUX_LANES), pl.BlockSpec((None, 1, tc), lambda i: (prev(i), 0, 0))],
        out_shape=[jax.ShapeDtypeStruct((n, d), F32), jax.ShapeDtypeStruct((n, d + AUX_LANES), BF16),
                   jax.ShapeDtypeStruct((n // tc, 1, tc), jnp.int32)],
        scratch_shapes=[pltpu.VMEM((tc, d), F32)],
        compiler_params=pltpu.CompilerParams(dimension_semantics=("arbitrary",),
                                             vmem_limit_bytes=V7X_VMEM_LIMIT),
        name="gated_merge",
    )(x, yloc, yct, w_gate_in, w_branch, w_o, ln_g, ln_b, wr2, r_bias)


DISPATCH_TILE = 512
DISPATCH_CHUNK = 16
DISPATCH_ROWS = DISPATCH_TILE + N_GROUPS * DISPATCH_CHUNK
COMBINE_ROWS = -(-DISPATCH_ROWS // 128) * 128
AUX_LANES = 128
MOE_TILE = 512
TAB_DEST, TAB_LOFF, TAB_NCH = 0, 1, 2


def _dispatch_tables(gid, n, tm):
    t, ch = DISPATCH_TILE, DISPATCH_CHUNK
    nt = n // t
    g = gid.reshape(nt, t)
    counts = jnp.sum((g[:, :, None] == jnp.arange(N_GROUPS, dtype=jnp.int32)).astype(jnp.int32), axis=1)
    nch = (counts + ch - 1) // ch
    rows = nch * ch
    loff = jnp.cumsum(rows, axis=1) - rows
    coff = jnp.cumsum(rows, axis=0) - rows
    gsize = jnp.sum(rows, axis=0)
    gpad = (gsize + tm - 1) // tm * tm
    gstart = jnp.cumsum(gpad) - gpad
    dest = gstart[None, :] + coff
    tab = jnp.concatenate([dest.reshape(-1), loff.reshape(-1), nch.reshape(-1)]).astype(jnp.int32)
    onehot = (g[:, :, None] == jnp.arange(N_GROUPS, dtype=jnp.int32)).astype(jnp.int32)
    rank = jnp.cumsum(onehot, axis=1) - onehot
    pos = jnp.sum(onehot * (rank + loff[:, None, :]), axis=2).astype(jnp.int32).reshape(nt, 1, t)
    cap = -(-(n + nt * N_GROUPS * ch + N_GROUPS * tm) // tm) * tm
    used = jnp.sum(gpad)
    gtab = jnp.concatenate([gstart + gsize, (gpad - gsize) // ch, used[None], (cap - used)[None] // tm])
    gtab = gtab.astype(jnp.int32)
    tile = jnp.arange(cap // tm, dtype=jnp.int32)
    tgroup = jnp.minimum(jnp.sum((tile[:, None] * tm >= (gstart + gpad)[None, :]).astype(jnp.int32), axis=1),
                         N_GROUPS - 1)
    tsrc = jnp.minimum(tile, jnp.sum(gpad) // tm - 1)
    return tab, gtab, pos, tgroup, tsrc, cap


def _tab(tab_ref, section, tile, grp):
    n_entries = tab_ref.shape[0] // 3
    return tab_ref[section * n_entries + tile * N_GROUPS + grp]


def _n_chunks(tab_ref, tile):
    total = _tab(tab_ref, TAB_NCH, tile, 0)
    for c in range(1, N_GROUPS):
        total = total + _tab(tab_ref, TAB_NCH, tile, c)
    return total


def _permutation(pos_row, n_rows):
    rows = lax.broadcasted_iota(jnp.int32, (n_rows, pos_row.shape[1]), 0)
    return jnp.where(rows == pos_row, 1.0, 0.0)


def _dispatch_kernel(tab_ref, gtab_ref, pos_ref, xa_ref, xs_ref, buf_ref, zero_ref, sem_ref):
    i = pl.program_id(0)
    nt = pl.num_programs(0)
    slot = lax.rem(i, 2)
    ch = DISPATCH_CHUNK

    def chunk_copy(slot_, src_row, dst_row):
        return pltpu.make_async_copy(buf_ref.at[slot_, pl.ds(src_row, ch), :], xs_ref.at[pl.ds(dst_row, ch), :],
                                     sem_ref.at[slot_])

    def wait_tile(tile, slot_):
        def one(k, carry):
            chunk_copy(slot_, 0, 0).wait()
            return carry
        lax.fori_loop(0, _n_chunks(tab_ref, tile), one, 0)

    @pl.when(i >= 2)
    def _():
        wait_tile(i - 2, slot)

    perm = _permutation(pos_ref[...], DISPATCH_ROWS).astype(BF16)
    buf_ref[slot] = jnp.dot(perm, xa_ref[...], preferred_element_type=F32).astype(BF16)
    for c in range(N_GROUPS):
        src0 = _tab(tab_ref, TAB_LOFF, i, c)
        dst0 = _tab(tab_ref, TAB_DEST, i, c)

        def issue(k, carry, src0=src0, dst0=dst0):
            chunk_copy(slot, pl.multiple_of(src0 + k * ch, ch), pl.multiple_of(dst0 + k * ch, ch)).start()
            return carry
        lax.fori_loop(0, _tab(tab_ref, TAB_NCH, i, c), issue, 0)

    @pl.when(i == nt - 1)
    def _():
        zero_ref[...] = jnp.zeros(zero_ref.shape, BF16)

        def tail_copy(dst_row):
            return pltpu.make_async_copy(zero_ref.at[pl.ds(0, ch), :], xs_ref.at[pl.ds(dst_row, ch), :],
                                         sem_ref.at[2])

        tile_rows = zero_ref.shape[0]

        def spare_copy(dst_row):
            return pltpu.make_async_copy(zero_ref, xs_ref.at[pl.ds(dst_row, tile_rows), :], sem_ref.at[3])

        def fill_spare(k, carry):
            spare_copy(pl.multiple_of(gtab_ref[2 * N_GROUPS] + k * tile_rows, tile_rows)).start()
            return carry
        lax.fori_loop(0, gtab_ref[2 * N_GROUPS + 1], fill_spare, 0)

        def drain_spare(k, carry):
            spare_copy(0).wait()
            return carry
        lax.fori_loop(0, gtab_ref[2 * N_GROUPS + 1], drain_spare, 0)

        n_tail = 0
        for c in range(N_GROUPS):
            dst0 = gtab_ref[c]

            def fill(k, carry, dst0=dst0):
                tail_copy(pl.multiple_of(dst0 + k * ch, ch)).start()
                return carry
            lax.fori_loop(0, gtab_ref[N_GROUPS + c], fill, 0)
            n_tail = n_tail + gtab_ref[N_GROUPS + c]

        def drain(k, carry):
            tail_copy(0).wait()
            return carry
        lax.fori_loop(0, n_tail, drain, 0)

        wait_tile(i, slot)

        @pl.when(i >= 1)
        def _():
            wait_tile(i - 1, 1 - slot)


def _dispatch(tab, gtab, pos, xa, cap):
    n, width = xa.shape
    t = DISPATCH_TILE
    grid_spec = pltpu.PrefetchScalarGridSpec(
        num_scalar_prefetch=2,
        grid=(n // t,),
        in_specs=[
            pl.BlockSpec((None, 1, t), lambda i, tab_ref, gtab_ref: (i, 0, 0)),
            pl.BlockSpec((t, width), lambda i, tab_ref, gtab_ref: (i, 0)),
        ],
        out_specs=pl.BlockSpec(memory_space=pl.ANY),
        scratch_shapes=[pltpu.VMEM((2, DISPATCH_ROWS, width), BF16), pltpu.VMEM((MOE_TILE, width), BF16),
                        pltpu.SemaphoreType.DMA((4,))],
    )
    return pl.pallas_call(
        _dispatch_kernel,
        grid_spec=grid_spec,
        out_shape=jax.ShapeDtypeStruct((cap, width), BF16),
        compiler_params=pltpu.CompilerParams(dimension_semantics=("arbitrary",),
                                             vmem_limit_bytes=V7X_VMEM_LIMIT),
        name="moe_dispatch",
    )(tab, gtab, pos, xa)


def _expert_kernel(tg_ref, ts_ref, xs_ref, wg_ref, wu_ref, wd_ref, ys_ref):
    del tg_ref
    j = pl.program_id(0)
    tm = xs_ref.shape[0]
    d = wg_ref.shape[1]

    @pl.when(ts_ref[j] != j)
    def _():
        ys_ref[...] = jnp.zeros(ys_ref.shape, BF16)

    @pl.when(ts_ref[j] == j)
    def _():
        xt = xs_ref[:, 0:d]
        aux = xs_ref[:, d:d + AUX_LANES].astype(F32)
        lane = lax.broadcasted_iota(jnp.int32, (tm, AUX_LANES), 1)
        acts = []
        for e in range(EXPERTS_PER_GROUP):
            hg = jnp.dot(xt, wg_ref[e], preferred_element_type=F32)
            hu = jnp.dot(xt, wu_ref[e], preferred_element_type=F32)
            two_terms = (lane == e) | (lane == e + EXPERTS_PER_GROUP)
            ge = jnp.sum(jnp.where(two_terms, aux, 0.0), axis=-1, keepdims=True)
            acts.append(((hg * jax.nn.sigmoid(hg)) * hu * ge).astype(BF16))
        act = jnp.concatenate(acts, axis=1)
        ys_ref[...] = jnp.dot(act, wd_ref[...], preferred_element_type=F32).astype(BF16)


def _experts(tgroup, tsrc, xs, wg, wu, wd, *, tm):
    cap, width = xs.shape
    d = wg.shape[1]
    gh = GROUP_HIDDEN
    epg = EXPERTS_PER_GROUP
    grid_spec = pltpu.PrefetchScalarGridSpec(
        num_scalar_prefetch=2,
        grid=(cap // tm,),
        in_specs=[
            pl.BlockSpec((tm, width), lambda j, tg, ts: (ts[j], 0)),
            pl.BlockSpec((epg, d, EXPERT_HIDDEN), lambda j, tg, ts: (tg[j], 0, 0)),
            pl.BlockSpec((epg, d, EXPERT_HIDDEN), lambda j, tg, ts: (tg[j], 0, 0)),
            pl.BlockSpec((None, gh, d), lambda j, tg, ts: (tg[j], 0, 0)),
        ],
        out_specs=pl.BlockSpec((tm, d), lambda j, tg, ts: (j, 0)),
    )
    return pl.pallas_call(
        _expert_kernel,
        grid_spec=grid_spec,
        out_shape=jax.ShapeDtypeStruct((cap, d), BF16),
        compiler_params=pltpu.CompilerParams(dimension_semantics=("arbitrary",),
                                             vmem_limit_bytes=V7X_VMEM_LIMIT),
        name="moe_experts",
    )(tgroup, tsrc, xs, wg, wu, wd)


def _combine_kernel(tab_ref, pos_ref, x_ref, ys_ref, g_ref, b_ref, o_ref, buf_ref, h_ref, sem_ref, *, alpha):
    step = pl.program_id(0)
    nt = pl.num_programs(0) - 1
    i = jnp.minimum(step, nt - 1)
    slot = lax.rem(i, 2)
    ch = DISPATCH_CHUNK

    def chunk_copy(slot_, src_row, dst_row):
        return pltpu.make_async_copy(ys_ref.at[pl.ds(src_row, ch), :], buf_ref.at[slot_, pl.ds(dst_row, ch), :],
                                     sem_ref.at[slot_])

    def fetch(tile, slot_):
        for c in range(N_GROUPS):
            src0 = _tab(tab_ref, TAB_DEST, tile, c)
            dst0 = _tab(tab_ref, TAB_LOFF, tile, c)

            def issue(k, carry, src0=src0, dst0=dst0):
                chunk_copy(slot_, pl.multiple_of(src0 + k * ch, ch), pl.multiple_of(dst0 + k * ch, ch)).start()
                return carry
            lax.fori_loop(0, _tab(tab_ref, TAB_NCH, tile, c), issue, 0)

    @pl.when(step == 0)
    def _():
        buf_ref[...] = jnp.zeros(buf_ref.shape, BF16)
        h_ref[...] = jnp.zeros(h_ref.shape, F32)
        fetch(0, 0)

    @pl.when(step + 1 < nt)
    def _():
        fetch(step + 1, 1 - slot)

    def one(k, carry):
        chunk_copy(slot, 0, 0).wait()
        return carry
    lax.fori_loop(0, jnp.where(step < nt, _n_chunks(tab_ref, i), 0), one, 0)

    o_ref[...] = _layer_norm(h_ref[...], g_ref[...], b_ref[...])
    perm_t = _permutation(pos_ref[...], COMBINE_ROWS).T.astype(BF16)
    y = jnp.dot(perm_t, buf_ref[slot], preferred_element_type=F32)
    h_ref[...] = alpha * x_ref[...] + y


def _combine(tab, pos, x1, ys, ln_g, ln_b, *, alpha):
    n, d = x1.shape
    t = DISPATCH_TILE
    nt = n // t
    cur = lambda i: jnp.minimum(i, nt - 1)
    prev = lambda i: jnp.maximum(i - 1, 0)
    grid_spec = pltpu.PrefetchScalarGridSpec(
        num_scalar_prefetch=1,
        grid=(nt + 1,),
        in_specs=[
            pl.BlockSpec((None, 1, t), lambda i, tab_ref: (cur(i), 0, 0)),
            pl.BlockSpec((t, d), lambda i, tab_ref: (cur(i), 0)),
            pl.BlockSpec(memory_space=pl.ANY),
            pl.BlockSpec((1, d), lambda i, tab_ref: (0, 0)),
            pl.BlockSpec((1, d), lambda i, tab_ref: (0, 0)),
        ],
        out_specs=pl.BlockSpec((t, d), lambda i, tab_ref: (prev(i), 0)),
        scratch_shapes=[pltpu.VMEM((2, COMBINE_ROWS, d), BF16), pltpu.VMEM((t, d), F32),
                        pltpu.SemaphoreType.DMA((2,))],
    )
    return pl.pallas_call(
        functools.partial(_combine_kernel, alpha=alpha),
        grid_spec=grid_spec,
        out_shape=jax.ShapeDtypeStruct((n, d), F32),
        compiler_params=pltpu.CompilerParams(dimension_semantics=("arbitrary",),
                                             vmem_limit_bytes=V7X_VMEM_LIMIT),
        name="moe_combine",
    )(tab, pos, x1, ys, ln_g, ln_b)


def _block_diag(blocks):
    n, r, c = blocks.shape
    eye = jnp.eye(n, dtype=blocks.dtype)
    return jnp.einsum("grc,gh->grhc", blocks, eye).reshape(n * r, n * c)


def kernel(x, w_in, pool_w, pool_scale, conv_w, lam_q1, lam_k1, lam_q2, lam_k2, subln_g, sg_ln_g, sg_ln_b, sg_w, sg_b, w_branch, w_o, ln1_g, ln1_b, w_rg, b_rg, w_re, b_re, w_gate, w_up, w_down, ln2_g, ln2_b):
    b, s, d = x.shape
    depth = w_in.shape[0]
    n = b * s
    w = BRANCH_W
    alpha = (2 * depth) ** 0.25
    ta = min(512, s)
    tq = min(512, s)
    tc = min(512, s)
    tm = MOE_TILE

    half = DA_HEAD_DIM // 2
    inv_freq = ROPE_THETA ** (-jnp.arange(half, dtype=F32) / half)
    ang = jnp.arange(s, dtype=F32)[:, None] * inv_freq[None, :]
    cos, sin = jnp.cos(ang), jnp.sin(ang)
    reps = w // DA_HEAD_DIM
    cos_t = jnp.tile(jnp.concatenate([cos, cos], axis=-1), (1, reps))
    sin_t = jnp.tile(jnp.concatenate([-sin, sin], axis=-1), (1, reps))

    for l in range(depth):
        lam_init = 0.8 - 0.6 * math.exp(-0.3 * l)
        w_loc = w_in[l, :, :COL_GATE].astype(BF16)
        w_gate_in = w_in[l, :, COL_GATE:].astype(BF16)
        pool_bd = _block_diag(pool_w[l]).astype(BF16)
        sg_wcat = jnp.transpose(sg_w[l], (1, 0, 2)).reshape(SG_CHUNK, SG_GROUPS * SG_CHUNK).astype(BF16)
        sg_bias = jnp.repeat(sg_b[l].T, SG_GW, axis=1)
        lam_rows = jnp.zeros((8, 128), F32).at[0:4, 0:DA_HEAD_DIM].set(
            jnp.stack([lam_q1[l], lam_k1[l], lam_q2[l], lam_k2[l]]).astype(F32))
        subln_cols = jnp.broadcast_to(subln_g[l][:, None], (DA_V_DIM, tq))
        w_router = jnp.concatenate([w_rg[l], jnp.transpose(w_re[l], (1, 0, 2)).reshape(d, N_EXPERTS)], axis=1)
        w_router = jnp.pad(w_router, ((0, 0), (0, ROUTER_LANES - w_router.shape[1])))
        wr_hi = w_router.astype(BF16)
        wr_lo = (w_router - wr_hi.astype(F32)).astype(BF16)
        wr2 = jnp.concatenate([jnp.concatenate([wr_hi, wr_lo], axis=1),
                               jnp.concatenate([wr_hi, jnp.zeros_like(wr_lo)], axis=1)], axis=0)
        r_bias = jnp.pad(jnp.concatenate([b_rg[l], b_re[l].reshape(-1)]), (0, ROUTER_LANES - N_GROUPS - N_EXPERTS))[None, :]
        wg = w_gate[l].astype(BF16)
        wu = w_up[l].astype(BF16)
        wd = w_down[l].reshape(N_GROUPS, GROUP_HIDDEN, d).astype(BF16)

        yloc, q, k, vt = _local_mixer(x, w_loc, cos_t, sin_t, pool_bd, pool_scale[l][None, :], conv_w[l],
                                      sg_ln_g[l][None, :], sg_ln_b[l][None, :], sg_wcat, sg_bias, ta=ta)
        yct = _diff_attn(lam_rows, q, k, vt, subln_cols, lam_init=lam_init, tq=tq)
        x1, xa, gid = _merge(x.reshape(n, d), yloc.reshape(n, 3 * w), yct, w_gate_in,
                             w_branch[l].astype(BF16), w_o[l].astype(BF16), ln1_g[l][None, :], ln1_b[l][None, :],
                             wr2, r_bias, alpha=alpha, tc=tc)
        tab, gtab, pos, tgroup, tsrc, cap = _dispatch_tables(gid, n, tm)
        xs = _dispatch(tab, gtab, pos, xa, cap)
        ys = _experts(tgroup, tsrc, xs, wg, wu, wd, tm=tm)
        x2 = _combine(tab, pos, x1, ys, ln2_g[l][None, :], ln2_b[l][None, :], alpha=alpha)
        x = x2.reshape(b, s, d)
    return x
```

```python
import functools
import math

import jax
import jax.numpy as jnp
from jax import lax
from jax.experimental import pallas as pl
from jax.experimental.pallas import tpu as pltpu

F32 = jnp.float32
BF16 = jnp.bfloat16

BRANCH_W = 256
POOL_WINDOWS = (2, 4, 8, 16)
POOL_GW = 64
MAX_POOL = 16
CONV_W = 3
DA_HEADS = 4
DA_HEAD_DIM = 32
DA_V_DIM = 64
ROPE_THETA = 10000.0
SG_CHUNK = 128
SG_GROUPS = 4
SG_GW = 64
N_GROUPS = 4
EXPERTS_PER_GROUP = 4
N_EXPERTS = 16
EXPERT_HIDDEN = 256
GROUP_HIDDEN = EXPERTS_PER_GROUP * EXPERT_HIDDEN
N_BRANCH = 4
LN_EPS = 1e-5
NEG_INF = -1e30
ROUTER_LANES = 128
V7X_VMEM_LIMIT = 56 * 1024 * 1024

COL_POOL = 0
COL_CONV = BRANCH_W
COL_ATTN = 4 * BRANCH_W
COL_SG = 7 * BRANCH_W
COL_GATE = 9 * BRANCH_W


def _layer_norm(h, g, b):
    mu = jnp.mean(h, axis=-1, keepdims=True)
    hc = h - mu
    var = jnp.mean(hc * hc, axis=-1, keepdims=True)
    return hc * lax.rsqrt(var + LN_EPS) * g + b


def _gelu_tanh(x):
    c = math.sqrt(2.0 / math.pi)
    return 0.5 * x * (1.0 + jnp.tanh(c * (x + 0.044715 * (x * x * x))))


def _local_mixer_kernel(x_ref, w_ref, cos_ref, sin_ref, poolw_ref, pscale_ref, convw_ref, lng_ref, lnb_ref,
                        sgw_ref, sgb_ref, yloc_ref, qt_ref, k_ref, vt_ref, pext_ref, zext_ref,
                        ppool_ref, pconv_ref, pattn_ref, psg_ref, *, tiles_per_seq):
    step = pl.program_id(0)
    t = lax.rem(jnp.maximum(step - 1, 0), tiles_per_seq)
    ta = x_ref.shape[0]
    w = BRANCH_W
    lane = lax.broadcasted_iota(jnp.int32, (ta, w), 1)
    row = lax.broadcasted_iota(jnp.int32, (ta, w), 0)

    @pl.when(step == 0)
    def _():
        for ref in (ppool_ref, pconv_ref, pattn_ref, psg_ref):
            ref[...] = jnp.zeros(ref.shape, F32)

    @pl.when(t == 0)
    def _():
        pext_ref[0:MAX_POOL, :] = jnp.zeros((MAX_POOL, w), F32)
        zext_ref[0:8, :] = jnp.zeros((8, w), F32)

    @pl.when(t > 0)
    def _():
        pext_ref[0:MAX_POOL, :] = pext_ref[ta:ta + MAX_POOL, :]
        zext_ref[0:8, :] = zext_ref[ta:ta + 8, :]

    p = ppool_ref[...]
    pext_ref[MAX_POOL:MAX_POOL + ta, :] = p

    def prev(kk):
        return pext_ref[pl.ds(MAX_POOL - kk, ta), :]

    s2 = p + prev(1)
    s4 = s2 + (prev(2) + prev(3))
    s8 = s4 + ((prev(4) + prev(5)) + (prev(6) + prev(7)))
    s16 = s8 + (((prev(8) + prev(9)) + (prev(10) + prev(11))) + ((prev(12) + prev(13)) + (prev(14) + prev(15))))
    grp = lane // POOL_GW
    win_sum = jnp.where(grp == 0, s2, jnp.where(grp == 1, s4, jnp.where(grp == 2, s8, s16)))
    win = jnp.where(grp == 0, 2, jnp.where(grp == 1, 4, jnp.where(grp == 2, 8, 16)))
    count = jnp.minimum(t * ta + row + 1, win).astype(F32)
    d = (win_sum / count - p).astype(BF16)
    y_a = jnp.dot(d, poolw_ref[...], preferred_element_type=F32) * pscale_ref[...]
    yloc_ref[:, 0:w] = y_a.astype(BF16)

    pc = pconv_ref[...]
    gb = pc[:, 0:w]
    z = pc[:, w:2 * w] * pc[:, 2 * w:3 * w]
    zext_ref[8:8 + ta, :] = z
    cw = convw_ref[...]
    y_b = zext_ref[pl.ds(6, ta), :] * cw[0:1, :] + zext_ref[pl.ds(7, ta), :] * cw[1:2, :] + z * cw[2:3, :]
    yloc_ref[:, w:2 * w] = (gb * y_b).astype(BF16)

    pa = pattn_ref[...]
    cos = cos_ref[...]
    sin = sin_ref[...]
    first_half = (lane % DA_HEAD_DIM) < (DA_HEAD_DIM // 2)

    def rope(u):
        swapped = jnp.where(first_half, pltpu.roll(u, w - DA_HEAD_DIM // 2, axis=1),
                            pltpu.roll(u, DA_HEAD_DIM // 2, axis=1))
        return u * cos + swapped * sin

    qt_ref[...] = (rope(pa[:, 0:w]) * (DA_HEAD_DIM ** -0.5 * math.log2(math.e))).T.astype(BF16)
    k_ref[...] = rope(pa[:, w:2 * w]).astype(BF16)
    vt_ref[...] = pa[:, 2 * w:3 * w].T.astype(BF16)

    uv = _gelu_tanh(psg_ref[...])
    u = uv[:, 0:w]
    vn = _layer_norm(uv[:, w:2 * w], lng_ref[...], lnb_ref[...])
    wrow = lax.broadcasted_iota(jnp.int32, (SG_CHUNK, SG_GROUPS * SG_CHUNK), 0)
    wcol = lax.broadcasted_iota(jnp.int32, (SG_CHUNK, SG_GROUPS * SG_CHUNK), 1)
    ws = jnp.where(wrow >= (wcol % SG_CHUNK), sgw_ref[...], jnp.zeros((), BF16))
    cgrp = lax.broadcasted_iota(jnp.int32, (SG_CHUNK, w), 1) // SG_GW
    ys = []
    for c in range(ta // SG_CHUNK):
        vc = vn[c * SG_CHUNK:(c + 1) * SG_CHUNK, :]
        rhs = jnp.concatenate([jnp.where(cgrp == g, vc, 0.0) for g in range(SG_GROUPS)], axis=0).astype(BF16)
        ys.append(jnp.dot(ws, rhs, preferred_element_type=F32) + sgb_ref[...])
    y_d = u * jnp.concatenate(ys, axis=0)
    yloc_ref[:, 2 * w:3 * w] = y_d.astype(BF16)

    xb = x_ref[...].astype(BF16)
    for ref, col in ((ppool_ref, COL_POOL), (pconv_ref, COL_CONV), (pattn_ref, COL_ATTN), (psg_ref, COL_SG)):
        ref[...] = jnp.dot(xb, w_ref[:, col:col + ref.shape[1]], preferred_element_type=F32)


def _local_mixer(x, w_loc, cos_t, sin_t, pool_bd, pool_scale, conv_w, sg_ln_g, sg_ln_b, sg_wcat, sg_bias, *, ta):
    b, s, d = x.shape
    w = BRANCH_W
    ncol = w_loc.shape[1]
    tps = s // ta
    nt = b * tps
    cur = lambda i: jnp.minimum(i, nt - 1)
    prev = lambda i: jnp.maximum(i - 1, 0)
    full = lambda shape: pl.BlockSpec(shape, lambda i: (0,) * len(shape))
    out_blk = lambda width: pl.BlockSpec((None, ta, width), lambda i: (prev(i) // tps, prev(i) % tps, 0))
    out_blk_t = pl.BlockSpec((None, w, ta), lambda i: (prev(i) // tps, 0, prev(i) % tps))
    return pl.pallas_call(
        functools.partial(_local_mixer_kernel, tiles_per_seq=tps),
        grid=(nt + 1,),
        in_specs=[
            pl.BlockSpec((None, ta, d), lambda i: (cur(i) // tps, cur(i) % tps, 0)),
            full((d, ncol)),
            pl.BlockSpec((ta, w), lambda i: (prev(i) % tps, 0)),
            pl.BlockSpec((ta, w), lambda i: (prev(i) % tps, 0)),
            full((w, w)), full((1, w)), full((CONV_W, w)), full((1, w)), full((1, w)),
            full((SG_CHUNK, SG_GROUPS * SG_CHUNK)), full((SG_CHUNK, w)),
        ],
        out_specs=[out_blk(3 * w), out_blk_t, out_blk(w), out_blk_t],
        out_shape=[
            jax.ShapeDtypeStruct((b, s, 3 * w), BF16),
            jax.ShapeDtypeStruct((b, w, s), BF16),
            jax.ShapeDtypeStruct((b, s, w), BF16),
            jax.ShapeDtypeStruct((b, w, s), BF16),
        ],
        scratch_shapes=[pltpu.VMEM((ta + MAX_POOL, w), F32), pltpu.VMEM((ta + 8, w), F32),
                        pltpu.VMEM((ta, w), F32), pltpu.VMEM((ta, 3 * w), F32), pltpu.VMEM((ta, 3 * w), F32),
                        pltpu.VMEM((ta, 2 * w), F32)],
        compiler_params=pltpu.CompilerParams(dimension_semantics=("arbitrary",),
                                             vmem_limit_bytes=V7X_VMEM_LIMIT),
        name="local_mixer",
    )(x, w_loc, cos_t, sin_t, pool_bd, pool_scale, conv_w, sg_ln_g, sg_ln_b, sg_wcat, sg_bias)


ATTN_ONES_ROWS = 16
ATTN_LANE_CHUNK = 256


def _diff_attn_kernel(lam_ref, qt_ref, k_ref, vt_ref, g_ref, o_ref, q2_ref, s0_ref, s1_ref, cm0_ref, cm1_ref,
                      m_ref, acc_ref, *, lam_init):
    qi = pl.program_id(1)
    tq = qt_ref.shape[1]
    tk = tq
    w = BRANCH_W

    qt = qt_ref[...]
    sub = lax.broadcasted_iota(jnp.int32, (w, tq), 0) // DA_HEAD_DIM
    zero = jnp.zeros((), BF16)
    for h in range(DA_HEADS):
        q2_ref[h, :, 0:tq] = jnp.where(sub == 2 * h, qt, zero)
        q2_ref[h, :, tq:2 * tq] = jnp.where(sub == 2 * h + 1, qt, zero)

    lam_rows = lam_ref[...]
    lam = (jnp.exp(jnp.sum(lam_rows[0:1, :] * lam_rows[1:2, :], axis=-1, keepdims=True))
           - jnp.exp(jnp.sum(lam_rows[2:3, :] * lam_rows[3:4, :], axis=-1, keepdims=True)) + lam_init)

    def column_max(sc):
        parts = [sc[r * 8:(r + 1) * 8, :] for r in range(sc.shape[0] // 8)]
        while len(parts) > 1:
            parts = [jnp.maximum(parts[2 * r], parts[2 * r + 1]) for r in range(len(parts) // 2)]
        return jnp.max(parts[0], axis=0, keepdims=True)

    bufs = ((s0_ref, cm0_ref), (s1_ref, cm1_ref))

    def first_query(c):
        return (c * ATTN_LANE_CHUNK) % tq

    def keys_seen(c, diagonal):
        return first_query(c) + ATTN_LANE_CHUNK if diagonal else tk

    def causal(c):
        shape = (keys_seen(c, True), ATTN_LANE_CHUNK)
        return lax.broadcasted_iota(jnp.int32, shape, 0) <= lax.broadcasted_iota(jnp.int32, shape, 1) + first_query(c)

    def scores(h, kv, dst, diagonal=False):
        dst_ref, dst_max_ref = dst
        chans = slice((h // 2) * 2 * DA_V_DIM, (h // 2 + 1) * 2 * DA_V_DIM)
        kt = k_ref[pl.ds(pl.multiple_of(kv * tk, tk), tk), chans]
        for c in range(2 * tq // ATTN_LANE_CHUNK):
            cols = slice(c * ATTN_LANE_CHUNK, (c + 1) * ATTN_LANE_CHUNK)
            nk = keys_seen(c, diagonal)
            sc = jnp.dot(kt[0:nk, :], q2_ref[h, chans, cols], preferred_element_type=F32)
            if diagonal:
                sc = jnp.where(causal(c), sc, NEG_INF)
            dst_ref[0:nk, cols] = sc
            dst_max_ref[:, cols] = column_max(sc)

    def mask_diagonal(cur):
        cur_ref, cur_max_ref = cur
        for c in range(2 * tq // ATTN_LANE_CHUNK):
            cols = slice(c * ATTN_LANE_CHUNK, (c + 1) * ATTN_LANE_CHUNK)
            nk = keys_seen(c, True)
            sc = jnp.where(causal(c), cur_ref[0:nk, cols], NEG_INF)
            cur_ref[0:nk, cols] = sc
            cur_max_ref[:, cols] = column_max(sc)

    def softmax_pv(h, kv, cur, diagonal=False):
        cur_ref, cur_max_ref = cur
        vth = vt_ref[h * DA_V_DIM:(h + 1) * DA_V_DIM, pl.ds(pl.multiple_of(kv * tk, tk), tk)]
        lhs = jnp.concatenate([vth, jnp.ones((ATTN_ONES_ROWS, tk), BF16)], axis=0)
        for c in range(2 * tq // ATTN_LANE_CHUNK):
            cols = slice(c * ATTN_LANE_CHUNK, (c + 1) * ATTN_LANE_CHUNK)
            nk = keys_seen(c, diagonal)
            sc = cur_ref[0:nk, cols]
            m_old = m_ref[h, :, cols]
            m_new = jnp.maximum(m_old, cur_max_ref[:, cols])
            e = jnp.exp2(sc - m_new).astype(BF16)
            pv = jnp.dot(lhs[:, 0:nk], e, preferred_element_type=F32)
            acc_ref[h, :, cols] = acc_ref[h, :, cols] * jnp.exp2(m_old - m_new) + pv
            m_ref[h, :, cols] = m_new

    def finalize(h):
        o1 = acc_ref[h, 0:DA_V_DIM, 0:tq]
        o2 = acc_ref[h, 0:DA_V_DIM, tq:2 * tq]
        r1 = 1.0 / acc_ref[h, DA_V_DIM:DA_V_DIM + 1, 0:tq]
        r2 = 1.0 / acc_ref[h, DA_V_DIM:DA_V_DIM + 1, tq:2 * tq]
        a = o1 * r1 - lam * (o2 * r2)
        ms = jnp.mean(a * a, axis=0, keepdims=True)
        y = (a * lax.rsqrt(ms + LN_EPS)) * (1.0 - lam_init) * g_ref[...]
        o_ref[h * DA_V_DIM:(h + 1) * DA_V_DIM, :] = y.astype(BF16)

    m_ref[...] = jnp.full(m_ref.shape, NEG_INF, F32)
    acc_ref[...] = jnp.zeros(acc_ref.shape, F32)
    scores(0, 0, bufs[0])

    def full_tile(kv, carry):
        for h in range(DA_HEADS):
            nxt_h, nxt_kv = (h + 1, kv) if h + 1 < DA_HEADS else (0, kv + 1)
            scores(nxt_h, nxt_kv, bufs[(h + 1) % 2])
            softmax_pv(h, kv, bufs[h % 2])
        return carry

    lax.fori_loop(0, qi, full_tile, 0)

    mask_diagonal(bufs[0])
    for h in range(DA_HEADS):
        if h + 1 < DA_HEADS:
            scores(h + 1, qi, bufs[(h + 1) % 2], diagonal=True)
        softmax_pv(h, qi, bufs[h % 2], diagonal=True)
        finalize(h)


def _diff_attn(lam_rows, qt, k, vt, subln_cols, *, lam_init, tq):
    b, s, w = k.shape
    return pl.pallas_call(
        functools.partial(_diff_attn_kernel, lam_init=lam_init),
        grid=(b, s // tq),
        in_specs=[
            pl.BlockSpec((8, 128), lambda i, j: (0, 0)),
            pl.BlockSpec((None, w, tq), lambda i, j: (i, 0, j)),
            pl.BlockSpec((None, s, w), lambda i, j: (i, 0, 0)),
            pl.BlockSpec((None, w, s), lambda i, j: (i, 0, 0)),
            pl.BlockSpec((DA_V_DIM, tq), lambda i, j: (0, 0)),
        ],
        out_specs=pl.BlockSpec((None, w, tq), lambda i, j: (i, 0, j)),
        out_shape=jax.ShapeDtypeStruct((b, w, s), BF16),
        scratch_shapes=[pltpu.VMEM((DA_HEADS, w, 2 * tq), BF16),
                        pltpu.VMEM((tq, 2 * tq), F32), pltpu.VMEM((tq, 2 * tq), F32),
                        pltpu.VMEM((1, 2 * tq), F32), pltpu.VMEM((1, 2 * tq), F32),
                        pltpu.VMEM((DA_HEADS, 1, 2 * tq), F32),
                        pltpu.VMEM((DA_HEADS, DA_V_DIM + ATTN_ONES_ROWS, 2 * tq), F32)],
        compiler_params=pltpu.CompilerParams(dimension_semantics=("parallel", "parallel"),
                                             vmem_limit_bytes=V7X_VMEM_LIMIT),
        name="diff_attn",
    )(lam_rows, qt, k, vt, subln_cols)


def _route(x, wr2, bias):
    rows = x.shape[0]
    xh = x.astype(BF16)
    xl = (x - xh.astype(F32)).astype(BF16)
    prod = jnp.dot(jnp.concatenate([xh, xl], axis=1), wr2, preferred_element_type=F32)
    logits = prod[:, 0:ROUTER_LANES] + prod[:, ROUTER_LANES:2 * ROUTER_LANES] + bias
    lt = logits.T

    def first_argmax(vals):
        best = vals[0]
        for v in vals[1:]:
            best = jnp.maximum(best, v)
        idx = jnp.full(best.shape, len(vals) - 1, jnp.int32)
        for i in range(len(vals) - 2, -1, -1):
            idx = jnp.where(vals[i] == best, i, idx)
        return best, idx

    gl = [lt[c:c + 1, :] for c in range(N_GROUPS)]
    gmax, g_sel = first_argmax(gl)
    denom = jnp.exp(gl[0] - gmax)
    for c in range(1, N_GROUPS):
        denom = denom + jnp.exp(gl[c] - gmax)
    p_sel = 1.0 / denom
    el = []
    for e in range(EXPERTS_PER_GROUP):
        row = N_GROUPS + EXPERTS_PER_GROUP * (N_GROUPS - 1) + e
        v = lt[row:row + 1, :]
        for c in range(N_GROUPS - 2, -1, -1):
            row = N_GROUPS + EXPERTS_PER_GROUP * c + e
            v = jnp.where(g_sel == c, lt[row:row + 1, :], v)
        el.append(v)
    v1, i1 = first_argmax(el)
    v2, i2 = first_argmax([jnp.where(i1 == e, NEG_INF, el[e]) for e in range(EXPERTS_PER_GROUP)])
    e2 = jnp.exp(v2 - v1)
    w1 = p_sel / (1.0 + e2)
    w2 = w1 * e2
    gates = [jnp.where(i1 == e, w1, jnp.where(i2 == e, w2, 0.0)) for e in range(EXPERTS_PER_GROUP)]
    hi = [g.astype(BF16).astype(F32) for g in gates]
    lo = [g - h for g, h in zip(gates, hi)]
    pad = jnp.zeros((ROUTER_LANES - 2 * EXPERTS_PER_GROUP, rows), F32)
    aux = jnp.concatenate(hi + lo + [pad], axis=0).T
    return aux, g_sel


def _merge_kernel(x_ref, yloc_ref, yct_ref, wgate_ref, wbr_ref, wo_ref, g_ref, b_ref, wr2_ref, rb_ref,
                  o_ref, xa_ref, gid_ref, h_ref, *, alpha):
    w = BRANCH_W
    tc, d = x_ref.shape

    @pl.when(pl.program_id(0) == 0)
    def _():
        h_ref[...] = jnp.zeros(h_ref.shape, F32)

    x1 = _layer_norm(h_ref[...], g_ref[...], b_ref[...])
    o_ref[...] = x1
    aux, g_row = _route(x1, wr2_ref[...], rb_ref[...])
    xa_ref[:, 0:d] = x1.astype(BF16)
    xa_ref[:, d:d + AUX_LANES] = aux.astype(BF16)
    gid_ref[...] = g_row

    x = x_ref[...]
    xb = x.astype(BF16)
    y_c = yct_ref[...].astype(F32).T.astype(BF16)
    branches = (yloc_ref[:, 0:w], yloc_ref[:, w:2 * w], y_c, yloc_ref[:, 2 * w:3 * w])
    merged = None
    for i in range(N_BRANCH):
        gate = jax.nn.sigmoid(jnp.dot(xb, wgate_ref[:, i * d:(i + 1) * d], preferred_element_type=F32))
        term = gate * jnp.dot(branches[i], wbr_ref[i], preferred_element_type=F32)
        merged = term if merged is None else merged + term
    mix = jnp.dot(merged.astype(BF16), wo_ref[...], preferred_element_type=F32)
    h_ref[...] = alpha * x + mix


def _merge(x, yloc, yct, w_gate_in, w_branch, w_o, ln_g, ln_b, wr2, r_bias, *, alpha, tc):
    n, d = x.shape
    w = BRANCH_W
    s = yct.shape[2]
    tiles_per_row = s // tc
    nt = n // tc
    cur = lambda i: jnp.minimum(i, nt - 1)
    prev = lambda i: jnp.maximum(i - 1, 0)
    tok_in = lambda width: pl.BlockSpec((tc, width), lambda i: (cur(i), 0))
    tok_out = lambda width: pl.BlockSpec((tc, width), lambda i: (prev(i), 0))
    const = lambda shape: pl.BlockSpec(shape, lambda i: (0,) * len(shape))
    return pl.pallas_call(
        functools.partial(_merge_kernel, alpha=alpha),
        grid=(nt + 1,),
        in_specs=[
            tok_in(d), tok_in(3 * w),
            pl.BlockSpec((None, w, tc), lambda i: (cur(i) // tiles_per_row, 0, cur(i) % tiles_per_row)),
            const((d, N_BRANCH * d)), const((N_BRANCH, w, d)), const((d, d)), const((1, d)), const((1, d)),
            const((2 * d, 2 * ROUTER_LANES)), const((1, ROUTER_LANES)),
        ],
        out_specs=[tok_out(d), tok_out(d + AUX_LANES), pl.BlockSpec((None, 1, tc), lambda i: (prev(i), 0, 0))],
        out_shape=[jax.ShapeDtypeStruct((n, d), F32), jax.ShapeDtypeStruct((n, d + AUX_LANES), BF16),
                   jax.ShapeDtypeStruct((n // tc, 1, tc), jnp.int32)],
        scratch_shapes=[pltpu.VMEM((tc, d), F32)],
        compiler_params=pltpu.CompilerParams(dimension_semantics=("arbitrary",),
                                             vmem_limit_bytes=V7X_VMEM_LIMIT),
        name="gated_merge",
    )(x, yloc, yct, w_gate_in, w_branch, w_o, ln_g, ln_b, wr2, r_bias)


DISPATCH_TILE = 512
DISPATCH_CHUNK = 16
DISPATCH_ROWS = DISPATCH_TILE + N_GROUPS * DISPATCH_CHUNK
COMBINE_ROWS = -(-DISPATCH_ROWS // 128) * 128
AUX_LANES = 128
MOE_TILE = 512
TAB_DEST, TAB_LOFF, TAB_NCH = 0, 1, 2
RUN_PIECES = tuple(1 << b for b in range((DISPATCH_TILE // DISPATCH_CHUNK).bit_length() - 1, -1, -1))


def _dispatch_tables(gid, n, tm):
    t, ch = DISPATCH_TILE, DISPATCH_CHUNK
    nt = n // t
    g = gid.reshape(nt, t)
    counts = jnp.sum((g[:, :, None] == jnp.arange(N_GROUPS, dtype=jnp.int32)).astype(jnp.int32), axis=1)
    nch = (counts + ch - 1) // ch
    rows = nch * ch
    loff = jnp.cumsum(rows, axis=1) - rows
    coff = jnp.cumsum(rows, axis=0) - rows
    gsize = jnp.sum(rows, axis=0)
    gpad = (gsize + tm - 1) // tm * tm
    gstart = jnp.cumsum(gpad) - gpad
    dest = gstart[None, :] + coff
    tab = jnp.concatenate([dest.reshape(-1), loff.reshape(-1), nch.reshape(-1)]).astype(jnp.int32)
    onehot = (g[:, :, None] == jnp.arange(N_GROUPS, dtype=jnp.int32)).astype(jnp.int32)
    rank = jnp.cumsum(onehot, axis=1) - onehot
    pos = jnp.sum(onehot * (rank + loff[:, None, :]), axis=2).astype(jnp.int32).reshape(nt, 1, t)
    cap = -(-(n + nt * N_GROUPS * ch + N_GROUPS * tm) // tm) * tm
    used = jnp.sum(gpad)
    gtab = jnp.concatenate([gstart + gsize, (gpad - gsize) // ch, used[None], (cap - used)[None] // tm])
    gtab = gtab.astype(jnp.int32)
    tile = jnp.arange(cap // tm, dtype=jnp.int32)
    tgroup = jnp.minimum(jnp.sum((tile[:, None] * tm >= (gstart + gpad)[None, :]).astype(jnp.int32), axis=1),
                         N_GROUPS - 1)
    tsrc = jnp.minimum(tile, jnp.sum(gpad) // tm - 1)
    return tab, gtab, pos, tgroup, tsrc, cap


def _tab(tab_ref, section, tile, grp):
    n_entries = tab_ref.shape[0] // 3
    return tab_ref[section * n_entries + tile * N_GROUPS + grp]


def _for_each_piece(n_chunks, fn):
    off = 0
    for p in RUN_PIECES:
        bit = n_chunks & p

        @pl.when(bit != 0)
        def _(p=p, off=off):
            fn(p, off)
        off = off + bit


def _permutation(pos_row, n_rows):
    rows = lax.broadcasted_iota(jnp.int32, (n_rows, pos_row.shape[1]), 0)
    return jnp.where(rows == pos_row, 1.0, 0.0)


def _dispatch_kernel(tab_ref, gtab_ref, pos_ref, xa_ref, xs_ref, buf_ref, zero_ref, sem_ref):
    i = pl.program_id(0)
    nt = pl.num_programs(0)
    slot = lax.rem(i, 2)
    ch = DISPATCH_CHUNK

    def run_copy(slot_, src_row, dst_row, pieces):
        rows = pieces * ch
        return pltpu.make_async_copy(buf_ref.at[slot_, pl.ds(src_row, rows), :],
                                     xs_ref.at[pl.ds(dst_row, rows), :], sem_ref.at[slot_])

    def wait_tile(tile, slot_):
        for c in range(N_GROUPS):
            _for_each_piece(_tab(tab_ref, TAB_NCH, tile, c), lambda p, off: run_copy(slot_, 0, 0, p).wait())

    @pl.when(i >= 2)
    def _():
        wait_tile(i - 2, slot)

    perm = _permutation(pos_ref[...], DISPATCH_ROWS).astype(BF16)
    buf_ref[slot] = jnp.dot(perm, xa_ref[...], preferred_element_type=F32).astype(BF16)
    for c in range(N_GROUPS):
        src0 = _tab(tab_ref, TAB_LOFF, i, c)
        dst0 = _tab(tab_ref, TAB_DEST, i, c)
        _for_each_piece(_tab(tab_ref, TAB_NCH, i, c),
                        lambda p, off, src0=src0, dst0=dst0: run_copy(
                            slot, pl.multiple_of(src0 + off * ch, ch), pl.multiple_of(dst0 + off * ch, ch), p).start())

    @pl.when(i == nt - 1)
    def _():
        zero_ref[...] = jnp.zeros(zero_ref.shape, BF16)

        def tail_copy(dst_row):
            return pltpu.make_async_copy(zero_ref.at[pl.ds(0, ch), :], xs_ref.at[pl.ds(dst_row, ch), :],
                                         sem_ref.at[2])

        tile_rows = zero_ref.shape[0]

        def spare_copy(dst_row):
            return pltpu.make_async_copy(zero_ref, xs_ref.at[pl.ds(dst_row, tile_rows), :], sem_ref.at[3])

        def fill_spare(k, carry):
            spare_copy(pl.multiple_of(gtab_ref[2 * N_GROUPS] + k * tile_rows, tile_rows)).start()
            return carry
        lax.fori_loop(0, gtab_ref[2 * N_GROUPS + 1], fill_spare, 0)

        def drain_spare(k, carry):
            spare_copy(0).wait()
            return carry
        lax.fori_loop(0, gtab_ref[2 * N_GROUPS + 1], drain_spare, 0)

        n_tail = 0
        for c in range(N_GROUPS):
            dst0 = gtab_ref[c]

            def fill(k, carry, dst0=dst0):
                tail_copy(pl.multiple_of(dst0 + k * ch, ch)).start()
                return carry
            lax.fori_loop(0, gtab_ref[N_GROUPS + c], fill, 0)
            n_tail = n_tail + gtab_ref[N_GROUPS + c]

        def drain(k, carry):
            tail_copy(0).wait()
            return carry
        lax.fori_loop(0, n_tail, drain, 0)

        wait_tile(i, slot)

        @pl.when(i >= 1)
        def _():
            wait_tile(i - 1, 1 - slot)


def _dispatch(tab, gtab, pos, xa, cap):
    n, width = xa.shape
    t = DISPATCH_TILE
    grid_spec = pltpu.PrefetchScalarGridSpec(
        num_scalar_prefetch=2,
        grid=(n // t,),
        in_specs=[
            pl.BlockSpec((None, 1, t), lambda i, tab_ref, gtab_ref: (i, 0, 0)),
            pl.BlockSpec((t, width), lambda i, tab_ref, gtab_ref: (i, 0)),
        ],
        out_specs=pl.BlockSpec(memory_space=pl.ANY),
        scratch_shapes=[pltpu.VMEM((2, DISPATCH_ROWS, width), BF16), pltpu.VMEM((MOE_TILE, width), BF16),
                        pltpu.SemaphoreType.DMA((4,))],
    )
    return pl.pallas_call(
        _dispatch_kernel,
        grid_spec=grid_spec,
        out_shape=jax.ShapeDtypeStruct((cap, width), BF16),
        compiler_params=pltpu.CompilerParams(dimension_semantics=("arbitrary",),
                                             vmem_limit_bytes=V7X_VMEM_LIMIT),
        name="moe_dispatch",
    )(tab, gtab, pos, xa)


def _expert_kernel(tg_ref, ts_ref, xs_ref, wg_ref, wu_ref, wd_ref, ys_ref):
    del tg_ref
    j = pl.program_id(0)
    tm = xs_ref.shape[0]
    d = wg_ref.shape[1]

    @pl.when(ts_ref[j] != j)
    def _():
        ys_ref[...] = jnp.zeros(ys_ref.shape, BF16)

    @pl.when(ts_ref[j] == j)
    def _():
        xt = xs_ref[:, 0:d]
        aux = xs_ref[:, d:d + AUX_LANES].astype(F32)
        lane = lax.broadcasted_iota(jnp.int32, (tm, AUX_LANES), 1)
        acts = []
        for e in range(EXPERTS_PER_GROUP):
            hg = jnp.dot(xt, wg_ref[e], preferred_element_type=F32)
            hu = jnp.dot(xt, wu_ref[e], preferred_element_type=F32)
            two_terms = (lane == e) | (lane == e + EXPERTS_PER_GROUP)
            ge = jnp.sum(jnp.where(two_terms, aux, 0.0), axis=-1, keepdims=True)
            acts.append(((hg * jax.nn.sigmoid(hg)) * hu * ge).astype(BF16))
        act = jnp.concatenate(acts, axis=1)
        ys_ref[...] = jnp.dot(act, wd_ref[...], preferred_element_type=F32).astype(BF16)


def _experts(tgroup, tsrc, xs, wg, wu, wd, *, tm):
    cap, width = xs.shape
    d = wg.shape[1]
    gh = GROUP_HIDDEN
    epg = EXPERTS_PER_GROUP
    grid_spec = pltpu.PrefetchScalarGridSpec(
        num_scalar_prefetch=2,
        grid=(cap // tm,),
        in_specs=[
            pl.BlockSpec((tm, width), lambda j, tg, ts: (ts[j], 0)),
            pl.BlockSpec((epg, d, EXPERT_HIDDEN), lambda j, tg, ts: (tg[j], 0, 0)),
            pl.BlockSpec((epg, d, EXPERT_HIDDEN), lambda j, tg, ts: (tg[j], 0, 0)),
            pl.BlockSpec((None, gh, d), lambda j, tg, ts: (tg[j], 0, 0)),
        ],
        out_specs=pl.BlockSpec((tm, d), lambda j, tg, ts: (j, 0)),
    )
    return pl.pallas_call(
        _expert_kernel,
        grid_spec=grid_spec,
        out_shape=jax.ShapeDtypeStruct((cap, d), BF16),
        compiler_params=pltpu.CompilerParams(dimension_semantics=("arbitrary",),
                                             vmem_limit_bytes=V7X_VMEM_LIMIT),
        name="moe_experts",
    )(tgroup, tsrc, xs, wg, wu, wd)


def _combine_kernel(tab_ref, pos_ref, x_ref, ys_ref, g_ref, b_ref, o_ref, buf_ref, h_ref, sem_ref, *, alpha):
    step = pl.program_id(0)
    nt = pl.num_programs(0) - 1
    i = jnp.minimum(step, nt - 1)
    slot = lax.rem(i, 2)
    ch = DISPATCH_CHUNK

    def run_copy(slot_, src_row, dst_row, pieces):
        rows = pieces * ch
        return pltpu.make_async_copy(ys_ref.at[pl.ds(src_row, rows), :],
                                     buf_ref.at[slot_, pl.ds(dst_row, rows), :], sem_ref.at[slot_])

    def fetch(tile, slot_):
        for c in range(N_GROUPS):
            src0 = _tab(tab_ref, TAB_DEST, tile, c)
            dst0 = _tab(tab_ref, TAB_LOFF, tile, c)
            _for_each_piece(_tab(tab_ref, TAB_NCH, tile, c),
                            lambda p, off, src0=src0, dst0=dst0: run_copy(
                                slot_, pl.multiple_of(src0 + off * ch, ch), pl.multiple_of(dst0 + off * ch, ch),
                                p).start())

    @pl.when(step == 0)
    def _():
        buf_ref[...] = jnp.zeros(buf_ref.shape, BF16)
        h_ref[...] = jnp.zeros(h_ref.shape, F32)
        fetch(0, 0)

    @pl.when(step + 1 < nt)
    def _():
        fetch(step + 1, 1 - slot)

    @pl.when(step < nt)
    def _():
        for c in range(N_GROUPS):
            _for_each_piece(_tab(tab_ref, TAB_NCH, i, c), lambda p, off: run_copy(slot, 0, 0, p).wait())

    o_ref[...] = _layer_norm(h_ref[...], g_ref[...], b_ref[...])
    perm_t = _permutation(pos_ref[...], COMBINE_ROWS).T.astype(BF16)
    y = jnp.dot(perm_t, buf_ref[slot], preferred_element_type=F32)
    h_ref[...] = alpha * x_ref[...] + y


def _combine(tab, pos, x1, ys, ln_g, ln_b, *, alpha):
    n, d = x1.shape
    t = DISPATCH_TILE
    nt = n // t
    cur = lambda i: jnp.minimum(i, nt - 1)
    prev = lambda i: jnp.maximum(i - 1, 0)
    grid_spec = pltpu.PrefetchScalarGridSpec(
        num_scalar_prefetch=1,
        grid=(nt + 1,),
        in_specs=[
            pl.BlockSpec((None, 1, t), lambda i, tab_ref: (cur(i), 0, 0)),
            pl.BlockSpec((t, d), lambda i, tab_ref: (cur(i), 0)),
            pl.BlockSpec(memory_space=pl.ANY),
            pl.BlockSpec((1, d), lambda i, tab_ref: (0, 0)),
            pl.BlockSpec((1, d), lambda i, tab_ref: (0, 0)),
        ],
        out_specs=pl.BlockSpec((t, d), lambda i, tab_ref: (prev(i), 0)),
        scratch_shapes=[pltpu.VMEM((2, COMBINE_ROWS, d), BF16), pltpu.VMEM((t, d), F32),
                        pltpu.SemaphoreType.DMA((2,))],
    )
    return pl.pallas_call(
        functools.partial(_combine_kernel, alpha=alpha),
        grid_spec=grid_spec,
        out_shape=jax.ShapeDtypeStruct((n, d), F32),
        compiler_params=pltpu.CompilerParams(dimension_semantics=("arbitrary",),
                                             vmem_limit_bytes=V7X_VMEM_LIMIT),
        name="moe_combine",
    )(tab, pos, x1, ys, ln_g, ln_b)


def _block_diag(blocks):
    n, r, c = blocks.shape
    eye = jnp.eye(n, dtype=blocks.dtype)
    return jnp.einsum("grc,gh->grhc", blocks, eye).reshape(n * r, n * c)


def kernel(x, w_in, pool_w, pool_scale, conv_w, lam_q1, lam_k1, lam_q2, lam_k2, subln_g, sg_ln_g, sg_ln_b, sg_w, sg_b, w_branch, w_o, ln1_g, ln1_b, w_rg, b_rg, w_re, b_re, w_gate, w_up, w_down, ln2_g, ln2_b):
    b, s, d = x.shape
    depth = w_in.shape[0]
    n = b * s
    w = BRANCH_W
    alpha = (2 * depth) ** 0.25
    ta = min(512, s)
    tq = min(512, s)
    tc = min(512, s)
    tm = MOE_TILE

    half = DA_HEAD_DIM // 2
    inv_freq = ROPE_THETA ** (-jnp.arange(half, dtype=F32) / half)
    ang = jnp.arange(s, dtype=F32)[:, None] * inv_freq[None, :]
    cos, sin = jnp.cos(ang), jnp.sin(ang)
    reps = w // DA_HEAD_DIM
    cos_t = jnp.tile(jnp.concatenate([cos, cos], axis=-1), (1, reps))
    sin_t = jnp.tile(jnp.concatenate([-sin, sin], axis=-1), (1, reps))

    for l in range(depth):
        lam_init = 0.8 - 0.6 * math.exp(-0.3 * l)
        w_loc = w_in[l, :, :COL_GATE].astype(BF16)
        w_gate_in = w_in[l, :, COL_GATE:].astype(BF16)
        pool_bd = _block_diag(pool_w[l]).astype(BF16)
        sg_wcat = jnp.transpose(sg_w[l], (1, 0, 2)).reshape(SG_CHUNK, SG_GROUPS * SG_CHUNK).astype(BF16)
        sg_bias = jnp.repeat(sg_b[l].T, SG_GW, axis=1)
        lam_rows = jnp.zeros((8, 128), F32).at[0:4, 0:DA_HEAD_DIM].set(
            jnp.stack([lam_q1[l], lam_k1[l], lam_q2[l], lam_k2[l]]).astype(F32))
        subln_cols = jnp.broadcast_to(subln_g[l][:, None], (DA_V_DIM, tq))
        w_router = jnp.concatenate([w_rg[l], jnp.transpose(w_re[l], (1, 0, 2)).reshape(d, N_EXPERTS)], axis=1)
        w_router = jnp.pad(w_router, ((0, 0), (0, ROUTER_LANES - w_router.shape[1])))
        wr_hi = w_router.astype(BF16)
        wr_lo = (w_router - wr_hi.astype(F32)).astype(BF16)
        wr2 = jnp.concatenate([jnp.concatenate([wr_hi, wr_lo], axis=1),
                               jnp.concatenate([wr_hi, jnp.zeros_like(wr_lo)], axis=1)], axis=0)
        r_bias = jnp.pad(jnp.concatenate([b_rg[l], b_re[l].reshape(-1)]), (0, ROUTER_LANES - N_GROUPS - N_EXPERTS))[None, :]
        wg = w_gate[l].astype(BF16)
        wu = w_up[l].astype(BF16)
        wd = w_down[l].reshape(N_GROUPS, GROUP_HIDDEN, d).astype(BF16)

        yloc, q, k, vt = _local_mixer(x, w_loc, cos_t, sin_t, pool_bd, pool_scale[l][None, :], conv_w[l],
                                      sg_ln_g[l][None, :], sg_ln_b[l][None, :], sg_wcat, sg_bias, ta=ta)
        yct = _diff_attn(lam_rows, q, k, vt, subln_cols, lam_init=lam_init, tq=tq)
        x1, xa, gid = _merge(x.reshape(n, d), yloc.reshape(n, 3 * w), yct, w_gate_in,
                             w_branch[l].astype(BF16), w_o[l].astype(BF16), ln1_g[l][None, :], ln1_b[l][None, :],
                             wr2, r_bias, alpha=alpha, tc=tc)
        tab, gtab, pos, tgroup, tsrc, cap = _dispatch_tables(gid, n, tm)
        xs = _dispatch(tab, gtab, pos, xa, cap)
        ys = _experts(tgroup, tsrc, xs, wg, wu, wd, tm=tm)
        x2 = _combine(tab, pos, x1, ys, ln2_g[l][None, :], ln2_b[l][None, :], alpha=alpha)
        x = x2.reshape(b, s, d)
    return x
```
